```python
import jax, jax.numpy as jnp
from jax import lax
import numpy as np

D_MODEL = 2048
BATCH = 8
SEQ = 4096
DEPTH = 4

N_MIXERS = 3
GRID_W = 64
ROPE_THETA = 500000.0
NORM_EPS = 1e-6
D_FF = -(-8 * D_MODEL // (3 * 256)) * 256
N_A = (DEPTH + 2) // 3
N_B = (DEPTH + 1) // 3
N_C = DEPTH // 3

A_HEADS = 16
A_KV_HEADS = 4
A_HEAD_DIM = D_MODEL // A_HEADS
A_ROT_DIM = A_HEAD_DIM // 4
WINDOW = 128
A_BLOCK = 128

B_HEADS = 16
B_NOPE_DIM = 128
B_ROPE_DIM = 64
B_V_DIM = 128
B_Q_RANK = 512
B_KV_RANK = 512
B_QBLOCK = 128

C_HEADS = 16
C_HEAD_DIM = D_MODEL // C_HEADS
NB_H_MAX = 8
NB_W = 16
C_QCOLS = 16
C_KCOLS = C_QCOLS + NB_W
N_CBLK = GRID_W // C_QCOLS

kernel_name = "hybrid_swa_mla_natten_encoder"


def rms_norm(x, g):
    xf = x.astype(jnp.float32)
    y = xf * lax.rsqrt(jnp.mean(xf * xf, axis=-1, keepdims=True) + NORM_EPS)
    return (y * g.astype(jnp.float32)).astype(x.dtype)


def rope_tables(seq, dim):
    pos = jnp.arange(seq, dtype=jnp.float32)
    inv = ROPE_THETA ** (-jnp.arange(0, dim, 2, dtype=jnp.float32) / dim)
    ang = pos[:, None] * inv[None, :]
    return jnp.cos(ang), jnp.sin(ang)


def apply_rope(x, cos, sin):
    half = x.shape[-1] // 2
    x1 = x[..., :half].astype(jnp.float32)
    x2 = x[..., half:].astype(jnp.float32)
    c, s = cos[:, None, :], sin[:, None, :]
    return jnp.concatenate([x1 * c - x2 * s, x2 * c + x1 * s], axis=-1).astype(x.dtype)


def window_gqa(x, w_qkv, sinks, w_o, cos, sin):
    B, S, _ = x.shape
    nb = S // A_BLOCK
    g = A_HEADS // A_KV_HEADS
    qkv = x @ w_qkv
    q, k, v = jnp.split(qkv, [A_HEADS * A_HEAD_DIM, (A_HEADS + A_KV_HEADS) * A_HEAD_DIM], axis=-1)
    q = q.reshape(B, S, A_HEADS, A_HEAD_DIM)
    k = k.reshape(B, S, A_KV_HEADS, A_HEAD_DIM)
    v = v.reshape(B, S, A_KV_HEADS, A_HEAD_DIM)
    q = jnp.concatenate([apply_rope(q[..., :A_ROT_DIM], cos, sin), q[..., A_ROT_DIM:]], axis=-1)
    k = jnp.concatenate([apply_rope(k[..., :A_ROT_DIM], cos, sin), k[..., A_ROT_DIM:]], axis=-1)
    qb = q.reshape(B, nb, A_BLOCK, A_KV_HEADS, g, A_HEAD_DIM)
    pad = ((0, 0), (A_BLOCK, A_BLOCK), (0, 0), (0, 0))
    kp = jnp.pad(k, pad).reshape(B, nb + 2, A_BLOCK, A_KV_HEADS, A_HEAD_DIM)
    vp = jnp.pad(v, pad).reshape(B, nb + 2, A_BLOCK, A_KV_HEADS, A_HEAD_DIM)
    kw = jnp.concatenate([kp[:, :-2], kp[:, 1:-1], kp[:, 2:]], axis=2)
    vw = jnp.concatenate([vp[:, :-2], vp[:, 1:-1], vp[:, 2:]], axis=2)
    a = np.arange(A_BLOCK)[:, None]
    j = np.arange(3 * A_BLOCK)[None, :]
    band = np.abs(j - A_BLOCK - a) <= WINDOW
    kabs = (np.arange(nb)[:, None] - 1) * A_BLOCK + np.arange(3 * A_BLOCK)[None, :]
    valid = band[None] & ((kabs >= 0) & (kabs < S))[:, None, :]
    scale = A_HEAD_DIM ** -0.5
    s = jnp.einsum('bnqkgd,bnjkd->bnkgqj', qb, kw).astype(jnp.float32) * scale
    s = jnp.where(valid[None, :, None, None, :, :], s, -jnp.inf)
    sink = sinks.astype(jnp.float32).reshape(A_KV_HEADS, g)[None, None, :, :, None, None]
    m = jnp.maximum(jnp.max(s, axis=-1, keepdims=True), sink)
    p = jnp.exp(s - m)
    probs = p / (jnp.sum(p, axis=-1, keepdims=True) + jnp.exp(sink - m))
    o = jnp.einsum('bnkgqj,bnjkd->bnqkgd', probs.astype(v.dtype), vw)
    return o.reshape(B, S, A_HEADS * A_HEAD_DIM) @ w_o


def latent_attention(x, w_down, q_norm, w_uq, kv_norm, w_ukv, w_o, cos, sin):
    B, S, _ = x.shape
    nb = S // B_QBLOCK
    c = x @ w_down
    cq, ckv, k_rope = jnp.split(c, [B_Q_RANK, B_Q_RANK + B_KV_RANK], axis=-1)
    cq = rms_norm(cq, q_norm)
    ckv = rms_norm(ckv, kv_norm)
    q = (cq @ w_uq).reshape(B, S, B_HEADS, B_NOPE_DIM + B_ROPE_DIM)
    q_nope, q_rope = q[..., :B_NOPE_DIM], apply_rope(q[..., B_NOPE_DIM:], cos, sin)
    k_rope = apply_rope(k_rope[:, :, None, :], cos, sin)[:, :, 0, :]
    kv = (ckv @ w_ukv).reshape(B, S, B_HEADS, B_NOPE_DIM + B_V_DIM)
    k_nope, v = kv[..., :B_NOPE_DIM], kv[..., B_NOPE_DIM:]
    scale = (B_NOPE_DIM + B_ROPE_DIM) ** -0.5
    qn_blocks = q_nope.reshape(B, nb, B_QBLOCK, B_HEADS, B_NOPE_DIM).transpose(1, 0, 2, 3, 4)
    qr_blocks = q_rope.reshape(B, nb, B_QBLOCK, B_HEADS, B_ROPE_DIM).transpose(1, 0, 2, 3, 4)

    def attend(blk):
        qn, qr = blk
        s = (jnp.einsum('bqhd,bkhd->bhqk', qn, k_nope)
             + jnp.einsum('bqhr,bkr->bhqk', qr, k_rope)).astype(jnp.float32) * scale
        p = jax.nn.softmax(s, axis=-1)
        return jnp.einsum('bhqk,bkhd->bqhd', p.astype(v.dtype), v)

    o = lax.map(attend, (qn_blocks, qr_blocks))
    o = o.transpose(1, 0, 2, 3, 4).reshape(B, S, B_HEADS * B_V_DIM)
    return o @ w_o


def neighborhood_attention(x, w_qkv, rel_bias, w_o):
    B, S, _ = x.shape
    rows = S // GRID_W
    wh = min(NB_H_MAX, rows)
    qkv = (x @ w_qkv).reshape(B, rows, GRID_W, 3, C_HEADS, C_HEAD_DIM)
    q, k, v = qkv[:, :, :, 0], qkv[:, :, :, 1], qkv[:, :, :, 2]
    qcol = np.arange(GRID_W).reshape(N_CBLK, C_QCOLS)
    kstart = np.clip(np.arange(N_CBLK) * C_QCOLS - NB_W // 2, 0, GRID_W - C_KCOLS)
    kcol = kstart[:, None] + np.arange(C_KCOLS)[None, :]
    cstart = np.clip(qcol - NB_W // 2, 0, GRID_W - NB_W)
    col_ok = (kcol[:, None, :] >= cstart[:, :, None]) & (kcol[:, None, :] < cstart[:, :, None] + NB_W)
    dcol_idx = np.clip(kcol[:, None, :] - qcol[:, :, None] + NB_W - 1, 0, 2 * NB_W - 2)
    bias_c = rel_bias[:, :, dcol_idx]
    mask = np.broadcast_to(col_ok[:, :, None, :], (N_CBLK, C_QCOLS, wh, C_KCOLS)).reshape(
        N_CBLK, C_QCOLS, wh * C_KCOLS)
    scale = C_HEAD_DIM ** -0.5
    q_rows = q.transpose(1, 0, 2, 3, 4)

    def row_step(args):
        r, qr = args
        rs = jnp.clip(r - wh // 2, 0, rows - wh)
        k_rows = lax.dynamic_slice_in_dim(k, rs, wh, axis=1)
        v_rows = lax.dynamic_slice_in_dim(v, rs, wh, axis=1)
        kb = k_rows[:, :, kcol].transpose(0, 2, 1, 3, 4, 5).reshape(
            B, N_CBLK, wh * C_KCOLS, C_HEADS, C_HEAD_DIM)
        vb = v_rows[:, :, kcol].transpose(0, 2, 1, 3, 4, 5).reshape(
            B, N_CBLK, wh * C_KCOLS, C_HEADS, C_HEAD_DIM)
        qb = qr.reshape(B, N_CBLK, C_QCOLS, C_HEADS, C_HEAD_DIM)
        dr_idx = rs - r + jnp.arange(wh) + NB_H_MAX - 1
        bias = jnp.take(bias_c, dr_idx, axis=1).transpose(0, 2, 3, 1, 4).reshape(
            C_HEADS, N_CBLK, C_QCOLS, wh * C_KCOLS)
        s = jnp.einsum('bjqhd,bjkhd->bhjqk', qb, kb).astype(jnp.float32) * scale
        s = jnp.where(mask, s + bias.astype(jnp.float32), -jnp.inf)
        p = jax.nn.softmax(s, axis=-1)
        o = jnp.einsum('bhjqk,bjkhd->bjqhd', p.astype(vb.dtype), vb)
        return o.reshape(B, GRID_W, C_HEADS * C_HEAD_DIM)

    o = lax.map(row_step, (jnp.arange(rows, dtype=jnp.int32), q_rows))
    o = o.transpose(1, 0, 2, 3).reshape(B, S, C_HEADS * C_HEAD_DIM)
    return o @ w_o


def swiglu(h, w_in, w_out):
    gate, up = jnp.split(h @ w_in, 2, axis=-1)
    return (jax.nn.silu(gate) * up) @ w_out


def setup_inputs(seed: int = 0) -> dict:
    key = jax.random.key(seed)
    ks = jax.random.split(key, 24)
    f32 = jnp.float32

    def w(k, shape, fan_in):
        return jax.random.normal(k, shape, f32) * (fan_in ** -0.5)

    def gain(k, shape):
        return 1.0 + 0.02 * jax.random.normal(k, shape, f32)

    a_qkv_w = (A_HEADS + 2 * A_KV_HEADS) * A_HEAD_DIM
    return {
        "x": jax.random.normal(ks[0], (BATCH, SEQ, D_MODEL), f32),
        "pre_mix_norm": gain(ks[1], (DEPTH, D_MODEL)),
        "post_mix_norm": gain(ks[2], (DEPTH, D_MODEL)),
        "pre_ffn_norm": gain(ks[3], (DEPTH, D_MODEL)),
        "post_ffn_norm": gain(ks[4], (DEPTH, D_MODEL)),
        "a_w_qkv": w(ks[5], (N_A, D_MODEL, a_qkv_w), D_MODEL),
        "a_sinks": 0.5 * jax.random.normal(ks[6], (N_A, A_HEADS), f32),
        "a_w_o": w(ks[7], (N_A, A_HEADS * A_HEAD_DIM, D_MODEL), A_HEADS * A_HEAD_DIM),
        "b_w_down": w(ks[8], (N_B, D_MODEL, B_Q_RANK + B_KV_RANK + B_ROPE_DIM), D_MODEL),
        "b_q_norm": gain(ks[9], (N_B, B_Q_RANK)),
        "b_w_uq": w(ks[10], (N_B, B_Q_RANK, B_HEADS * (B_NOPE_DIM + B_ROPE_DIM)), B_Q_RANK),
        "b_kv_norm": gain(ks[11], (N_B, B_KV_RANK)),
        "b_w_ukv": w(ks[12], (N_B, B_KV_RANK, B_HEADS * (B_NOPE_DIM + B_V_DIM)), B_KV_RANK),
        "b_w_o": w(ks[13], (N_B, B_HEADS * B_V_DIM, D_MODEL), B_HEADS * B_V_DIM),
        "c_w_qkv": w(ks[14], (N_C, D_MODEL, 3 * C_HEADS * C_HEAD_DIM), D_MODEL),
        "c_rel_bias": 0.1 * jax.random.normal(ks[15], (N_C, C_HEADS, 2 * NB_H_MAX - 1, 2 * NB_W - 1), f32),
        "c_w_o": w(ks[16], (N_C, C_HEADS * C_HEAD_DIM, D_MODEL), C_HEADS * C_HEAD_DIM),
        "ffn_w_in": w(ks[17], (DEPTH, D_MODEL, 2 * D_FF), D_MODEL),
        "ffn_w_out": w(ks[18], (DEPTH, D_FF, D_MODEL), D_FF),
    }


def reference(x, pre_mix_norm, post_mix_norm, pre_ffn_norm, post_ffn_norm,
              a_w_qkv, a_sinks, a_w_o,
              b_w_down, b_q_norm, b_w_uq, b_kv_norm, b_w_ukv, b_w_o,
              c_w_qkv, c_rel_bias, c_w_o,
              ffn_w_in, ffn_w_out):
    S = x.shape[1]
    cos_a, sin_a = rope_tables(S, A_ROT_DIM)
    cos_b, sin_b = rope_tables(S, B_ROPE_DIM)
    for i in range(DEPTH):
        kind, slot = i % N_MIXERS, i // N_MIXERS
        h = rms_norm(x, pre_mix_norm[i])
        if kind == 0:
            m = window_gqa(h, a_w_qkv[slot], a_sinks[slot], a_w_o[slot], cos_a, sin_a)
        elif kind == 1:
            m = latent_attention(h, b_w_down[slot], b_q_norm[slot], b_w_uq[slot],
                                 b_kv_norm[slot], b_w_ukv[slot], b_w_o[slot], cos_b, sin_b)
        else:
            m = neighborhood_attention(h, c_w_qkv[slot], c_rel_bias[slot], c_w_o[slot])
        x = x + rms_norm(m, post_mix_norm[i])
        h = rms_norm(x, pre_ffn_norm[i])
        x = x + rms_norm(swiglu(h, ffn_w_in[i], ffn_w_out[i]), post_ffn_norm[i])
    return x
```

```python
import functools

import jax
import jax.numpy as jnp
import numpy as np
from jax import lax
from jax.experimental import pallas as pl
from jax.experimental.pallas import tpu as pltpu

D_MODEL = 2048
DEPTH = 4
N_MIXERS = 3
GRID_W = 64
ROPE_THETA = 500000.0
NORM_EPS = 1e-6
D_FF = 5632

A_HEADS = 16
A_KV_HEADS = 4
A_HEAD_DIM = 128
A_ROT_DIM = 32
WINDOW = 128
A_BLOCK = 128

B_HEADS = 16
B_NOPE_DIM = 128
B_ROPE_DIM = 64
B_V_DIM = 128
B_Q_RANK = 512
B_KV_RANK = 512
B_QK_PAD = 256

C_HEADS = 16
C_HEAD_DIM = 128
NB_H = 8
NB_W = 16
C_QROWS = 8
C_KROWS = 16

LANES = 128
V7X_VMEM_LIMIT_BYTES = 56 * 1024 * 1024

NEG_BIG = -1e30
BF16 = jnp.bfloat16
F32 = jnp.float32


def _cparams(semantics):
    return pltpu.CompilerParams(dimension_semantics=semantics,
                                vmem_limit_bytes=V7X_VMEM_LIMIT_BYTES)


def _rms(xf, g):
    ms = jnp.mean(xf * xf, axis=-1, keepdims=True)
    return xf * lax.rsqrt(ms + NORM_EPS) * g


def _dot(a, b):
    return jnp.dot(a, b, preferred_element_type=F32)


def _dot_nt(a, b):
    return lax.dot_general(a, b, (((1,), (1,)), ((), ())), preferred_element_type=F32)


def _rope(y, cos, sa, sb, shift):
    n = y.shape[-1]
    return y * cos + pltpu.roll(y, shift, 1) * sa + pltpu.roll(y, n - shift, 1) * sb


def _tile_lanes(t, reps):
    return t if reps == 1 else jnp.concatenate([t] * reps, axis=1)


def _norm_proj_kernel(x_ref, g_ref, w_ref, cos_ref, sa_ref, sb_ref, o_ref, h_ref, *,
                      rope_tiles, scale_tiles, scale, shift):
    j = pl.program_id(1)

    @pl.when(j == 0)
    def _():
        h_ref[...] = _rms(x_ref[...], g_ref[...]).astype(BF16)

    y = _dot(h_ref[...], w_ref[...])
    y = y * jnp.where(j < scale_tiles, scale, 1.0).astype(F32)
    if rope_tiles == 0:
        o_ref[...] = y.astype(o_ref.dtype)
    else:
        reps = y.shape[-1] // cos_ref.shape[-1]

        @pl.when(j < rope_tiles)
        def _():
            o_ref[...] = _rope(y, _tile_lanes(cos_ref[...], reps), _tile_lanes(sa_ref[...], reps),
                               _tile_lanes(sb_ref[...], reps), shift).astype(o_ref.dtype)

        @pl.when(j >= rope_tiles)
        def _():
            o_ref[...] = y.astype(o_ref.dtype)


def norm_proj(x, g, w, tabs, *, seq, tm, tn, rope_tiles, scale_tiles, scale, shift):
    m, d = x.shape
    n = w.shape[1]
    cos, sa, sb = tabs
    tw = cos.shape[1]
    sblk = seq // tm
    kern = functools.partial(_norm_proj_kernel, rope_tiles=rope_tiles, scale_tiles=scale_tiles,
                             scale=scale, shift=shift)
    tab_spec = pl.BlockSpec((tm, tw), lambda i, j: (i % sblk, 0))
    return pl.pallas_call(
        kern,
        grid=(m // tm, n // tn),
        in_specs=[pl.BlockSpec((tm, d), lambda i, j: (i, 0)),
                  pl.BlockSpec((1, d), lambda i, j: (0, 0)),
                  pl.BlockSpec((d, tn), lambda i, j: (0, j)),
                  tab_spec, tab_spec, tab_spec],
        out_specs=pl.BlockSpec((tm, tn), lambda i, j: (i, j)),
        out_shape=jax.ShapeDtypeStruct((m, n), BF16),
        scratch_shapes=[pltpu.VMEM((tm, d), BF16)],
        compiler_params=_cparams(("parallel", "arbitrary")),
        name="norm_proj",
    )(x, g.reshape(1, d), w, cos, sa, sb)


def _proj_kernel(a_ref, w_ref, cos_ref, sa_ref, sb_ref, o_ref, *, rope, scale, shift):
    y = _dot(a_ref[...], w_ref[...])
    if rope:
        reps = y.shape[-1] // cos_ref.shape[-1]
        y = _rope(y * scale, _tile_lanes(cos_ref[...], reps), _tile_lanes(sa_ref[...], reps),
                  _tile_lanes(sb_ref[...], reps), shift)
    o_ref[...] = y.astype(o_ref.dtype)


def proj(a, w, tabs, *, seq, tm, tn, rope, scale, shift):
    m, k = a.shape
    n = w.shape[1]
    cos, sa, sb = tabs
    tw = cos.shape[1]
    sblk = seq // tm
    kern = functools.partial(_proj_kernel, rope=rope, scale=scale, shift=shift)
    tab_spec = pl.BlockSpec((tm, tw), lambda i, j: (i % sblk, 0))
    return pl.pallas_call(
        kern,
        grid=(m // tm, n // tn),
        in_specs=[pl.BlockSpec((tm, k), lambda i, j: (i, 0)),
                  pl.BlockSpec((k, tn), lambda i, j: (0, j)),
                  tab_spec, tab_spec, tab_spec],
        out_specs=pl.BlockSpec((tm, tn), lambda i, j: (i, j)),
        out_shape=jax.ShapeDtypeStruct((m, n), BF16),
        compiler_params=_cparams(("parallel", "arbitrary")),
        name="proj",
    )(a, w, cos, sa, sb)


def _out_proj_kernel(a_ref, w_ref, g_ref, x_ref, o_ref):
    m = _dot(a_ref[...], w_ref[...])
    o_ref[...] = x_ref[...] + _rms(m, g_ref[...])


def out_proj_residual(a, w, g, x, *, tm):
    m, k = a.shape
    d = w.shape[1]
    return pl.pallas_call(
        _out_proj_kernel,
        grid=(m // tm,),
        in_specs=[pl.BlockSpec((tm, k), lambda i: (i, 0)),
                  pl.BlockSpec((k, d), lambda i: (0, 0)),
                  pl.BlockSpec((1, d), lambda i: (0, 0)),
                  pl.BlockSpec((tm, d), lambda i: (i, 0))],
        out_specs=pl.BlockSpec((tm, d), lambda i: (i, 0)),
        out_shape=jax.ShapeDtypeStruct((m, d), F32),
        compiler_params=_cparams(("parallel",)),
        name="out_proj_residual",
    )(a, w, g.reshape(1, d), x)


def _ffn_kernel(x_ref, gpre_ref, wg_ref, wu_ref, wo_ref, gpost_ref, o_ref, h_ref):
    j = pl.program_id(1)

    @pl.when(j == 0)
    def _():
        h_ref[...] = _rms(x_ref[...], gpre_ref[...]).astype(BF16)
        o_ref[...] = jnp.zeros_like(o_ref)

    h = h_ref[...]
    gate = _dot(h, wg_ref[...])
    up = _dot(h, wu_ref[...])
    act = (gate / (1.0 + jnp.exp(-gate)) * up).astype(BF16)
    o_ref[...] += _dot(act, wo_ref[...])

    @pl.when(j == pl.num_programs(1) - 1)
    def _():
        o_ref[...] = x_ref[...] + _rms(o_ref[...], gpost_ref[...])


def ffn(x, gpre, w_in, w_out, gpost, *, tm, tf):
    m, d = x.shape
    dff = w_out.shape[0]
    nf = dff // tf
    return pl.pallas_call(
        _ffn_kernel,
        grid=(m // tm, nf),
        in_specs=[pl.BlockSpec((tm, d), lambda i, j: (i, 0)),
                  pl.BlockSpec((1, d), lambda i, j: (0, 0)),
                  pl.BlockSpec((d, tf), lambda i, j: (0, j)),
                  pl.BlockSpec((d, tf), lambda i, j: (0, nf + j)),
                  pl.BlockSpec((tf, d), lambda i, j: (j, 0)),
                  pl.BlockSpec((1, d), lambda i, j: (0, 0))],
        out_specs=pl.BlockSpec((tm, d), lambda i, j: (i, 0)),
        out_shape=jax.ShapeDtypeStruct((m, d), F32),
        scratch_shapes=[pltpu.VMEM((tm, d), BF16)],
        compiler_params=_cparams(("parallel", "arbitrary")),
        name="ffn",
    )(x, gpre.reshape(1, d), w_in, w_in, w_out, gpost.reshape(1, d))


def _win_attn_kernel(sink_ref, q_ref, k_ref, v_ref, o_ref, *, seq):
    i = pl.program_id(1)
    group = A_HEADS // A_KV_HEADS
    kwin = 3 * A_BLOCK
    start = pl.multiple_of(jnp.clip((i - 1) * A_BLOCK, 0, seq - kwin), A_BLOCK)
    rows = group * A_BLOCK
    qpos = i * A_BLOCK + (lax.broadcasted_iota(jnp.int32, (rows, kwin), 0) & (A_BLOCK - 1))
    kpos = start + lax.broadcasted_iota(jnp.int32, (rows, kwin), 1)
    valid = jnp.abs(kpos - qpos) <= WINDOW
    for kv in range(A_KV_HEADS):
        q4 = jnp.concatenate(
            [q_ref[0, :, (kv * group + gi) * A_HEAD_DIM:(kv * group + gi + 1) * A_HEAD_DIM]
             for gi in range(group)], axis=0)
        kw = k_ref[0, pl.ds(start, kwin), kv * A_HEAD_DIM:(kv + 1) * A_HEAD_DIM]
        vw = v_ref[0, pl.ds(start, kwin), kv * A_HEAD_DIM:(kv + 1) * A_HEAD_DIM]
        s = jnp.where(valid, _dot_nt(q4, kw), NEG_BIG)
        sink = jnp.concatenate(
            [jnp.full((A_BLOCK, 1), sink_ref[kv * group + gi], F32) for gi in range(group)], axis=0)
        mx = jnp.maximum(jnp.max(s, axis=-1, keepdims=True), sink)
        p = jnp.exp(s - mx)
        denom = jnp.sum(p, axis=-1, keepdims=True) + jnp.exp(sink - mx)
        o = _dot(p.astype(BF16), vw) / denom
        for gi in range(group):
            h = kv * group + gi
            o_ref[0, :, h * A_HEAD_DIM:(h + 1) * A_HEAD_DIM] = (
                o[gi * A_BLOCK:(gi + 1) * A_BLOCK, :].astype(o_ref.dtype))


def window_attention(qkv, sinks):
    b, s, _ = qkv.shape
    dq = A_HEADS * A_HEAD_DIM
    dkv = A_KV_HEADS * A_HEAD_DIM
    return pl.pallas_call(
        functools.partial(_win_attn_kernel, seq=s),
        grid=(b, s // A_BLOCK),
        in_specs=[pl.BlockSpec(memory_space=pltpu.SMEM),
                  pl.BlockSpec((1, A_BLOCK, dq), lambda bi, i: (bi, i, 0)),
                  pl.BlockSpec((1, s, dkv), lambda bi, i: (bi, 0, dq // dkv)),
                  pl.BlockSpec((1, s, dkv), lambda bi, i: (bi, 0, dq // dkv + 1))],
        out_specs=pl.BlockSpec((1, A_BLOCK, dq), lambda bi, i: (bi, i, 0)),
        out_shape=jax.ShapeDtypeStruct((b, s, dq), BF16),
        compiler_params=_cparams(("parallel", "arbitrary")),
        name="window_attention",
    )(sinks, qkv, qkv, qkv)


def _mla_down_kernel(x_ref, g_ref, wq_ref, wkv_ref, wkr_ref, qn_ref, kvn_ref,
                     cos_ref, sa_ref, sb_ref, cq_ref, ckv_ref):
    h = _rms(x_ref[...], g_ref[...]).astype(BF16)
    cq_ref[...] = _rms(_dot(h, wq_ref[...]), qn_ref[...]).astype(BF16)
    ckv_ref[:, :B_KV_RANK] = _rms(_dot(h, wkv_ref[...]), kvn_ref[...]).astype(BF16)
    kr = _dot(h, wkr_ref[...])
    ckv_ref[:, B_KV_RANK:] = _rope(kr, cos_ref[...], sa_ref[...], sb_ref[...],
                                   B_ROPE_DIM // 2).astype(BF16)


def mla_down(x, g, wq, wkv, wkr, qn, kvn, tabs, *, seq, tm):
    m, d = x.shape
    cos, sa, sb = tabs
    sblk = seq // tm
    tab_spec = pl.BlockSpec((tm, LANES), lambda i: (i % sblk, 0))
    full = lambda shape: pl.BlockSpec(shape, lambda i: (0, 0))
    return pl.pallas_call(
        _mla_down_kernel,
        grid=(m // tm,),
        in_specs=[pl.BlockSpec((tm, d), lambda i: (i, 0)), full((1, d)),
                  full((d, B_Q_RANK)), full((d, B_KV_RANK)), full((d, LANES)),
                  full((1, B_Q_RANK)), full((1, B_KV_RANK)),
                  tab_spec, tab_spec, tab_spec],
        out_specs=[pl.BlockSpec((tm, B_Q_RANK), lambda i: (i, 0)),
                   pl.BlockSpec((tm, B_KV_RANK + LANES), lambda i: (i, 0))],
        out_shape=[jax.ShapeDtypeStruct((m, B_Q_RANK), BF16),
                   jax.ShapeDtypeStruct((m, B_KV_RANK + LANES), BF16)],
        compiler_params=_cparams(("parallel",)),
        name="mla_down",
    )(x, g.reshape(1, d), wq, wkv, wkr, qn.reshape(1, -1), kvn.reshape(1, -1), cos, sa, sb)


def _mla_attn_kernel(q_ref, k_ref, v_ref, o_ref):
    s = _dot_nt(q_ref[0], k_ref[0])
    mx = jnp.max(s, axis=-1, keepdims=True)
    p = jnp.exp(s - mx)
    denom = jnp.sum(p, axis=-1, keepdims=True)
    o_ref[0] = (_dot(p.astype(BF16), v_ref[0]) / denom).astype(o_ref.dtype)


def mla_attention(qcat, kvcat, *, tq):
    b, s, _ = qcat.shape
    vblk0 = B_HEADS * B_QK_PAD // B_V_DIM
    return pl.pallas_call(
        _mla_attn_kernel,
        grid=(b, B_HEADS, s // tq),
        in_specs=[pl.BlockSpec((1, tq, B_QK_PAD), lambda bi, h, qi: (bi, qi, h)),
                  pl.BlockSpec((1, s, B_QK_PAD), lambda bi, h, qi: (bi, 0, h)),
                  pl.BlockSpec((1, s, B_V_DIM), lambda bi, h, qi: (bi, 0, vblk0 + h))],
        out_specs=pl.BlockSpec((1, tq, B_V_DIM), lambda bi, h, qi: (bi, qi, h)),
        out_shape=jax.ShapeDtypeStruct((b, s, B_HEADS * B_V_DIM), BF16),
        compiler_params=_cparams(("parallel", "parallel", "arbitrary")),
        name="mla_attention",
    )(qcat, kvcat, kvcat)


def _nb_attn_kernel(q_ref, k_ref, v_ref, t_ref, o_ref, *, rows):
    rb = pl.program_id(2)
    nk = C_KROWS * GRID_W
    start = pl.multiple_of(
        jnp.clip(rb * C_QROWS - NB_H // 2, 0, rows - C_KROWS) * GRID_W, (NB_H // 2) * GRID_W)
    kw = k_ref[0, pl.ds(start, nk), :]
    vw = v_ref[0, pl.ds(start, nk), :]
    s = _dot_nt(q_ref[0], kw) + t_ref[0, 0]
    mx = jnp.max(s, axis=-1, keepdims=True)
    p = jnp.exp(s - mx)
    denom = jnp.sum(p, axis=-1, keepdims=True)
    o_ref[0] = (_dot(p.astype(BF16), vw) / denom).astype(o_ref.dtype)


def neighborhood_attention(qkv, table):
    b, s, _ = qkv.shape
    rows = s // GRID_W
    nrb = rows // C_QROWS
    tq = C_QROWS * GRID_W
    nk = C_KROWS * GRID_W

    def t_index(bi, h, rb):
        return (h, jnp.where(rb == 0, 0, jnp.where(rb == nrb - 1, 2, 1)), 0, 0)

    return pl.pallas_call(
        functools.partial(_nb_attn_kernel, rows=rows),
        grid=(b, C_HEADS, nrb),
        in_specs=[pl.BlockSpec((1, tq, C_HEAD_DIM), lambda bi, h, rb: (bi, rb, h)),
                  pl.BlockSpec((1, s, C_HEAD_DIM), lambda bi, h, rb: (bi, 0, C_HEADS + h)),
                  pl.BlockSpec((1, s, C_HEAD_DIM), lambda bi, h, rb: (bi, 0, 2 * C_HEADS + h)),
                  pl.BlockSpec((1, 1, tq, nk), t_index)],
        out_specs=pl.BlockSpec((1, tq, C_HEAD_DIM), lambda bi, h, rb: (bi, rb, h)),
        out_shape=jax.ShapeDtypeStruct((b, s, C_HEADS * C_HEAD_DIM), BF16),
        compiler_params=_cparams(("parallel", "parallel", "arbitrary")),
        name="neighborhood_attention",
    )(qkv, qkv, qkv, table)


def _nb_bias_table(rel_bias, rows):
    r0s = np.array([0, C_QROWS, rows - C_QROWS])
    w0s = np.clip(r0s - NB_H // 2, 0, rows - C_KROWS)
    i = np.arange(C_QROWS)[None, :, None, None, None]
    c = np.arange(GRID_W)[None, None, :, None, None]
    kl = np.arange(C_KROWS)[None, None, None, :, None]
    kc = np.arange(GRID_W)[None, None, None, None, :]
    r = r0s[:, None, None, None, None] + i
    kr = w0s[:, None, None, None, None] + kl
    rs = np.clip(r - NB_H // 2, 0, rows - NB_H)
    cs = np.clip(c - NB_W // 2, 0, GRID_W - NB_W)
    valid = (kr >= rs) & (kr < rs + NB_H) & (kc >= cs) & (kc < cs + NB_W)
    dr = np.clip(kr - r + NB_H - 1, 0, 2 * NB_H - 2)
    dc = np.clip(kc - c + NB_W - 1, 0, 2 * NB_W - 2)
    shape = (3, C_QROWS, GRID_W, C_KROWS, GRID_W)
    valid = np.broadcast_to(valid, shape).reshape(3, C_QROWS * GRID_W, C_KROWS * GRID_W)
    dr = np.broadcast_to(dr, shape).reshape(valid.shape)
    dc = np.broadcast_to(dc, shape).reshape(valid.shape)
    bias = rel_bias.astype(F32)[:, dr, dc]
    return jnp.where(valid[None], bias, NEG_BIG)


def _rope_tables(seq, dim, lane0, width):
    half = dim // 2
    pos = jnp.arange(seq, dtype=F32)
    inv = ROPE_THETA ** (-jnp.arange(0, dim, 2, dtype=F32) / dim)
    ang = pos[:, None] * inv[None, :]
    c, sn = jnp.cos(ang), jnp.sin(ang)
    cos = jnp.ones((seq, width), F32).at[:, lane0:lane0 + dim].set(jnp.concatenate([c, c], 1))
    sa = jnp.zeros((seq, width), F32).at[:, lane0 + half:lane0 + dim].set(sn)
    sb = jnp.zeros((seq, width), F32).at[:, lane0:lane0 + half].set(-sn)
    return cos, sa, sb


def kernel(x, pre_mix_norm, post_mix_norm, pre_ffn_norm, post_ffn_norm, a_w_qkv, a_sinks, a_w_o,
           b_w_down, b_q_norm, b_w_uq, b_kv_norm, b_w_ukv, b_w_o, c_w_qkv, c_rel_bias, c_w_o,
           ffn_w_in, ffn_w_out):
    bsz, seq, d = x.shape
    m = bsz * seq
    rows = seq // GRID_W
    tabs_a = _rope_tables(seq, A_ROT_DIM, 0, LANES)
    tabs_bk = _rope_tables(seq, B_ROPE_DIM, 0, LANES)
    tabs_bq = _rope_tables(seq, B_ROPE_DIM, B_NOPE_DIM, B_QK_PAD)

    xf = x.reshape(m, d)
    for i in range(DEPTH):
        kind, slot = i % N_MIXERS, i // N_MIXERS
        if kind == 0:
            qkv = norm_proj(xf, pre_mix_norm[i], a_w_qkv[slot].astype(BF16), tabs_a, seq=seq,
                            tm=1024, tn=512, rope_tiles=5, scale_tiles=4,
                            scale=A_HEAD_DIM ** -0.5, shift=A_ROT_DIM // 2)
            o = window_attention(qkv.reshape(bsz, seq, -1), a_sinks[slot].astype(F32))
            w_o = a_w_o[slot]
        elif kind == 1:
            wd = b_w_down[slot].astype(BF16)
            wkr = jnp.pad(wd[:, B_Q_RANK + B_KV_RANK:], ((0, 0), (0, LANES - B_ROPE_DIM)))
            cq, ckv = mla_down(xf, pre_mix_norm[i], wd[:, :B_Q_RANK],
                               wd[:, B_Q_RANK:B_Q_RANK + B_KV_RANK], wkr,
                               b_q_norm[slot], b_kv_norm[slot], tabs_bk, seq=seq, tm=512)
            wuq = b_w_uq[slot].astype(BF16).reshape(B_Q_RANK, B_HEADS, B_NOPE_DIM + B_ROPE_DIM)
            wuq = jnp.pad(wuq, ((0, 0), (0, 0), (0, B_QK_PAD - B_NOPE_DIM - B_ROPE_DIM)))
            qcat = proj(cq, wuq.reshape(B_Q_RANK, B_HEADS * B_QK_PAD), tabs_bq, seq=seq,
                        tm=1024, tn=1024, rope=True,
                        scale=(B_NOPE_DIM + B_ROPE_DIM) ** -0.5, shift=B_ROPE_DIM // 2)
            wukv = b_w_ukv[slot].astype(BF16).reshape(B_KV_RANK, B_HEADS, B_NOPE_DIM + B_V_DIM)
            eye = jnp.eye(LANES, dtype=BF16)[:, :B_QK_PAD - B_NOPE_DIM]
            eye = eye.at[B_ROPE_DIM:].set(0)
            wk = jnp.concatenate(
                [jnp.pad(wukv[:, :, :B_NOPE_DIM], ((0, LANES), (0, 0), (0, 0))),
                 jnp.pad(jnp.broadcast_to(eye[:, None, :], (LANES, B_HEADS, eye.shape[1])),
                         ((B_KV_RANK, 0), (0, 0), (0, 0)))], axis=2)
            wv = jnp.pad(wukv[:, :, B_NOPE_DIM:], ((0, LANES), (0, 0), (0, 0)))
            wkv = jnp.concatenate([wk.reshape(B_KV_RANK + LANES, -1),
                                   wv.reshape(B_KV_RANK + LANES, -1)], axis=1)
            kvcat = proj(ckv, wkv, tabs_bq, seq=seq, tm=1024, tn=1024, rope=False, scale=1.0,
                         shift=0)
            o = mla_attention(qcat.reshape(bsz, seq, -1), kvcat.reshape(bsz, seq, -1), tq=256)
            w_o = b_w_o[slot]
        else:
            qkv = norm_proj(xf, pre_mix_norm[i], c_w_qkv[slot].astype(BF16), tabs_a, seq=seq,
                            tm=1024, tn=512, rope_tiles=0, scale_tiles=4,
                            scale=C_HEAD_DIM ** -0.5, shift=0)
            table = _nb_bias_table(c_rel_bias[slot], rows)
            o = neighborhood_attention(qkv.reshape(bsz, seq, -1), table)
            w_o = c_w_o[slot]
        xf = out_proj_residual(o.reshape(m, -1), w_o.astype(BF16), post_mix_norm[i], xf, tm=512)
        xf = ffn(xf, pre_ffn_norm[i], ffn_w_in[i].astype(BF16), ffn_w_out[i].astype(BF16),
                 post_ffn_norm[i], tm=512, tf=512)
    return xf.reshape(bsz, seq, d)
```

```python
import functools

import jax
import jax.numpy as jnp
import numpy as np
from jax import lax
from jax.experimental import pallas as pl
from jax.experimental.pallas import tpu as pltpu

D_MODEL = 2048
DEPTH = 4
N_MIXERS = 3
GRID_W = 64
ROPE_THETA = 500000.0
NORM_EPS = 1e-6
D_FF = 5632

A_HEADS = 16
A_KV_HEADS = 4
A_HEAD_DIM = 128
A_ROT_DIM = 32
WINDOW = 128
A_BLOCK = 128

B_HEADS = 16
B_NOPE_DIM = 128
B_ROPE_DIM = 64
B_V_DIM = 128
B_Q_RANK = 512
B_KV_RANK = 512
B_QK_PAD = 256

C_HEADS = 16
C_HEAD_DIM = 128
NB_H = 8
NB_W = 16
C_QROWS = 8
C_KROWS = 16

LANES = 128
V7X_VMEM_LIMIT_BYTES = 56 * 1024 * 1024

NEG_BIG = -1e30
BF16 = jnp.bfloat16
F32 = jnp.float32


def _cparams(semantics):
    return pltpu.CompilerParams(dimension_semantics=semantics,
                                vmem_limit_bytes=V7X_VMEM_LIMIT_BYTES)


def _rms(xf, g):
    ms = jnp.mean(xf * xf, axis=-1, keepdims=True)
    return xf * lax.rsqrt(ms + NORM_EPS) * g


def _dot(a, b):
    return jnp.dot(a, b, preferred_element_type=F32)


def _dot_nt(a, b):
    return lax.dot_general(a, b, (((1,), (1,)), ((), ())), preferred_element_type=F32)


def _rope(y, cos, sa, sb, shift):
    n = y.shape[-1]
    return y * cos + pltpu.roll(y, shift, 1) * sa + pltpu.roll(y, n - shift, 1) * sb


def _tile_lanes(t, reps):
    return t if reps == 1 else jnp.concatenate([t] * reps, axis=1)


def _norm_proj_kernel(x_ref, g_ref, w_ref, cos_ref, sa_ref, sb_ref, o_ref, h_ref, *,
                      rope_tiles, scale_tiles, scale, shift):
    j = pl.program_id(1)

    @pl.when(j == 0)
    def _():
        h_ref[...] = _rms(x_ref[...], g_ref[...]).astype(BF16)

    y = _dot(h_ref[...], w_ref[...])
    y = y * jnp.where(j < scale_tiles, scale, 1.0).astype(F32)
    if rope_tiles == 0:
        o_ref[...] = y.astype(o_ref.dtype)
    else:
        reps = y.shape[-1] // cos_ref.shape[-1]

        @pl.when(j < rope_tiles)
        def _():
            o_ref[...] = _rope(y, _tile_lanes(cos_ref[...], reps), _tile_lanes(sa_ref[...], reps),
                               _tile_lanes(sb_ref[...], reps), shift).astype(o_ref.dtype)

        @pl.when(j >= rope_tiles)
        def _():
            o_ref[...] = y.astype(o_ref.dtype)


def norm_proj(x, g, w, tabs, *, seq, tm, tn, rope_tiles, scale_tiles, scale, shift):
    m, d = x.shape
    n = w.shape[1]
    cos, sa, sb = tabs
    tw = cos.shape[1]
    sblk = seq // tm
    kern = functools.partial(_norm_proj_kernel, rope_tiles=rope_tiles, scale_tiles=scale_tiles,
                             scale=scale, shift=shift)
    tab_spec = pl.BlockSpec((tm, tw), lambda i, j: (i % sblk, 0))
    return pl.pallas_call(
        kern,
        grid=(m // tm, n // tn),
        in_specs=[pl.BlockSpec((tm, d), lambda i, j: (i, 0)),
                  pl.BlockSpec((1, d), lambda i, j: (0, 0)),
                  pl.BlockSpec((d, tn), lambda i, j: (0, j)),
                  tab_spec, tab_spec, tab_spec],
        out_specs=pl.BlockSpec((tm, tn), lambda i, j: (i, j)),
        out_shape=jax.ShapeDtypeStruct((m, n), BF16),
        scratch_shapes=[pltpu.VMEM((tm, d), BF16)],
        compiler_params=_cparams(("parallel", "arbitrary")),
        name="norm_proj",
    )(x, g.reshape(1, d), w, cos, sa, sb)


def _proj_kernel(a_ref, w_ref, cos_ref, sa_ref, sb_ref, o_ref, *, rope, scale, shift):
    y = _dot(a_ref[...], w_ref[...])
    if rope:
        reps = y.shape[-1] // cos_ref.shape[-1]
        y = _rope(y * scale, _tile_lanes(cos_ref[...], reps), _tile_lanes(sa_ref[...], reps),
                  _tile_lanes(sb_ref[...], reps), shift)
    o_ref[...] = y.astype(o_ref.dtype)


def proj(a, w, tabs, *, seq, tm, tn, rope, scale, shift):
    m, k = a.shape
    n = w.shape[1]
    cos, sa, sb = tabs
    tw = cos.shape[1]
    sblk = seq // tm
    kern = functools.partial(_proj_kernel, rope=rope, scale=scale, shift=shift)
    tab_spec = pl.BlockSpec((tm, tw), lambda i, j: (i % sblk, 0))
    return pl.pallas_call(
        kern,
        grid=(m // tm, n // tn),
        in_specs=[pl.BlockSpec((tm, k), lambda i, j: (i, 0)),
                  pl.BlockSpec((k, tn), lambda i, j: (0, j)),
                  tab_spec, tab_spec, tab_spec],
        out_specs=pl.BlockSpec((tm, tn), lambda i, j: (i, j)),
        out_shape=jax.ShapeDtypeStruct((m, n), BF16),
        compiler_params=_cparams(("parallel", "arbitrary")),
        name="proj",
    )(a, w, cos, sa, sb)


def _out_proj_kernel(a_ref, w_ref, g_ref, x_ref, o_ref):
    m = _dot(a_ref[...], w_ref[...])
    o_ref[...] = x_ref[...] + _rms(m, g_ref[...])


def out_proj_residual(a, w, g, x, *, tm):
    m, k = a.shape
    d = w.shape[1]
    return pl.pallas_call(
        _out_proj_kernel,
        grid=(m // tm,),
        in_specs=[pl.BlockSpec((tm, k), lambda i: (i, 0)),
                  pl.BlockSpec((k, d), lambda i: (0, 0)),
                  pl.BlockSpec((1, d), lambda i: (0, 0)),
                  pl.BlockSpec((tm, d), lambda i: (i, 0))],
        out_specs=pl.BlockSpec((tm, d), lambda i: (i, 0)),
        out_shape=jax.ShapeDtypeStruct((m, d), F32),
        compiler_params=_cparams(("parallel",)),
        name="out_proj_residual",
    )(a, w, g.reshape(1, d), x)


def _ffn_kernel(x_ref, gpre_ref, wg_ref, wu_ref, wo_ref, gpost_ref, o_ref, h_ref):
    j = pl.program_id(1)

    @pl.when(j == 0)
    def _():
        h_ref[...] = _rms(x_ref[...], gpre_ref[...]).astype(BF16)
        o_ref[...] = jnp.zeros_like(o_ref)

    h = h_ref[...]
    gate = _dot(h, wg_ref[...])
    up = _dot(h, wu_ref[...])
    act = (gate / (1.0 + jnp.exp(-gate)) * up).astype(BF16)
    o_ref[...] += _dot(act, wo_ref[...])

    @pl.when(j == pl.num_programs(1) - 1)
    def _():
        o_ref[...] = x_ref[...] + _rms(o_ref[...], gpost_ref[...])


def ffn(x, gpre, w_in, w_out, gpost, *, tm, tf):
    m, d = x.shape
    dff = w_out.shape[0]
    nf = dff // tf
    return pl.pallas_call(
        _ffn_kernel,
        grid=(m // tm, nf),
        in_specs=[pl.BlockSpec((tm, d), lambda i, j: (i, 0)),
                  pl.BlockSpec((1, d), lambda i, j: (0, 0)),
                  pl.BlockSpec((d, tf), lambda i, j: (0, j)),
                  pl.BlockSpec((d, tf), lambda i, j: (0, nf + j)),
                  pl.BlockSpec((tf, d), lambda i, j: (j, 0)),
                  pl.BlockSpec((1, d), lambda i, j: (0, 0))],
        out_specs=pl.BlockSpec((tm, d), lambda i, j: (i, 0)),
        out_shape=jax.ShapeDtypeStruct((m, d), F32),
        scratch_shapes=[pltpu.VMEM((tm, d), BF16)],
        compiler_params=_cparams(("parallel", "arbitrary")),
        name="ffn",
    )(x, gpre.reshape(1, d), w_in, w_in, w_out, gpost.reshape(1, d))


def _win_attn_kernel(sink_ref, q_ref, k_ref, v_ref, o_ref, s_ref, p_ref, *, seq):
    i = pl.program_id(1)
    group = A_HEADS // A_KV_HEADS
    kwin = 3 * A_BLOCK
    start = pl.multiple_of(jnp.clip((i - 1) * A_BLOCK, 0, seq - kwin), A_BLOCK)
    rows = group * A_BLOCK
    qpos = i * A_BLOCK + (lax.broadcasted_iota(jnp.int32, (rows, kwin), 0) & (A_BLOCK - 1))
    kpos = start + lax.broadcasted_iota(jnp.int32, (rows, kwin), 1)
    valid = jnp.abs(kpos - qpos) <= WINDOW
    for kv in range(A_KV_HEADS):
        q4 = jnp.concatenate(
            [q_ref[0, :, (kv * group + gi) * A_HEAD_DIM:(kv * group + gi + 1) * A_HEAD_DIM]
             for gi in range(group)], axis=0)
        kw = k_ref[0, pl.ds(start, kwin), kv * A_HEAD_DIM:(kv + 1) * A_HEAD_DIM]
        s_ref[kv] = jnp.where(valid, _dot_nt(q4, kw), NEG_BIG)
    denoms = []
    for kv in range(A_KV_HEADS):
        s = s_ref[kv]
        sink = jnp.concatenate(
            [jnp.full((A_BLOCK, 1), sink_ref[kv * group + gi], F32) for gi in range(group)], axis=0)
        mx = jnp.maximum(jnp.max(s, axis=-1, keepdims=True), sink)
        p = jnp.exp(s - mx)
        denoms.append(jnp.sum(p, axis=-1, keepdims=True) + jnp.exp(sink - mx))
        p_ref[kv] = p.astype(BF16)
    for kv in range(A_KV_HEADS):
        vw = v_ref[0, pl.ds(start, kwin), kv * A_HEAD_DIM:(kv + 1) * A_HEAD_DIM]
        o = _dot(p_ref[kv], vw) / denoms[kv]
        for gi in range(group):
            h = kv * group + gi
            o_ref[0, :, h * A_HEAD_DIM:(h + 1) * A_HEAD_DIM] = (
                o[gi * A_BLOCK:(gi + 1) * A_BLOCK, :].astype(o_ref.dtype))


def window_attention(qkv, sinks):
    b, s, _ = qkv.shape
    dq = A_HEADS * A_HEAD_DIM
    dkv = A_KV_HEADS * A_HEAD_DIM
    return pl.pallas_call(
        functools.partial(_win_attn_kernel, seq=s),
        grid=(b, s // A_BLOCK),
        in_specs=[pl.BlockSpec(memory_space=pltpu.SMEM),
                  pl.BlockSpec((1, A_BLOCK, dq), lambda bi, i: (bi, i, 0)),
                  pl.BlockSpec((1, s, dkv), lambda bi, i: (bi, 0, dq // dkv)),
                  pl.BlockSpec((1, s, dkv), lambda bi, i: (bi, 0, dq // dkv + 1))],
        out_specs=pl.BlockSpec((1, A_BLOCK, dq), lambda bi, i: (bi, i, 0)),
        out_shape=jax.ShapeDtypeStruct((b, s, dq), BF16),
        scratch_shapes=[pltpu.VMEM((A_KV_HEADS, dq // A_KV_HEADS, 3 * A_BLOCK), F32),
                        pltpu.VMEM((A_KV_HEADS, dq // A_KV_HEADS, 3 * A_BLOCK), BF16)],
        compiler_params=_cparams(("parallel", "arbitrary")),
        name="window_attention",
    )(sinks, qkv, qkv, qkv)


def _mla_down_kernel(x_ref, g_ref, wq_ref, wkv_ref, wkr_ref, qn_ref, kvn_ref,
                     cos_ref, sa_ref, sb_ref, cq_ref, ckv_ref):
    h = _rms(x_ref[...], g_ref[...]).astype(BF16)
    cq_ref[...] = _rms(_dot(h, wq_ref[...]), qn_ref[...]).astype(BF16)
    ckv_ref[:, :B_KV_RANK] = _rms(_dot(h, wkv_ref[...]), kvn_ref[...]).astype(BF16)
    kr = _dot(h, wkr_ref[...])
    ckv_ref[:, B_KV_RANK:] = _rope(kr, cos_ref[...], sa_ref[...], sb_ref[...],
                                   B_ROPE_DIM // 2).astype(BF16)


def mla_down(x, g, wq, wkv, wkr, qn, kvn, tabs, *, seq, tm):
    m, d = x.shape
    cos, sa, sb = tabs
    sblk = seq // tm
    tab_spec = pl.BlockSpec((tm, LANES), lambda i: (i % sblk, 0))
    full = lambda shape: pl.BlockSpec(shape, lambda i: (0, 0))
    return pl.pallas_call(
        _mla_down_kernel,
        grid=(m // tm,),
        in_specs=[pl.BlockSpec((tm, d), lambda i: (i, 0)), full((1, d)),
                  full((d, B_Q_RANK)), full((d, B_KV_RANK)), full((d, LANES)),
                  full((1, B_Q_RANK)), full((1, B_KV_RANK)),
                  tab_spec, tab_spec, tab_spec],
        out_specs=[pl.BlockSpec((tm, B_Q_RANK), lambda i: (i, 0)),
                   pl.BlockSpec((tm, B_KV_RANK + LANES), lambda i: (i, 0))],
        out_shape=[jax.ShapeDtypeStruct((m, B_Q_RANK), BF16),
                   jax.ShapeDtypeStruct((m, B_KV_RANK + LANES), BF16)],
        compiler_params=_cparams(("parallel",)),
        name="mla_down",
    )(x, g.reshape(1, d), wq, wkv, wkr, qn.reshape(1, -1), kvn.reshape(1, -1), cos, sa, sb)


def _mla_attn_kernel(q_ref, k_ref, v_ref, o_ref, *, chunk):
    for c in range(q_ref.shape[1] // chunk):
        rows = slice(c * chunk, (c + 1) * chunk)
        s = _dot_nt(q_ref[0, rows, :], k_ref[0])
        mx = jnp.max(s, axis=-1, keepdims=True)
        p = jnp.exp(s - mx)
        denom = jnp.sum(p, axis=-1, keepdims=True)
        o_ref[0, rows, :] = (_dot(p.astype(BF16), v_ref[0]) / denom).astype(o_ref.dtype)


def mla_attention(qcat, kvcat, *, tq, chunk):
    b, s, _ = qcat.shape
    vblk0 = B_HEADS * B_QK_PAD // B_V_DIM
    return pl.pallas_call(
        functools.partial(_mla_attn_kernel, chunk=chunk),
        grid=(b, B_HEADS, s // tq),
        in_specs=[pl.BlockSpec((1, tq, B_QK_PAD), lambda bi, h, qi: (bi, qi, h)),
                  pl.BlockSpec((1, s, B_QK_PAD), lambda bi, h, qi: (bi, 0, h)),
                  pl.BlockSpec((1, s, B_V_DIM), lambda bi, h, qi: (bi, 0, vblk0 + h))],
        out_specs=pl.BlockSpec((1, tq, B_V_DIM), lambda bi, h, qi: (bi, qi, h)),
        out_shape=jax.ShapeDtypeStruct((b, s, B_HEADS * B_V_DIM), BF16),
        compiler_params=_cparams(("parallel", "parallel", "arbitrary")),
        name="mla_attention",
    )(qcat, kvcat, kvcat)


def _nb_attn_kernel(q_ref, k_ref, v_ref, t_ref, o_ref, *, rows):
    rb = pl.program_id(2)
    nk = C_KROWS * GRID_W
    start = pl.multiple_of(
        jnp.clip(rb * C_QROWS - NB_H // 2, 0, rows - C_KROWS) * GRID_W, (NB_H // 2) * GRID_W)
    kw = k_ref[0, pl.ds(start, nk), :]
    vw = v_ref[0, pl.ds(start, nk), :]
    s = _dot_nt(q_ref[0], kw) + t_ref[0, 0]
    mx = jnp.max(s, axis=-1, keepdims=True)
    p = jnp.exp(s - mx)
    denom = jnp.sum(p, axis=-1, keepdims=True)
    o_ref[0] = (_dot(p.astype(BF16), vw) / denom).astype(o_ref.dtype)


def neighborhood_attention(qkv, table):
    b, s, _ = qkv.shape
    rows = s // GRID_W
    nrb = rows // C_QROWS
    tq = C_QROWS * GRID_W
    nk = C_KROWS * GRID_W

    def t_index(bi, h, rb):
        return (h, jnp.where(rb == 0, 0, jnp.where(rb == nrb - 1, 2, 1)), 0, 0)

    return pl.pallas_call(
        functools.partial(_nb_attn_kernel, rows=rows),
        grid=(b, C_HEADS, nrb),
        in_specs=[pl.BlockSpec((1, tq, C_HEAD_DIM), lambda bi, h, rb: (bi, rb, h)),
                  pl.BlockSpec((1, s, C_HEAD_DIM), lambda bi, h, rb: (bi, 0, C_HEADS + h)),
                  pl.BlockSpec((1, s, C_HEAD_DIM), lambda bi, h, rb: (bi, 0, 2 * C_HEADS + h)),
                  pl.BlockSpec((1, 1, tq, nk), t_index)],
        out_specs=pl.BlockSpec((1, tq, C_HEAD_DIM), lambda bi, h, rb: (bi, rb, h)),
        out_shape=jax.ShapeDtypeStruct((b, s, C_HEADS * C_HEAD_DIM), BF16),
        compiler_params=_cparams(("parallel", "parallel", "arbitrary")),
        name="neighborhood_attention",
    )(qkv, qkv, qkv, table)


def _nb_bias_table(rel_bias, rows):
    nh = rel_bias.shape[0]
    c = np.arange(GRID_W)[:, None]
    kc = np.arange(GRID_W)[None, :]
    cs = np.clip(c - NB_W // 2, 0, GRID_W - NB_W)
    col_ok = (kc >= cs) & (kc < cs + NB_W)
    dc = np.clip(kc - c + NB_W - 1, 0, 2 * NB_W - 2)
    bias_c = jnp.where(col_ok[None, None], rel_bias.astype(F32)[:, :, dc], NEG_BIG)
    r0s = np.array([0, C_QROWS, rows - C_QROWS])
    w0s = np.clip(r0s - NB_H // 2, 0, rows - C_KROWS)
    r = r0s[:, None, None] + np.arange(C_QROWS)[None, :, None]
    kr = w0s[:, None, None] + np.arange(C_KROWS)[None, None, :]
    rs = np.clip(r - NB_H // 2, 0, rows - NB_H)
    row_ok = (kr >= rs) & (kr < rs + NB_H)
    dr = np.clip(kr - r + NB_H - 1, 0, 2 * NB_H - 2)
    t = bias_c[:, dr]
    t = jnp.where(row_ok[None, :, :, :, None, None], t, NEG_BIG)
    t = t.transpose(0, 1, 2, 4, 3, 5)
    return t.reshape(nh, 3, C_QROWS * GRID_W, C_KROWS * GRID_W)


def _rope_tables(seq, dim, lane0, width):
    half = dim // 2
    pos = jnp.arange(seq, dtype=F32)
    inv = ROPE_THETA ** (-jnp.arange(0, dim, 2, dtype=F32) / dim)
    ang = pos[:, None] * inv[None, :]
    c, sn = jnp.cos(ang), jnp.sin(ang)
    cos = jnp.ones((seq, width), F32).at[:, lane0:lane0 + dim].set(jnp.concatenate([c, c], 1))
    sa = jnp.zeros((seq, width), F32).at[:, lane0 + half:lane0 + dim].set(sn)
    sb = jnp.zeros((seq, width), F32).at[:, lane0:lane0 + half].set(-sn)
    return cos, sa, sb


def kernel(x, pre_mix_norm, post_mix_norm, pre_ffn_norm, post_ffn_norm, a_w_qkv, a_sinks, a_w_o,
           b_w_down, b_q_norm, b_w_uq, b_kv_norm, b_w_ukv, b_w_o, c_w_qkv, c_rel_bias, c_w_o,
           ffn_w_in, ffn_w_out):
    bsz, seq, d = x.shape
    m = bsz * seq
    rows = seq // GRID_W
    tabs_a = _rope_tables(seq, A_ROT_DIM, 0, LANES)
    tabs_bk = _rope_tables(seq, B_ROPE_DIM, 0, LANES)
    tabs_bq = _rope_tables(seq, B_ROPE_DIM, B_NOPE_DIM, B_QK_PAD)

    xf = x.reshape(m, d)
    for i in range(DEPTH):
        kind, slot = i % N_MIXERS, i // N_MIXERS
        if kind == 0:
            qkv = norm_proj(xf, pre_mix_norm[i], a_w_qkv[slot].astype(BF16), tabs_a, seq=seq,
                            tm=1024, tn=512, rope_tiles=5, scale_tiles=4,
                            scale=A_HEAD_DIM ** -0.5, shift=A_ROT_DIM // 2)
            o = window_attention(qkv.reshape(bsz, seq, -1), a_sinks[slot].astype(F32))
            w_o = a_w_o[slot]
        elif kind == 1:
            wd = b_w_down[slot].astype(BF16)
            wkr = jnp.pad(wd[:, B_Q_RANK + B_KV_RANK:], ((0, 0), (0, LANES - B_ROPE_DIM)))
            cq, ckv = mla_down(xf, pre_mix_norm[i], wd[:, :B_Q_RANK],
                               wd[:, B_Q_RANK:B_Q_RANK + B_KV_RANK], wkr,
                               b_q_norm[slot], b_kv_norm[slot], tabs_bk, seq=seq, tm=512)
            wuq = b_w_uq[slot].astype(BF16).reshape(B_Q_RANK, B_HEADS, B_NOPE_DIM + B_ROPE_DIM)
            wuq = jnp.pad(wuq, ((0, 0), (0, 0), (0, B_QK_PAD - B_NOPE_DIM - B_ROPE_DIM)))
            qcat = proj(cq, wuq.reshape(B_Q_RANK, B_HEADS * B_QK_PAD), tabs_bq, seq=seq,
                        tm=1024, tn=1024, rope=True,
                        scale=(B_NOPE_DIM + B_ROPE_DIM) ** -0.5, shift=B_ROPE_DIM // 2)
            wukv = b_w_ukv[slot].astype(BF16).reshape(B_KV_RANK, B_HEADS, B_NOPE_DIM + B_V_DIM)
            eye = jnp.eye(LANES, dtype=BF16)[:, :B_QK_PAD - B_NOPE_DIM]
            eye = eye.at[B_ROPE_DIM:].set(0)
            wk = jnp.concatenate(
                [jnp.pad(wukv[:, :, :B_NOPE_DIM], ((0, LANES), (0, 0), (0, 0))),
                 jnp.pad(jnp.broadcast_to(eye[:, None, :], (LANES, B_HEADS, eye.shape[1])),
                         ((B_KV_RANK, 0), (0, 0), (0, 0)))], axis=2)
            wv = jnp.pad(wukv[:, :, B_NOPE_DIM:], ((0, LANES), (0, 0), (0, 0)))
            wkv = jnp.concatenate([wk.reshape(B_KV_RANK + LANES, -1),
                                   wv.reshape(B_KV_RANK + LANES, -1)], axis=1)
            kvcat = proj(ckv, wkv, tabs_bq, seq=seq, tm=1024, tn=1024, rope=False, scale=1.0,
                         shift=0)
            o = mla_attention(qcat.reshape(bsz, seq, -1), kvcat.reshape(bsz, seq, -1), tq=1024,
                              chunk=256)
            w_o = b_w_o[slot]
        else:
            qkv = norm_proj(xf, pre_mix_norm[i], c_w_qkv[slot].astype(BF16), tabs_a, seq=seq,
                            tm=1024, tn=512, rope_tiles=0, scale_tiles=4,
                            scale=C_HEAD_DIM ** -0.5, shift=0)
            table = _nb_bias_table(c_rel_bias[slot], rows)
            o = neighborhood_attention(qkv.reshape(bsz, seq, -1), table)
            w_o = c_w_o[slot]
        xf = out_proj_residual(o.reshape(m, -1), w_o.astype(BF16), post_mix_norm[i], xf, tm=512)
        xf = ffn(xf, pre_ffn_norm[i], ffn_w_in[i].astype(BF16), ffn_w_out[i].astype(BF16),
                 post_ffn_norm[i], tm=512, tf=512)
    return xf.reshape(bsz, seq, d)
```

```python
import functools

import jax
import jax.numpy as jnp
import numpy as np
from jax import lax
from jax.experimental import pallas as pl
from jax.experimental.pallas import tpu as pltpu

D_MODEL = 2048
DEPTH = 4
N_MIXERS = 3
GRID_W = 64
ROPE_THETA = 500000.0
NORM_EPS = 1e-6
D_FF = 5632

A_HEADS = 16
A_KV_HEADS = 4
A_GROUP = A_HEADS // A_KV_HEADS
A_HEAD_DIM = 128
A_ROT_DIM = 32
WINDOW = 128
A_BLOCK = 128

B_HEADS = 16
B_NOPE_DIM = 128
B_ROPE_DIM = 64
B_V_DIM = 128
B_Q_RANK = 512
B_KV_RANK = 512
B_QK_PAD = 256

C_HEADS = 16
C_HEAD_DIM = 128
NB_H = 8
NB_W = 16
C_QROWS = 8
C_KROWS = 16

LANES = 128
ROPE_SWAP = LANES // 2
V7X_VMEM_LIMIT_BYTES = 56 * 1024 * 1024

NEG_BIG = -1e30
BF16 = jnp.bfloat16
F32 = jnp.float32
PLAIN = (False, 1.0)


def _cparams(semantics):
    return pltpu.CompilerParams(dimension_semantics=semantics,
                                vmem_limit_bytes=V7X_VMEM_LIMIT_BYTES)


def _rms(xf, g):
    ms = jnp.mean(xf * xf, axis=-1, keepdims=True)
    return xf * lax.rsqrt(ms + NORM_EPS) * g


def _dot(a, b):
    return jnp.dot(a, b, preferred_element_type=F32)


def _dot_nt(a, b):
    return lax.dot_general(a, b, (((1,), (1,)), ((), ())), preferred_element_type=F32)


def _rope(y, cos, sin):
    return y * cos + pltpu.roll(y, ROPE_SWAP, 1) * sin


def _store_columns(y, o_ref, cos_ref, sin_ref, ops):
    if all(op == PLAIN for op in ops):
        o_ref[...] = y.astype(o_ref.dtype)
        return
    for c, (rope, scale) in enumerate(ops):
        cols = slice(c * LANES, (c + 1) * LANES)
        yc = y[:, cols]
        if scale != 1.0:
            yc = yc * scale
        if rope:
            yc = _rope(yc, cos_ref[...], sin_ref[...])
        o_ref[:, cols] = yc.astype(o_ref.dtype)


def _epilogue(y, o_ref, cos_ref, sin_ref, kinds):
    if len(kinds) == 1:
        _store_columns(y, o_ref, cos_ref, sin_ref, kinds[0][2])
        return
    j = pl.program_id(1)
    for lo, hi, ops in kinds:
        @pl.when((j >= lo) & (j < hi))
        def _(ops=ops):
            _store_columns(y, o_ref, cos_ref, sin_ref, ops)


def _norm_proj_kernel(x_ref, g_ref, w_ref, cos_ref, sin_ref, o_ref, h_ref, *, kinds):
    @pl.when(pl.program_id(1) == 0)
    def _():
        h_ref[...] = _rms(x_ref[...], g_ref[...]).astype(BF16)

    _epilogue(_dot(h_ref[...], w_ref[...]), o_ref, cos_ref, sin_ref, kinds)


def norm_proj(x, g, w, tabs, *, seq, tm, tn, kinds):
    m, d = x.shape
    n = w.shape[1]
    cos, sin = tabs
    sblk = seq // tm
    tab_spec = pl.BlockSpec((tm, LANES), lambda i, j: (i % sblk, 0))
    return pl.pallas_call(
        functools.partial(_norm_proj_kernel, kinds=kinds),
        grid=(m // tm, n // tn),
        in_specs=[pl.BlockSpec((tm, d), lambda i, j: (i, 0)),
                  pl.BlockSpec((1, d), lambda i, j: (0, 0)),
                  pl.BlockSpec((d, tn), lambda i, j: (0, j)),
                  tab_spec, tab_spec],
        out_specs=pl.BlockSpec((tm, tn), lambda i, j: (i, j)),
        out_shape=jax.ShapeDtypeStruct((m, n), BF16),
        scratch_shapes=[pltpu.VMEM((tm, d), BF16)],
        compiler_params=_cparams(("parallel", "arbitrary")),
        name="norm_proj",
    )(x, g.reshape(1, d), w, cos, sin)


def _proj_kernel(a_ref, w_ref, cos_ref, sin_ref, o_ref, *, kinds):
    _epilogue(_dot(a_ref[...], w_ref[...]), o_ref, cos_ref, sin_ref, kinds)


def proj(a, w, tabs, *, seq, tm, tn, kinds):
    m, k = a.shape
    n = w.shape[1]
    cos, sin = tabs
    sblk = seq // tm
    tab_spec = pl.BlockSpec((tm, LANES), lambda i, j: (i % sblk, 0))
    return pl.pallas_call(
        functools.partial(_proj_kernel, kinds=kinds),
        grid=(m // tm, n // tn),
        in_specs=[pl.BlockSpec((tm, k), lambda i, j: (i, 0)),
                  pl.BlockSpec((k, tn), lambda i, j: (0, j)),
                  tab_spec, tab_spec],
        out_specs=pl.BlockSpec((tm, tn), lambda i, j: (i, j)),
        out_shape=jax.ShapeDtypeStruct((m, n), BF16),
        compiler_params=_cparams(("parallel", "arbitrary")),
        name="proj",
    )(a, w, cos, sin)


def _out_proj_kernel(a_ref, w_ref, g_ref, x_ref, o_ref):
    m = _dot(a_ref[...], w_ref[...])
    o_ref[...] = x_ref[...] + _rms(m, g_ref[...])


def out_proj_residual(a, w, g, x, *, tm):
    m, k = a.shape
    d = w.shape[1]
    return pl.pallas_call(
        _out_proj_kernel,
        grid=(m // tm,),
        in_specs=[pl.BlockSpec((tm, k), lambda i: (i, 0)),
                  pl.BlockSpec((k, d), lambda i: (0, 0), pipeline_mode=pl.Buffered(1)),
                  pl.BlockSpec((1, d), lambda i: (0, 0)),
                  pl.BlockSpec((tm, d), lambda i: (i, 0))],
        out_specs=pl.BlockSpec((tm, d), lambda i: (i, 0)),
        out_shape=jax.ShapeDtypeStruct((m, d), F32),
        compiler_params=_cparams(("parallel",)),
        name="out_proj_residual",
    )(a, w, g.reshape(1, d), x)


def _ffn_kernel(x_ref, gpre_ref, wgu_ref, wo_ref, gpost_ref, o_ref, h_ref):
    j = pl.program_id(1)

    @pl.when(j == 0)
    def _():
        h_ref[...] = _rms(x_ref[...], gpre_ref[...]).astype(BF16)
        o_ref[...] = jnp.zeros_like(o_ref)

    gu = _dot(h_ref[...], wgu_ref[...])
    tf = gu.shape[1] // 2
    gate, up = gu[:, :tf], gu[:, tf:]
    act = (gate / (1.0 + jnp.exp(-gate)) * up).astype(BF16)
    o_ref[...] += _dot(act, wo_ref[...])

    @pl.when(j == pl.num_programs(1) - 1)
    def _():
        o_ref[...] = x_ref[...] + _rms(o_ref[...], gpost_ref[...])


def ffn(x, gpre, w_gu, w_out, gpost, *, tm, tf):
    m, d = x.shape
    dff = w_out.shape[0]
    return pl.pallas_call(
        _ffn_kernel,
        grid=(m // tm, dff // tf),
        in_specs=[pl.BlockSpec((tm, d), lambda i, j: (i, 0)),
                  pl.BlockSpec((1, d), lambda i, j: (0, 0)),
                  pl.BlockSpec((d, 2 * tf), lambda i, j: (0, j)),
                  pl.BlockSpec((tf, d), lambda i, j: (j, 0)),
                  pl.BlockSpec((1, d), lambda i, j: (0, 0))],
        out_specs=pl.BlockSpec((tm, d), lambda i, j: (i, 0)),
        out_shape=jax.ShapeDtypeStruct((m, d), F32),
        scratch_shapes=[pltpu.VMEM((tm, d), BF16)],
        compiler_params=_cparams(("parallel", "arbitrary")),
        name="ffn",
    )(x, gpre.reshape(1, d), w_gu, w_out, gpost.reshape(1, d))


def _win_attn_kernel(sink_ref, q_ref, k_ref, v_ref, o_ref, s_ref, p_ref, *, seq):
    i = pl.program_id(1)
    kwin = 3 * A_BLOCK
    start = pl.multiple_of(jnp.clip((i - 1) * A_BLOCK, 0, seq - kwin), A_BLOCK)
    rows = A_GROUP * A_BLOCK
    qpos = i * A_BLOCK + (lax.broadcasted_iota(jnp.int32, (rows, kwin), 0) & (A_BLOCK - 1))
    kpos = start + lax.broadcasted_iota(jnp.int32, (rows, kwin), 1)
    valid = jnp.abs(kpos - qpos) <= WINDOW
    for kv in range(A_KV_HEADS):
        q4 = jnp.concatenate(
            [q_ref[0, :, (kv * A_GROUP + gi) * A_HEAD_DIM:(kv * A_GROUP + gi + 1) * A_HEAD_DIM]
             for gi in range(A_GROUP)], axis=0)
        kw = k_ref[0, pl.ds(start, kwin), kv * A_HEAD_DIM:(kv + 1) * A_HEAD_DIM]
        s_ref[kv] = jnp.where(valid, _dot_nt(q4, kw), NEG_BIG)
    denoms = []
    for kv in range(A_KV_HEADS):
        s = s_ref[kv]
        sink = jnp.concatenate(
            [jnp.full((A_BLOCK, 1), sink_ref[kv * A_GROUP + gi], F32) for gi in range(A_GROUP)],
            axis=0)
        mx = jnp.maximum(jnp.max(s, axis=-1, keepdims=True), sink)
        p = jnp.exp(s - mx)
        denoms.append(jnp.sum(p, axis=-1, keepdims=True) + jnp.exp(sink - mx))
        p_ref[kv] = p.astype(BF16)
    for kv in range(A_KV_HEADS):
        vw = v_ref[0, pl.ds(start, kwin), kv * A_HEAD_DIM:(kv + 1) * A_HEAD_DIM]
        o = _dot(p_ref[kv], vw) / denoms[kv]
        for gi in range(A_GROUP):
            h = kv * A_GROUP + gi
            o_ref[0, :, h * A_HEAD_DIM:(h + 1) * A_HEAD_DIM] = (
                o[gi * A_BLOCK:(gi + 1) * A_BLOCK, :].astype(o_ref.dtype))


def window_attention(qkv, sinks):
    b, s, _ = qkv.shape
    dq = A_HEADS * A_HEAD_DIM
    dkv = A_KV_HEADS * A_HEAD_DIM
    score_shape = (A_KV_HEADS, A_GROUP * A_BLOCK, 3 * A_BLOCK)
    return pl.pallas_call(
        functools.partial(_win_attn_kernel, seq=s),
        grid=(b, s // A_BLOCK),
        in_specs=[pl.BlockSpec(memory_space=pltpu.SMEM),
                  pl.BlockSpec((1, A_BLOCK, dq), lambda bi, i: (bi, i, 0)),
                  pl.BlockSpec((1, s, dkv), lambda bi, i: (bi, 0, dq // dkv)),
                  pl.BlockSpec((1, s, dkv), lambda bi, i: (bi, 0, dq // dkv + 1))],
        out_specs=pl.BlockSpec((1, A_BLOCK, dq), lambda bi, i: (bi, i, 0)),
        out_shape=jax.ShapeDtypeStruct((b, s, dq), BF16),
        scratch_shapes=[pltpu.VMEM(score_shape, F32), pltpu.VMEM(score_shape, BF16)],
        compiler_params=_cparams(("parallel", "arbitrary")),
        name="window_attention",
    )(sinks, qkv, qkv, qkv)


def _mla_down_kernel(x_ref, g_ref, wq_ref, wkv_ref, wkr_ref, qn_ref, kvn_ref,
                     cos_ref, sin_ref, cq_ref, ckv_ref):
    h = _rms(x_ref[...], g_ref[...]).astype(BF16)
    cq_ref[...] = _rms(_dot(h, wq_ref[...]), qn_ref[...]).astype(BF16)
    ckv_ref[:, :B_KV_RANK] = _rms(_dot(h, wkv_ref[...]), kvn_ref[...]).astype(BF16)
    ckv_ref[:, B_KV_RANK:] = _rope(_dot(h, wkr_ref[...]), cos_ref[...], sin_ref[...]).astype(BF16)


def mla_down(x, g, wq, wkv, wkr, qn, kvn, tabs, *, seq, tm):
    m, d = x.shape
    cos, sin = tabs
    sblk = seq // tm
    tab_spec = pl.BlockSpec((tm, LANES), lambda i: (i % sblk, 0))
    full = lambda shape: pl.BlockSpec(shape, lambda i: (0, 0))
    return pl.pallas_call(
        _mla_down_kernel,
        grid=(m // tm,),
        in_specs=[pl.BlockSpec((tm, d), lambda i: (i, 0)), full((1, d)),
                  full((d, B_Q_RANK)), full((d, B_KV_RANK)), full((d, LANES)),
                  full((1, B_Q_RANK)), full((1, B_KV_RANK)),
                  tab_spec, tab_spec],
        out_specs=[pl.BlockSpec((tm, B_Q_RANK), lambda i: (i, 0)),
                   pl.BlockSpec((tm, B_KV_RANK + LANES), lambda i: (i, 0))],
        out_shape=[jax.ShapeDtypeStruct((m, B_Q_RANK), BF16),
                   jax.ShapeDtypeStruct((m, B_KV_RANK + LANES), BF16)],
        compiler_params=_cparams(("parallel",)),
        name="mla_down",
    )(x, g.reshape(1, d), wq, wkv, wkr, qn.reshape(1, -1), kvn.reshape(1, -1), cos, sin)


def _mla_attn_kernel(q_ref, k_ref, v_ref, o_ref, *, chunk):
    for c in range(q_ref.shape[1] // chunk):
        rows = slice(c * chunk, (c + 1) * chunk)
        s = _dot_nt(q_ref[0, rows, :], k_ref[0])
        mx = jnp.max(s, axis=-1, keepdims=True)
        p = jnp.exp(s - mx)
        denom = jnp.sum(p, axis=-1, keepdims=True)
        o_ref[0, rows, :] = (_dot(p.astype(BF16), v_ref[0]) / denom).astype(o_ref.dtype)


def mla_attention(qcat, kvcat, *, tq, chunk):
    b, s, _ = qcat.shape
    vblk0 = B_HEADS * B_QK_PAD // B_V_DIM
    return pl.pallas_call(
        functools.partial(_mla_attn_kernel, chunk=chunk),
        grid=(b, B_HEADS, s // tq),
        in_specs=[pl.BlockSpec((1, tq, B_QK_PAD), lambda bi, h, qi: (bi, qi, h)),
                  pl.BlockSpec((1, s, B_QK_PAD), lambda bi, h, qi: (bi, 0, h)),
                  pl.BlockSpec((1, s, B_V_DIM), lambda bi, h, qi: (bi, 0, vblk0 + h))],
        out_specs=pl.BlockSpec((1, tq, B_V_DIM), lambda bi, h, qi: (bi, qi, h)),
        out_shape=jax.ShapeDtypeStruct((b, s, B_HEADS * B_V_DIM), BF16),
        compiler_params=_cparams(("parallel", "parallel", "arbitrary")),
        name="mla_attention",
    )(qcat, kvcat, kvcat)


def _nb_attn_kernel(q_ref, k_ref, v_ref, t_ref, o_ref, *, rows):
    rb = pl.program_id(2)
    nk = C_KROWS * GRID_W
    start = pl.multiple_of(
        jnp.clip(rb * C_QROWS - NB_H // 2, 0, rows - C_KROWS) * GRID_W, (NB_H // 2) * GRID_W)
    kw = k_ref[0, pl.ds(start, nk), :]
    vw = v_ref[0, pl.ds(start, nk), :]
    s = _dot_nt(q_ref[0], kw) + t_ref[0, 0]
    mx = jnp.max(s, axis=-1, keepdims=True)
    p = jnp.exp(s - mx)
    denom = jnp.sum(p, axis=-1, keepdims=True)
    o_ref[0] = (_dot(p.astype(BF16), vw) / denom).astype(o_ref.dtype)


def neighborhood_attention(qkv, table):
    b, s, _ = qkv.shape
    rows = s // GRID_W
    nrb = rows // C_QROWS
    tq = C_QROWS * GRID_W
    nk = C_KROWS * GRID_W

    def t_index(bi, h, rb):
        return (h, jnp.where(rb == 0, 0, jnp.where(rb == nrb - 1, 2, 1)), 0, 0)

    return pl.pallas_call(
        functools.partial(_nb_attn_kernel, rows=rows),
        grid=(b, C_HEADS, nrb),
        in_specs=[pl.BlockSpec((1, tq, C_HEAD_DIM), lambda bi, h, rb: (bi, rb, h)),
                  pl.BlockSpec((1, s, C_HEAD_DIM), lambda bi, h, rb: (bi, 0, C_HEADS + h)),
                  pl.BlockSpec((1, s, C_HEAD_DIM), lambda bi, h, rb: (bi, 0, 2 * C_HEADS + h)),
                  pl.BlockSpec((1, 1, tq, nk), t_index)],
        out_specs=pl.BlockSpec((1, tq, C_HEAD_DIM), lambda bi, h, rb: (bi, rb, h)),
        out_shape=jax.ShapeDtypeStruct((b, s, C_HEADS * C_HEAD_DIM), BF16),
        compiler_params=_cparams(("parallel", "parallel", "arbitrary")),
        name="neighborhood_attention",
    )(qkv, qkv, qkv, table)


def _nb_bias_table(rel_bias, rows):
    nh = rel_bias.shape[0]
    c = np.arange(GRID_W)[:, None]
    kc = np.arange(GRID_W)[None, :]
    cs = np.clip(c - NB_W // 2, 0, GRID_W - NB_W)
    col_ok = (kc >= cs) & (kc < cs + NB_W)
    dc = np.clip(kc - c + NB_W - 1, 0, 2 * NB_W - 2)
    bias_c = jnp.where(col_ok[None, None], rel_bias.astype(F32)[:, :, dc], NEG_BIG)
    r0s = np.array([0, C_QROWS, rows - C_QROWS])
    w0s = np.clip(r0s - NB_H // 2, 0, rows - C_KROWS)
    r = r0s[:, None, None] + np.arange(C_QROWS)[None, :, None]
    kr = w0s[:, None, None] + np.arange(C_KROWS)[None, None, :]
    rs = np.clip(r - NB_H // 2, 0, rows - NB_H)
    row_ok = (kr >= rs) & (kr < rs + NB_H)
    dr = np.clip(kr - r + NB_H - 1, 0, 2 * NB_H - 2)
    t = bias_c[:, dr]
    t = jnp.where(row_ok[None, :, :, :, None, None], t, NEG_BIG)
    t = t.transpose(0, 1, 2, 4, 3, 5)
    return t.reshape(nh, 3, C_QROWS * GRID_W, C_KROWS * GRID_W)


def _pair_split(w, dim):
    half = dim // 2
    gap = ROPE_SWAP - half
    return jnp.concatenate([w[..., :half], w[..., dim:dim + gap], w[..., half:dim],
                            w[..., dim + gap:]], axis=-1)


def _rope_tables(seq, dim):
    half = dim // 2
    pos = jnp.arange(seq, dtype=F32)
    inv = ROPE_THETA ** (-jnp.arange(0, dim, 2, dtype=F32) / dim)
    ang = pos[:, None] * inv[None, :]
    c, sn = jnp.cos(ang), jnp.sin(ang)
    cos = jnp.ones((seq, LANES), F32).at[:, :half].set(c).at[:, ROPE_SWAP:ROPE_SWAP + half].set(c)
    sin = jnp.zeros((seq, LANES), F32).at[:, :half].set(-sn).at[:, ROPE_SWAP:ROPE_SWAP + half].set(sn)
    return cos, sin


def kernel(x, pre_mix_norm, post_mix_norm, pre_ffn_norm, post_ffn_norm, a_w_qkv, a_sinks, a_w_o,
           b_w_down, b_q_norm, b_w_uq, b_kv_norm, b_w_ukv, b_w_o, c_w_qkv, c_rel_bias, c_w_o,
           ffn_w_in, ffn_w_out):
    bsz, seq, d = x.shape
    m = bsz * seq
    rows = seq // GRID_W
    tabs_a = _rope_tables(seq, A_ROT_DIM)
    tabs_b = _rope_tables(seq, B_ROPE_DIM)
    tn = 1024
    cols = tn // LANES

    xf = x.reshape(m, d)
    for i in range(DEPTH):
        kind, slot = i % N_MIXERS, i // N_MIXERS
        if kind == 0:
            nqk = A_HEADS + A_KV_HEADS
            w3 = a_w_qkv[slot].reshape(d, nqk + A_KV_HEADS, A_HEAD_DIM)
            w = jnp.concatenate([_pair_split(w3[:, :nqk], A_ROT_DIM), w3[:, nqk:]], axis=1)
            q_ops = ((True, A_HEAD_DIM ** -0.5),) * cols
            kv_ops = ((True, 1.0),) * A_KV_HEADS + (PLAIN,) * A_KV_HEADS
            qkv = norm_proj(xf, pre_mix_norm[i], w.reshape(d, -1).astype(BF16), tabs_a, seq=seq,
                            tm=1024, tn=tn, kinds=((0, 2, q_ops), (2, 3, kv_ops)))
            o = window_attention(qkv.reshape(bsz, seq, -1), a_sinks[slot].astype(F32))
            w_o = a_w_o[slot]
        elif kind == 1:
            wd = b_w_down[slot]
            wkr = jnp.pad(wd[:, B_Q_RANK + B_KV_RANK:], ((0, 0), (0, LANES - B_ROPE_DIM)))
            cq, ckv = mla_down(xf, pre_mix_norm[i], wd[:, :B_Q_RANK].astype(BF16),
                               wd[:, B_Q_RANK:B_Q_RANK + B_KV_RANK].astype(BF16),
                               _pair_split(wkr, B_ROPE_DIM).astype(BF16),
                               b_q_norm[slot], b_kv_norm[slot], tabs_b, seq=seq, tm=512)
            wuq = b_w_uq[slot].reshape(B_Q_RANK, B_HEADS, B_NOPE_DIM + B_ROPE_DIM)
            wuq_r = jnp.pad(wuq[:, :, B_NOPE_DIM:], ((0, 0), (0, 0), (0, LANES - B_ROPE_DIM)))
            wuq = jnp.concatenate([wuq[:, :, :B_NOPE_DIM], _pair_split(wuq_r, B_ROPE_DIM)], axis=2)
            scale = (B_NOPE_DIM + B_ROPE_DIM) ** -0.5
            q_ops = ((False, scale), (True, scale)) * (cols // 2)
            qcat = proj(cq, wuq.reshape(B_Q_RANK, -1).astype(BF16), tabs_b, seq=seq,
                        tm=1024, tn=tn, kinds=((0, 1, q_ops),))
            wukv = b_w_ukv[slot].reshape(B_KV_RANK, B_HEADS, B_NOPE_DIM + B_V_DIM)
            eye = jnp.broadcast_to(jnp.eye(LANES, dtype=F32)[:, None, :], (LANES, B_HEADS, LANES))
            wk = jnp.concatenate(
                [jnp.pad(wukv[:, :, :B_NOPE_DIM], ((0, LANES), (0, 0), (0, 0))),
                 jnp.pad(eye, ((B_KV_RANK, 0), (0, 0), (0, 0)))], axis=2)
            wv = jnp.pad(wukv[:, :, B_NOPE_DIM:], ((0, LANES), (0, 0), (0, 0)))
            wkv = jnp.concatenate([wk.reshape(B_KV_RANK + LANES, -1),
                                   wv.reshape(B_KV_RANK + LANES, -1)], axis=1).astype(BF16)
            kvcat = proj(ckv, wkv, tabs_b, seq=seq, tm=1024, tn=tn,
                         kinds=((0, 1, (PLAIN,) * cols),))
            o = mla_attention(qcat.reshape(bsz, seq, -1), kvcat.reshape(bsz, seq, -1), tq=1024,
                              chunk=256)
            w_o = b_w_o[slot]
        else:
            q_ops = ((False, C_HEAD_DIM ** -0.5),) * cols
            nq = C_HEADS * C_HEAD_DIM // tn
            qkv = norm_proj(xf, pre_mix_norm[i], c_w_qkv[slot].astype(BF16), tabs_a, seq=seq,
                            tm=1024, tn=tn, kinds=((0, nq, q_ops), (nq, 3 * nq, (PLAIN,) * cols)))
            table = _nb_bias_table(c_rel_bias[slot], rows)
            o = neighborhood_attention(qkv.reshape(bsz, seq, -1), table)
            w_o = c_w_o[slot]
        xf = out_proj_residual(o.reshape(m, -1), w_o.astype(BF16), post_mix_norm[i], xf, tm=512)
        tf = 256
        w_gu = ffn_w_in[i].reshape(d, 2, D_FF // tf, tf).transpose(0, 2, 1, 3).reshape(d, 2 * D_FF)
        xf = ffn(xf, pre_ffn_norm[i], w_gu.astype(BF16), ffn_w_out[i].astype(BF16),
                 post_ffn_norm[i], tm=1024, tf=tf)
    return xf.reshape(bsz, seq, d)
```

```python
import functools

import jax
import jax.numpy as jnp
import numpy as np
from jax import lax
from jax.experimental import pallas as pl
from jax.experimental.pallas import tpu as pltpu

D_MODEL = 2048
DEPTH = 4
N_MIXERS = 3
GRID_W = 64
ROPE_THETA = 500000.0
NORM_EPS = 1e-6
D_FF = 5632

A_HEADS = 16
A_KV_HEADS = 4
A_GROUP = A_HEADS // A_KV_HEADS
A_HEAD_DIM = 128
A_ROT_DIM = 32
WINDOW = 128
A_BLOCK = 128

B_HEADS = 16
B_NOPE_DIM = 128
B_ROPE_DIM = 64
B_V_DIM = 128
B_Q_RANK = 512
B_KV_RANK = 512
B_QK_PAD = 256

C_HEADS = 16
C_HEAD_DIM = 128
NB_H = 8
NB_W = 16
C_QROWS = 8
C_KROWS = 16

LANES = 128
ROPE_SWAP = LANES // 2
V7X_VMEM_BYTES = 64 * 1024 * 1024
V7X_VMEM_LIMIT_BYTES = V7X_VMEM_BYTES - 4 * 1024 * 1024

NEG_BIG = -1e30
BF16 = jnp.bfloat16
F32 = jnp.float32
PLAIN = (False, 1.0)


def _cparams(semantics):
    return pltpu.CompilerParams(dimension_semantics=semantics,
                                vmem_limit_bytes=V7X_VMEM_LIMIT_BYTES)


def _rms(xf, g):
    ms = jnp.mean(xf * xf, axis=-1, keepdims=True)
    return xf * lax.rsqrt(ms + NORM_EPS) * g


def _dot(a, b):
    return jnp.dot(a, b, preferred_element_type=F32)


def _dot_nt(a, b):
    return lax.dot_general(a, b, (((1,), (1,)), ((), ())), preferred_element_type=F32)


def _rope(y, tabs, shift):
    r = y * tabs[0][...] + pltpu.roll(y, shift, 1) * tabs[1][...]
    if shift != ROPE_SWAP:
        r = r + pltpu.roll(y, LANES - shift, 1) * tabs[2][...]
    return r


def _store_columns(y, o_ref, tabs, shift, ops):
    if all(op == PLAIN for op in ops):
        o_ref[...] = y.astype(o_ref.dtype)
        return
    for c, (rope, scale) in enumerate(ops):
        cols = slice(c * LANES, (c + 1) * LANES)
        yc = y[:, cols]
        if scale != 1.0:
            yc = yc * scale
        if rope:
            yc = _rope(yc, tabs, shift)
        o_ref[:, cols] = yc.astype(o_ref.dtype)


def _epilogue(y, o_ref, tabs, shift, kinds):
    if len(kinds) == 1:
        _store_columns(y, o_ref, tabs, shift, kinds[0][2])
        return
    j = pl.program_id(1)
    for lo, hi, ops in kinds:
        @pl.when((j >= lo) & (j < hi))
        def _(ops=ops):
            _store_columns(y, o_ref, tabs, shift, ops)


def _norm_proj_kernel(x_ref, g_ref, w_ref, *rest, kinds, shift):
    *tabs, o_ref, h_ref = rest

    @pl.when(pl.program_id(1) == 0)
    def _():
        h_ref[...] = _rms(x_ref[...], g_ref[...]).astype(BF16)

    _epilogue(_dot(h_ref[...], w_ref[...]), o_ref, tabs, shift, kinds)


def norm_proj(x, g, w, rope, *, seq, tm, tn, kinds):
    m, d = x.shape
    n = w.shape[1]
    shift, tabs = rope
    sblk = seq // tm
    tab_spec = pl.BlockSpec((tm, LANES), lambda i, j: (i % sblk, 0))
    return pl.pallas_call(
        functools.partial(_norm_proj_kernel, kinds=kinds, shift=shift),
        grid=(m // tm, n // tn),
        in_specs=[pl.BlockSpec((tm, d), lambda i, j: (i, 0)),
                  pl.BlockSpec((1, d), lambda i, j: (0, 0)),
                  pl.BlockSpec((d, tn), lambda i, j: (0, j))] + [tab_spec] * len(tabs),
        out_specs=pl.BlockSpec((tm, tn), lambda i, j: (i, j)),
        out_shape=jax.ShapeDtypeStruct((m, n), BF16),
        scratch_shapes=[pltpu.VMEM((tm, d), BF16)],
        compiler_params=_cparams(("parallel", "arbitrary")),
        name="norm_proj",
    )(x, g.reshape(1, d), w, *tabs)


def _proj_kernel(a_ref, w_ref, *rest, kinds, shift):
    *tabs, o_ref = rest
    _epilogue(_dot(a_ref[...], w_ref[...]), o_ref, tabs, shift, kinds)


def proj(a, w, rope, *, seq, tm, tn, kinds):
    m, k = a.shape
    n = w.shape[1]
    shift, tabs = rope
    sblk = seq // tm
    tab_spec = pl.BlockSpec((tm, LANES), lambda i, j: (i % sblk, 0))
    return pl.pallas_call(
        functools.partial(_proj_kernel, kinds=kinds, shift=shift),
        grid=(m // tm, n // tn),
        in_specs=[pl.BlockSpec((tm, k), lambda i, j: (i, 0)),
                  pl.BlockSpec((k, tn), lambda i, j: (0, j))] + [tab_spec] * len(tabs),
        out_specs=pl.BlockSpec((tm, tn), lambda i, j: (i, j)),
        out_shape=jax.ShapeDtypeStruct((m, n), BF16),
        compiler_params=_cparams(("parallel", "arbitrary")),
        name="proj",
    )(a, w, *tabs)


def _out_proj_kernel(a_ref, w_ref, g_ref, x_ref, o_ref):
    m = _dot(a_ref[...], w_ref[...])
    o_ref[...] = x_ref[...] + _rms(m, g_ref[...])


def out_proj_residual(a, w, g, x, *, tm):
    m, k = a.shape
    d = w.shape[1]
    return pl.pallas_call(
        _out_proj_kernel,
        grid=(m // tm,),
        in_specs=[pl.BlockSpec((tm, k), lambda i: (i, 0)),
                  pl.BlockSpec((k, d), lambda i: (0, 0), pipeline_mode=pl.Buffered(1)),
                  pl.BlockSpec((1, d), lambda i: (0, 0)),
                  pl.BlockSpec((tm, d), lambda i: (i, 0))],
        out_specs=pl.BlockSpec((tm, d), lambda i: (i, 0)),
        out_shape=jax.ShapeDtypeStruct((m, d), F32),
        compiler_params=_cparams(("parallel",)),
        name="out_proj_residual",
    )(a, w, g.reshape(1, d), x)


def _ffn_kernel(x_ref, gpre_ref, wg_ref, wu_ref, wo_ref, gpost_ref, o_ref, h_ref):
    j = pl.program_id(1)

    @pl.when(j == 0)
    def _():
        h_ref[...] = _rms(x_ref[...], gpre_ref[...]).astype(BF16)
        o_ref[...] = jnp.zeros_like(o_ref)

    h = h_ref[...]
    gate = _dot(h, wg_ref[...])
    up = _dot(h, wu_ref[...])
    act = (gate / (1.0 + jnp.exp(-gate)) * up).astype(BF16)
    o_ref[...] += _dot(act, wo_ref[...])

    @pl.when(j == pl.num_programs(1) - 1)
    def _():
        o_ref[...] = x_ref[...] + _rms(o_ref[...], gpost_ref[...])


def ffn(x, gpre, w_in, w_out, gpost, *, tm, tf):
    m, d = x.shape
    dff = w_out.shape[0]
    nf = dff // tf
    return pl.pallas_call(
        _ffn_kernel,
        grid=(m // tm, nf),
        in_specs=[pl.BlockSpec((tm, d), lambda i, j: (i, 0)),
                  pl.BlockSpec((1, d), lambda i, j: (0, 0)),
                  pl.BlockSpec((d, tf), lambda i, j: (0, j)),
                  pl.BlockSpec((d, tf), lambda i, j: (0, nf + j)),
                  pl.BlockSpec((tf, d), lambda i, j: (j, 0)),
                  pl.BlockSpec((1, d), lambda i, j: (0, 0))],
        out_specs=pl.BlockSpec((tm, d), lambda i, j: (i, 0)),
        out_shape=jax.ShapeDtypeStruct((m, d), F32),
        scratch_shapes=[pltpu.VMEM((tm, d), BF16)],
        compiler_params=_cparams(("parallel", "arbitrary")),
        name="ffn",
    )(x, gpre.reshape(1, d), w_in, w_in, w_out, gpost.reshape(1, d))


def _win_attn_kernel(sink_ref, q_ref, k_ref, v_ref, o_ref, s_ref, p_ref, *, seq):
    i = pl.program_id(1)
    kwin = 3 * A_BLOCK
    start = pl.multiple_of(jnp.clip((i - 1) * A_BLOCK, 0, seq - kwin), A_BLOCK)
    rows = A_GROUP * A_BLOCK
    qpos = i * A_BLOCK + (lax.broadcasted_iota(jnp.int32, (rows, kwin), 0) & (A_BLOCK - 1))
    kpos = start + lax.broadcasted_iota(jnp.int32, (rows, kwin), 1)
    valid = jnp.abs(kpos - qpos) <= WINDOW
    for kv in range(A_KV_HEADS):
        q4 = jnp.concatenate(
            [q_ref[0, :, (kv * A_GROUP + gi) * A_HEAD_DIM:(kv * A_GROUP + gi + 1) * A_HEAD_DIM]
             for gi in range(A_GROUP)], axis=0)
        kw = k_ref[0, pl.ds(start, kwin), kv * A_HEAD_DIM:(kv + 1) * A_HEAD_DIM]
        s_ref[kv] = jnp.where(valid, _dot_nt(q4, kw), NEG_BIG)
    denoms = []
    for kv in range(A_KV_HEADS):
        s = s_ref[kv]
        sink = jnp.concatenate(
            [jnp.full((A_BLOCK, 1), sink_ref[kv * A_GROUP + gi], F32) for gi in range(A_GROUP)],
            axis=0)
        mx = jnp.maximum(jnp.max(s, axis=-1, keepdims=True), sink)
        p = jnp.exp(s - mx)
        denoms.append(jnp.sum(p, axis=-1, keepdims=True) + jnp.exp(sink - mx))
        p_ref[kv] = p.astype(BF16)
    for kv in range(A_KV_HEADS):
        vw = v_ref[0, pl.ds(start, kwin), kv * A_HEAD_DIM:(kv + 1) * A_HEAD_DIM]
        o = _dot(p_ref[kv], vw) / denoms[kv]
        for gi in range(A_GROUP):
            h = kv * A_GROUP + gi
            o_ref[0, :, h * A_HEAD_DIM:(h + 1) * A_HEAD_DIM] = (
                o[gi * A_BLOCK:(gi + 1) * A_BLOCK, :].astype(o_ref.dtype))


def window_attention(qkv, sinks):
    b, s, _ = qkv.shape
    dq = A_HEADS * A_HEAD_DIM
    dkv = A_KV_HEADS * A_HEAD_DIM
    score_shape = (A_KV_HEADS, A_GROUP * A_BLOCK, 3 * A_BLOCK)
    return pl.pallas_call(
        functools.partial(_win_attn_kernel, seq=s),
        grid=(b, s // A_BLOCK),
        in_specs=[pl.BlockSpec(memory_space=pltpu.SMEM),
                  pl.BlockSpec((1, A_BLOCK, dq), lambda bi, i: (bi, i, 0)),
                  pl.BlockSpec((1, s, dkv), lambda bi, i: (bi, 0, dq // dkv)),
                  pl.BlockSpec((1, s, dkv), lambda bi, i: (bi, 0, dq // dkv + 1))],
        out_specs=pl.BlockSpec((1, A_BLOCK, dq), lambda bi, i: (bi, i, 0)),
        out_shape=jax.ShapeDtypeStruct((b, s, dq), BF16),
        scratch_shapes=[pltpu.VMEM(score_shape, F32), pltpu.VMEM(score_shape, BF16)],
        compiler_params=_cparams(("parallel", "arbitrary")),
        name="window_attention",
    )(sinks, qkv, qkv, qkv)


def _mla_down_kernel(x_ref, g_ref, wq_ref, wkv_ref, wkr_ref, qn_ref, kvn_ref, *rest, shift):
    *tabs, cq_ref, ckv_ref = rest
    h = _rms(x_ref[...], g_ref[...]).astype(BF16)
    cq_ref[...] = _rms(_dot(h, wq_ref[...]), qn_ref[...]).astype(BF16)
    ckv_ref[:, :B_KV_RANK] = _rms(_dot(h, wkv_ref[...]), kvn_ref[...]).astype(BF16)
    ckv_ref[:, B_KV_RANK:] = _rope(_dot(h, wkr_ref[...]), tabs, shift).astype(BF16)


def mla_down(x, g, wq, wkv, wkr, qn, kvn, rope, *, seq, tm):
    m, d = x.shape
    shift, tabs = rope
    sblk = seq // tm
    tab_spec = pl.BlockSpec((tm, LANES), lambda i: (i % sblk, 0))
    full = lambda shape: pl.BlockSpec(shape, lambda i: (0, 0))
    return pl.pallas_call(
        functools.partial(_mla_down_kernel, shift=shift),
        grid=(m // tm,),
        in_specs=[pl.BlockSpec((tm, d), lambda i: (i, 0)), full((1, d)),
                  full((d, B_Q_RANK)), full((d, B_KV_RANK)), full((d, LANES)),
                  full((1, B_Q_RANK)), full((1, B_KV_RANK))] + [tab_spec] * len(tabs),
        out_specs=[pl.BlockSpec((tm, B_Q_RANK), lambda i: (i, 0)),
                   pl.BlockSpec((tm, B_KV_RANK + LANES), lambda i: (i, 0))],
        out_shape=[jax.ShapeDtypeStruct((m, B_Q_RANK), BF16),
                   jax.ShapeDtypeStruct((m, B_KV_RANK + LANES), BF16)],
        compiler_params=_cparams(("parallel",)),
        name="mla_down",
    )(x, g.reshape(1, d), wq, wkv, wkr, qn.reshape(1, -1), kvn.reshape(1, -1), *tabs)


def _mla_attn_kernel(q_ref, k_ref, v_ref, o_ref, *, chunk):
    v = v_ref[0]
    v_ones = jnp.concatenate([v, jnp.ones_like(v)], axis=1)
    dv = v.shape[1]
    for c in range(q_ref.shape[1] // chunk):
        rows = slice(c * chunk, (c + 1) * chunk)
        s = _dot_nt(q_ref[0, rows, :], k_ref[0])
        p = jnp.exp(s - jnp.max(s, axis=-1, keepdims=True))
        o = _dot(p.astype(BF16), v_ones)
        o_ref[0, rows, :] = (o[:, :dv] / o[:, dv:]).astype(o_ref.dtype)


def mla_attention(qcat, kvcat, *, tq, chunk):
    b, s, _ = qcat.shape
    vblk0 = B_HEADS * B_QK_PAD // B_V_DIM
    return pl.pallas_call(
        functools.partial(_mla_attn_kernel, chunk=chunk),
        grid=(b, B_HEADS, s // tq),
        in_specs=[pl.BlockSpec((1, tq, B_QK_PAD), lambda bi, h, qi: (bi, qi, h)),
                  pl.BlockSpec((1, s, B_QK_PAD), lambda bi, h, qi: (bi, 0, h)),
                  pl.BlockSpec((1, s, B_V_DIM), lambda bi, h, qi: (bi, 0, vblk0 + h))],
        out_specs=pl.BlockSpec((1, tq, B_V_DIM), lambda bi, h, qi: (bi, qi, h)),
        out_shape=jax.ShapeDtypeStruct((b, s, B_HEADS * B_V_DIM), BF16),
        compiler_params=_cparams(("parallel", "parallel", "arbitrary")),
        name="mla_attention",
    )(qcat, kvcat, kvcat)


def _nb_attn_kernel(q_ref, k_ref, v_ref, t_ref, o_ref, *, rows):
    rb = pl.program_id(2)
    nk = C_KROWS * GRID_W
    start = pl.multiple_of(
        jnp.clip(rb * C_QROWS - NB_H // 2, 0, rows - C_KROWS) * GRID_W, (NB_H // 2) * GRID_W)
    kw = k_ref[0, pl.ds(start, nk), :]
    vw = v_ref[0, pl.ds(start, nk), :]
    s = _dot_nt(q_ref[0], kw) + t_ref[0, 0]
    p = jnp.exp(s - jnp.max(s, axis=-1, keepdims=True))
    o = _dot(p.astype(BF16), jnp.concatenate([vw, jnp.ones_like(vw)], axis=1))
    o_ref[0] = (o[:, :C_HEAD_DIM] / o[:, C_HEAD_DIM:]).astype(o_ref.dtype)


def neighborhood_attention(qkv, table):
    b, s, _ = qkv.shape
    rows = s // GRID_W
    nrb = rows // C_QROWS
    tq = C_QROWS * GRID_W
    nk = C_KROWS * GRID_W

    def t_index(bi, h, rb):
        return (h, jnp.where(rb == 0, 0, jnp.where(rb == nrb - 1, 2, 1)), 0, 0)

    return pl.pallas_call(
        functools.partial(_nb_attn_kernel, rows=rows),
        grid=(b, C_HEADS, nrb),
        in_specs=[pl.BlockSpec((1, tq, C_HEAD_DIM), lambda bi, h, rb: (bi, rb, h)),
                  pl.BlockSpec((1, s, C_HEAD_DIM), lambda bi, h, rb: (bi, 0, C_HEADS + h)),
                  pl.BlockSpec((1, s, C_HEAD_DIM), lambda bi, h, rb: (bi, 0, 2 * C_HEADS + h)),
                  pl.BlockSpec((1, 1, tq, nk), t_index)],
        out_specs=pl.BlockSpec((1, tq, C_HEAD_DIM), lambda bi, h, rb: (bi, rb, h)),
        out_shape=jax.ShapeDtypeStruct((b, s, C_HEADS * C_HEAD_DIM), BF16),
        compiler_params=_cparams(("parallel", "parallel", "arbitrary")),
        name="neighborhood_attention",
    )(qkv, qkv, qkv, table)


def _nb_bias_table(rel_bias, rows):
    nh = rel_bias.shape[0]
    c = np.arange(GRID_W)[:, None]
    kc = np.arange(GRID_W)[None, :]
    cs = np.clip(c - NB_W // 2, 0, GRID_W - NB_W)
    col_ok = (kc >= cs) & (kc < cs + NB_W)
    dc = np.clip(kc - c + NB_W - 1, 0, 2 * NB_W - 2)
    bias_c = jnp.where(col_ok[None, None], rel_bias.astype(F32)[:, :, dc], NEG_BIG)
    r0s = np.array([0, C_QROWS, rows - C_QROWS])
    w0s = np.clip(r0s - NB_H // 2, 0, rows - C_KROWS)
    r = r0s[:, None, None] + np.arange(C_QROWS)[None, :, None]
    kr = w0s[:, None, None] + np.arange(C_KROWS)[None, None, :]
    rs = np.clip(r - NB_H // 2, 0, rows - NB_H)
    row_ok = (kr >= rs) & (kr < rs + NB_H)
    dr = np.clip(kr - r + NB_H - 1, 0, 2 * NB_H - 2)
    t = bias_c[:, dr]
    t = jnp.where(row_ok[None, :, :, :, None, None], t, NEG_BIG)
    t = t.transpose(0, 1, 2, 4, 3, 5)
    return t.reshape(nh, 3, C_QROWS * GRID_W, C_KROWS * GRID_W)


def _rope_angles(seq, dim):
    pos = jnp.arange(seq, dtype=F32)
    inv = ROPE_THETA ** (-jnp.arange(0, dim, 2, dtype=F32) / dim)
    ang = pos[:, None] * inv[None, :]
    return jnp.cos(ang), jnp.sin(ang)


def _rope_adjacent(seq, dim):
    half = dim // 2
    c, sn = _rope_angles(seq, dim)
    cos = jnp.ones((seq, LANES), F32).at[:, :dim].set(jnp.concatenate([c, c], axis=1))
    up = jnp.zeros((seq, LANES), F32).at[:, half:dim].set(sn)
    down = jnp.zeros((seq, LANES), F32).at[:, :half].set(-sn)
    return half, (cos, up, down)


def _rope_swapped(seq, dim):
    half = dim // 2
    c, sn = _rope_angles(seq, dim)
    cos = jnp.ones((seq, LANES), F32).at[:, :half].set(c).at[:, ROPE_SWAP:ROPE_SWAP + half].set(c)
    sin = jnp.zeros((seq, LANES), F32).at[:, :half].set(-sn).at[:, ROPE_SWAP:ROPE_SWAP + half].set(sn)
    return ROPE_SWAP, (cos, sin)


def _pair_split(w, dim):
    half = dim // 2
    gap = ROPE_SWAP - half
    return jnp.concatenate([w[..., :half], w[..., dim:dim + gap], w[..., half:dim],
                            w[..., dim + gap:]], axis=-1)


def kernel(x, pre_mix_norm, post_mix_norm, pre_ffn_norm, post_ffn_norm, a_w_qkv, a_sinks, a_w_o,
           b_w_down, b_q_norm, b_w_uq, b_kv_norm, b_w_ukv, b_w_o, c_w_qkv, c_rel_bias, c_w_o,
           ffn_w_in, ffn_w_out):
    bsz, seq, d = x.shape
    m = bsz * seq
    rows = seq // GRID_W
    rope_a = _rope_adjacent(seq, A_ROT_DIM)
    rope_b = _rope_swapped(seq, B_ROPE_DIM)
    tn = 1024
    cols = tn // LANES

    xf = x.reshape(m, d)
    for i in range(DEPTH):
        kind, slot = i % N_MIXERS, i // N_MIXERS
        if kind == 0:
            q_ops = ((True, A_HEAD_DIM ** -0.5),) * cols
            kv_ops = ((True, 1.0),) * A_KV_HEADS + (PLAIN,) * A_KV_HEADS
            qkv = norm_proj(xf, pre_mix_norm[i], a_w_qkv[slot].astype(BF16), rope_a, seq=seq,
                            tm=1024, tn=tn, kinds=((0, 2, q_ops), (2, 3, kv_ops)))
            o = window_attention(qkv.reshape(bsz, seq, -1), a_sinks[slot].astype(F32))
            w_o = a_w_o[slot]
        elif kind == 1:
            wd = b_w_down[slot]
            wkr = jnp.pad(wd[:, B_Q_RANK + B_KV_RANK:], ((0, 0), (0, LANES - B_ROPE_DIM)))
            cq, ckv = mla_down(xf, pre_mix_norm[i], wd[:, :B_Q_RANK].astype(BF16),
                               wd[:, B_Q_RANK:B_Q_RANK + B_KV_RANK].astype(BF16),
                               _pair_split(wkr, B_ROPE_DIM).astype(BF16),
                               b_q_norm[slot], b_kv_norm[slot], rope_b, seq=seq, tm=512)
            wuq = b_w_uq[slot].reshape(B_Q_RANK, B_HEADS, B_NOPE_DIM + B_ROPE_DIM)
            wuq_r = jnp.pad(wuq[:, :, B_NOPE_DIM:], ((0, 0), (0, 0), (0, LANES - B_ROPE_DIM)))
            wuq = jnp.concatenate([wuq[:, :, :B_NOPE_DIM], _pair_split(wuq_r, B_ROPE_DIM)], axis=2)
            scale = (B_NOPE_DIM + B_ROPE_DIM) ** -0.5
            q_ops = ((False, scale), (True, scale)) * (cols // 2)
            qcat = proj(cq, wuq.reshape(B_Q_RANK, -1).astype(BF16), rope_b, seq=seq,
                        tm=1024, tn=tn, kinds=((0, 1, q_ops),))
            wukv = b_w_ukv[slot].reshape(B_KV_RANK, B_HEADS, B_NOPE_DIM + B_V_DIM)
            eye = jnp.broadcast_to(jnp.eye(LANES, dtype=F32)[:, None, :], (LANES, B_HEADS, LANES))
            wk = jnp.concatenate(
                [jnp.pad(wukv[:, :, :B_NOPE_DIM], ((0, LANES), (0, 0), (0, 0))),
                 jnp.pad(eye, ((B_KV_RANK, 0), (0, 0), (0, 0)))], axis=2)
            wv = jnp.pad(wukv[:, :, B_NOPE_DIM:], ((0, LANES), (0, 0), (0, 0)))
            wkv = jnp.concatenate([wk.reshape(B_KV_RANK + LANES, -1),
                                   wv.reshape(B_KV_RANK + LANES, -1)], axis=1).astype(BF16)
            kvcat = proj(ckv, wkv, rope_b, seq=seq, tm=1024, tn=tn,
                         kinds=((0, 1, (PLAIN,) * cols),))
            o = mla_attention(qcat.reshape(bsz, seq, -1), kvcat.reshape(bsz, seq, -1), tq=2048,
                              chunk=256)
            w_o = b_w_o[slot]
        else:
            q_ops = ((False, C_HEAD_DIM ** -0.5),) * cols
            nq = C_HEADS * C_HEAD_DIM // tn
            qkv = norm_proj(xf, pre_mix_norm[i], c_w_qkv[slot].astype(BF16), rope_a, seq=seq,
                            tm=1024, tn=tn, kinds=((0, nq, q_ops), (nq, 3 * nq, (PLAIN,) * cols)))
            table = _nb_bias_table(c_rel_bias[slot], rows)
            o = neighborhood_attention(qkv.reshape(bsz, seq, -1), table)
            w_o = c_w_o[slot]
        xf = out_proj_residual(o.reshape(m, -1), w_o.astype(BF16), post_mix_norm[i], xf, tm=512)
        xf = ffn(xf, pre_ffn_norm[i], ffn_w_in[i].astype(BF16), ffn_w_out[i].astype(BF16),
                 post_ffn_norm[i], tm=1024, tf=256)
    return xf.reshape(bsz, seq, d)
```

```python
import functools

import jax
import jax.numpy as jnp
import numpy as np
from jax import lax
from jax.experimental import pallas as pl
from jax.experimental.pallas import tpu as pltpu

D_MODEL = 2048
DEPTH = 4
N_MIXERS = 3
GRID_W = 64
ROPE_THETA = 500000.0
NORM_EPS = 1e-6
D_FF = 5632

A_HEADS = 16
A_KV_HEADS = 4
A_GROUP = A_HEADS // A_KV_HEADS
A_HEAD_DIM = 128
A_ROT_DIM = 32
WINDOW = 128
A_BLOCK = 128

B_HEADS = 16
B_NOPE_DIM = 128
B_ROPE_DIM = 64
B_V_DIM = 128
B_Q_RANK = 512
B_KV_RANK = 512
B_QK_PAD = 256

C_HEADS = 16
C_HEAD_DIM = 128
NB_H = 8
NB_W = 16
C_QROWS = 8
C_KROWS = 16

LANES = 128
ROPE_SWAP = LANES // 2
V7X_VMEM_BYTES = 64 * 1024 * 1024
V7X_VMEM_LIMIT_BYTES = V7X_VMEM_BYTES - 4 * 1024 * 1024

NEG_BIG = -1e30
LOG2E = float(np.log2(np.e))
BF16 = jnp.bfloat16
F32 = jnp.float32
PLAIN = (False, 1.0)


def _cparams(semantics):
    return pltpu.CompilerParams(dimension_semantics=semantics,
                                vmem_limit_bytes=V7X_VMEM_LIMIT_BYTES)


def _rms(xf, g):
    ms = jnp.mean(xf * xf, axis=-1, keepdims=True)
    return xf * lax.rsqrt(ms + NORM_EPS) * g


def _dot(a, b):
    return jnp.dot(a, b, preferred_element_type=F32)


def _dot_nt(a, b):
    return lax.dot_general(a, b, (((1,), (1,)), ((), ())), preferred_element_type=F32)


def _rope(y, tabs, shift):
    r = y * tabs[0][...] + pltpu.roll(y, shift, 1) * tabs[1][...]
    if shift != ROPE_SWAP:
        r = r + pltpu.roll(y, LANES - shift, 1) * tabs[2][...]
    return r


def _store_columns(y, o_ref, tabs, shift, ops):
    if all(op == PLAIN for op in ops):
        o_ref[...] = y.astype(o_ref.dtype)
        return
    for c, (rope, scale) in enumerate(ops):
        cols = slice(c * LANES, (c + 1) * LANES)
        yc = y[:, cols]
        if scale != 1.0:
            yc = yc * scale
        if rope:
            yc = _rope(yc, tabs, shift)
        o_ref[:, cols] = yc.astype(o_ref.dtype)


def _epilogue(y, o_ref, tabs, shift, kinds):
    if len(kinds) == 1:
        _store_columns(y, o_ref, tabs, shift, kinds[0][2])
        return
    j = pl.program_id(1)
    for lo, hi, ops in kinds:
        @pl.when((j >= lo) & (j < hi))
        def _(ops=ops):
            _store_columns(y, o_ref, tabs, shift, ops)


def _norm_proj_kernel(x_ref, g_ref, w_ref, *rest, kinds, shift):
    *tabs, o_ref, h_ref = rest

    @pl.when(pl.program_id(1) == 0)
    def _():
        h_ref[...] = _rms(x_ref[...], g_ref[...]).astype(BF16)

    _epilogue(_dot(h_ref[...], w_ref[...]), o_ref, tabs, shift, kinds)


def norm_proj(x, g, w, rope, *, seq, tm, tn, kinds):
    m, d = x.shape
    n = w.shape[1]
    shift, tabs = rope
    sblk = seq // tm
    tab_spec = pl.BlockSpec((tm, LANES), lambda i, j: (i % sblk, 0))
    return pl.pallas_call(
        functools.partial(_norm_proj_kernel, kinds=kinds, shift=shift),
        grid=(m // tm, n // tn),
        in_specs=[pl.BlockSpec((tm, d), lambda i, j: (i, 0)),
                  pl.BlockSpec((1, d), lambda i, j: (0, 0)),
                  pl.BlockSpec((d, tn), lambda i, j: (0, j))] + [tab_spec] * len(tabs),
        out_specs=pl.BlockSpec((tm, tn), lambda i, j: (i, j)),
        out_shape=jax.ShapeDtypeStruct((m, n), BF16),
        scratch_shapes=[pltpu.VMEM((tm, d), BF16)],
        compiler_params=_cparams(("parallel", "arbitrary")),
        name="norm_proj",
    )(x, g.reshape(1, d), w, *tabs)


def _proj_kernel(a_ref, w_ref, *rest, kinds, shift):
    *tabs, o_ref = rest
    _epilogue(_dot(a_ref[...], w_ref[...]), o_ref, tabs, shift, kinds)


def proj(a, w, rope, *, seq, tm, tn, kinds):
    m, k = a.shape
    n = w.shape[1]
    shift, tabs = rope
    sblk = seq // tm
    tab_spec = pl.BlockSpec((tm, LANES), lambda i, j: (i % sblk, 0))
    return pl.pallas_call(
        functools.partial(_proj_kernel, kinds=kinds, shift=shift),
        grid=(m // tm, n // tn),
        in_specs=[pl.BlockSpec((tm, k), lambda i, j: (i, 0)),
                  pl.BlockSpec((k, tn), lambda i, j: (0, j))] + [tab_spec] * len(tabs),
        out_specs=pl.BlockSpec((tm, tn), lambda i, j: (i, j)),
        out_shape=jax.ShapeDtypeStruct((m, n), BF16),
        compiler_params=_cparams(("parallel", "arbitrary")),
        name="proj",
    )(a, w, *tabs)


def _out_proj_kernel(a_ref, w_ref, g_ref, x_ref, o_ref):
    m = _dot(a_ref[...], w_ref[...])
    o_ref[...] = x_ref[...] + _rms(m, g_ref[...])


def out_proj_residual(a, w, g, x, *, tm):
    m, k = a.shape
    d = w.shape[1]
    return pl.pallas_call(
        _out_proj_kernel,
        grid=(m // tm,),
        in_specs=[pl.BlockSpec((tm, k), lambda i: (i, 0)),
                  pl.BlockSpec((k, d), lambda i: (0, 0), pipeline_mode=pl.Buffered(1)),
                  pl.BlockSpec((1, d), lambda i: (0, 0)),
                  pl.BlockSpec((tm, d), lambda i: (i, 0))],
        out_specs=pl.BlockSpec((tm, d), lambda i: (i, 0)),
        out_shape=jax.ShapeDtypeStruct((m, d), F32),
        compiler_params=_cparams(("parallel",)),
        name="out_proj_residual",
    )(a, w, g.reshape(1, d), x)


def _ffn_kernel(x_ref, gpre_ref, wg_ref, wu_ref, wo_ref, gpost_ref, o_ref, h_ref):
    j = pl.program_id(1)

    @pl.when(j == 0)
    def _():
        h_ref[...] = _rms(x_ref[...], gpre_ref[...]).astype(BF16)
        o_ref[...] = jnp.zeros_like(o_ref)

    h = h_ref[...]
    gate = _dot(h, wg_ref[...])
    up = _dot(h, wu_ref[...])
    act = (gate / (1.0 + jnp.exp(-gate)) * up).astype(BF16)
    o_ref[...] += _dot(act, wo_ref[...])

    @pl.when(j == pl.num_programs(1) - 1)
    def _():
        o_ref[...] = x_ref[...] + _rms(o_ref[...], gpost_ref[...])


def ffn(x, gpre, w_in, w_out, gpost, *, tm, tf):
    m, d = x.shape
    dff = w_out.shape[0]
    nf = dff // tf
    return pl.pallas_call(
        _ffn_kernel,
        grid=(m // tm, nf),
        in_specs=[pl.BlockSpec((tm, d), lambda i, j: (i, 0)),
                  pl.BlockSpec((1, d), lambda i, j: (0, 0)),
                  pl.BlockSpec((d, tf), lambda i, j: (0, j)),
                  pl.BlockSpec((d, tf), lambda i, j: (0, nf + j)),
                  pl.BlockSpec((tf, d), lambda i, j: (j, 0)),
                  pl.BlockSpec((1, d), lambda i, j: (0, 0))],
        out_specs=pl.BlockSpec((tm, d), lambda i, j: (i, 0)),
        out_shape=jax.ShapeDtypeStruct((m, d), F32),
        scratch_shapes=[pltpu.VMEM((tm, d), BF16)],
        compiler_params=_cparams(("parallel", "arbitrary")),
        name="ffn",
    )(x, gpre.reshape(1, d), w_in, w_in, w_out, gpost.reshape(1, d))


def _win_attn_kernel(sink_ref, q_ref, k_ref, v_ref, o_ref, *, seq, nblk):
    kwin = 3 * A_BLOCK
    cols = A_GROUP * A_BLOCK
    for jb in range(nblk):
        blk = pl.program_id(1) * nblk + jb
        start = pl.multiple_of(jnp.clip((blk - 1) * A_BLOCK, 0, seq - kwin), A_BLOCK)
        kpos = start + lax.broadcasted_iota(jnp.int32, (kwin, cols), 0)
        qpos = blk * A_BLOCK + (lax.broadcasted_iota(jnp.int32, (kwin, cols), 1) & (A_BLOCK - 1))
        valid = jnp.abs(kpos - qpos) <= WINDOW
        qrows = slice(jb * A_BLOCK, (jb + 1) * A_BLOCK)
        for kv in range(A_KV_HEADS):
            heads = range(kv * A_GROUP, (kv + 1) * A_GROUP)
            kvcols = slice(kv * A_HEAD_DIM, (kv + 1) * A_HEAD_DIM)
            q4 = jnp.concatenate(
                [q_ref[0, qrows, h * A_HEAD_DIM:(h + 1) * A_HEAD_DIM] for h in heads], axis=0)
            kw = k_ref[0, pl.ds(start, kwin), kvcols]
            vw = v_ref[0, pl.ds(start, kwin), kvcols]
            st = jnp.where(valid, _dot_nt(kw, q4), NEG_BIG)
            sink = jnp.concatenate(
                [jnp.full((1, A_BLOCK), sink_ref[h] * LOG2E, F32) for h in heads], axis=1)
            mx = jnp.maximum(jnp.max(st, axis=0, keepdims=True), sink)
            pt = jnp.exp2(st - mx)
            denom = jnp.sum(pt, axis=0, keepdims=True) + jnp.exp2(sink - mx)
            ot = lax.dot_general(vw, pt.astype(BF16), (((0,), (0,)), ((), ())),
                                 preferred_element_type=F32) / denom
            for gi, h in enumerate(heads):
                o_ref[0, qrows, h * A_HEAD_DIM:(h + 1) * A_HEAD_DIM] = (
                    ot[:, gi * A_BLOCK:(gi + 1) * A_BLOCK].T.astype(o_ref.dtype))


def window_attention(qkv, sinks, *, nblk):
    b, s, _ = qkv.shape
    dq = A_HEADS * A_HEAD_DIM
    dkv = A_KV_HEADS * A_HEAD_DIM
    tq = nblk * A_BLOCK
    return pl.pallas_call(
        functools.partial(_win_attn_kernel, seq=s, nblk=nblk),
        grid=(b, s // tq),
        in_specs=[pl.BlockSpec(memory_space=pltpu.SMEM),
                  pl.BlockSpec((1, tq, dq), lambda bi, i: (bi, i, 0)),
                  pl.BlockSpec((1, s, dkv), lambda bi, i: (bi, 0, dq // dkv)),
                  pl.BlockSpec((1, s, dkv), lambda bi, i: (bi, 0, dq // dkv + 1))],
        out_specs=pl.BlockSpec((1, tq, dq), lambda bi, i: (bi, i, 0)),
        out_shape=jax.ShapeDtypeStruct((b, s, dq), BF16),
        compiler_params=_cparams(("parallel", "arbitrary")),
        name="window_attention",
    )(sinks, qkv, qkv, qkv)


def _mla_down_kernel(x_ref, g_ref, wq_ref, wkv_ref, wkr_ref, qn_ref, kvn_ref, *rest, shift):
    *tabs, cq_ref, ckv_ref = rest
    h = _rms(x_ref[...], g_ref[...]).astype(BF16)
    cq_ref[...] = _rms(_dot(h, wq_ref[...]), qn_ref[...]).astype(BF16)
    ckv_ref[:, :B_KV_RANK] = _rms(_dot(h, wkv_ref[...]), kvn_ref[...]).astype(BF16)
    ckv_ref[:, B_KV_RANK:] = _rope(_dot(h, wkr_ref[...]), tabs, shift).astype(BF16)


def mla_down(x, g, wq, wkv, wkr, qn, kvn, rope, *, seq, tm):
    m, d = x.shape
    shift, tabs = rope
    sblk = seq // tm
    tab_spec = pl.BlockSpec((tm, LANES), lambda i: (i % sblk, 0))
    full = lambda shape: pl.BlockSpec(shape, lambda i: (0, 0))
    return pl.pallas_call(
        functools.partial(_mla_down_kernel, shift=shift),
        grid=(m // tm,),
        in_specs=[pl.BlockSpec((tm, d), lambda i: (i, 0)), full((1, d)),
                  full((d, B_Q_RANK)), full((d, B_KV_RANK)), full((d, LANES)),
                  full((1, B_Q_RANK)), full((1, B_KV_RANK))] + [tab_spec] * len(tabs),
        out_specs=[pl.BlockSpec((tm, B_Q_RANK), lambda i: (i, 0)),
                   pl.BlockSpec((tm, B_KV_RANK + LANES), lambda i: (i, 0))],
        out_shape=[jax.ShapeDtypeStruct((m, B_Q_RANK), BF16),
                   jax.ShapeDtypeStruct((m, B_KV_RANK + LANES), BF16)],
        compiler_params=_cparams(("parallel",)),
        name="mla_down",
    )(x, g.reshape(1, d), wq, wkv, wkr, qn.reshape(1, -1), kvn.reshape(1, -1), *tabs)


def _mla_attn_kernel(q_ref, k_ref, v_ref, o_ref, *, chunk):
    v = v_ref[0]
    v_ones = jnp.concatenate([v, jnp.ones_like(v)], axis=1)
    dv = v.shape[1]
    for c in range(q_ref.shape[1] // chunk):
        rows = slice(c * chunk, (c + 1) * chunk)
        s = _dot_nt(q_ref[0, rows, :], k_ref[0])
        p = jnp.exp2(s - jnp.max(s, axis=-1, keepdims=True))
        o = _dot(p.astype(BF16), v_ones)
        o_ref[0, rows, :] = (o[:, :dv] / o[:, dv:]).astype(o_ref.dtype)


def mla_attention(qcat, kvcat, *, tq, chunk):
    b, s, _ = qcat.shape
    vblk0 = B_HEADS * B_QK_PAD // B_V_DIM
    return pl.pallas_call(
        functools.partial(_mla_attn_kernel, chunk=chunk),
        grid=(b, B_HEADS, s // tq),
        in_specs=[pl.BlockSpec((1, tq, B_QK_PAD), lambda bi, h, qi: (bi, qi, h)),
                  pl.BlockSpec((1, s, B_QK_PAD), lambda bi, h, qi: (bi, 0, h)),
                  pl.BlockSpec((1, s, B_V_DIM), lambda bi, h, qi: (bi, 0, vblk0 + h))],
        out_specs=pl.BlockSpec((1, tq, B_V_DIM), lambda bi, h, qi: (bi, qi, h)),
        out_shape=jax.ShapeDtypeStruct((b, s, B_HEADS * B_V_DIM), BF16),
        compiler_params=_cparams(("parallel", "parallel", "arbitrary")),
        name="mla_attention",
    )(qcat, kvcat, kvcat)


N_DR = 2 * NB_H - 1
SLAB_LOW = N_DR - 1
SLAB_HIGH = SLAB_LOW + N_DR
SLAB_NONE = SLAB_HIGH + N_DR


def _nb_slab_plan(rows):
    plan = []
    for rb in range(rows // C_QROWS):
        w0 = min(max(rb * C_QROWS - NB_H // 2, 0), rows - C_KROWS)
        block = []
        for i in range(C_QROWS):
            r = rb * C_QROWS + i
            rs = min(max(r - NB_H // 2, 0), rows - NB_H)
            ids = []
            for pair in range(C_KROWS // 2):
                kr = w0 + 2 * pair
                ok0, ok1 = rs <= kr < rs + NB_H, rs <= kr + 1 < rs + NB_H
                dr = kr - r + NB_H - 1
                ids.append(dr if ok0 and ok1 else SLAB_LOW + dr if ok0
                           else SLAB_HIGH + dr + 1 if ok1 else SLAB_NONE)
            block.append(tuple(ids))
        plan.append(tuple(block))
    return tuple(plan)


def _nb_attn_kernel(q_ref, k_ref, v_ref, t_ref, o_ref, *, plan, rows):
    tq = C_QROWS * GRID_W
    nk = C_KROWS * GRID_W
    biases = {}
    for rb, block in enumerate(plan):
        if block not in biases:
            biases[block] = jnp.concatenate(
                [jnp.concatenate([t_ref[0, sid] for sid in ids], axis=1) for ids in block], axis=0)
        start = min(max(rb * C_QROWS - NB_H // 2, 0), rows - C_KROWS) * GRID_W
        qrows = slice(rb * tq, (rb + 1) * tq)
        kw = k_ref[0, start:start + nk, :]
        vw = v_ref[0, start:start + nk, :]
        s = _dot_nt(q_ref[0, qrows, :], kw) + biases[block]
        p = jnp.exp2(s - jnp.max(s, axis=-1, keepdims=True))
        o = _dot(p.astype(BF16), jnp.concatenate([vw, jnp.ones_like(vw)], axis=1))
        o_ref[0, qrows, :] = (o[:, :C_HEAD_DIM] / o[:, C_HEAD_DIM:]).astype(o_ref.dtype)


def neighborhood_attention(qkv, slabs):
    b, s, _ = qkv.shape
    rows = s // GRID_W
    head_spec = lambda col0: pl.BlockSpec((1, s, C_HEAD_DIM), lambda h, bi: (bi, 0, col0 + h))
    return pl.pallas_call(
        functools.partial(_nb_attn_kernel, plan=_nb_slab_plan(rows), rows=rows),
        grid=(C_HEADS, b),
        in_specs=[head_spec(0), head_spec(C_HEADS), head_spec(2 * C_HEADS),
                  pl.BlockSpec((1,) + slabs.shape[1:], lambda h, bi: (h, 0, 0, 0))],
        out_specs=head_spec(0),
        out_shape=jax.ShapeDtypeStruct((b, s, C_HEADS * C_HEAD_DIM), BF16),
        compiler_params=_cparams(("parallel", "arbitrary")),
        name="neighborhood_attention",
    )(qkv, qkv, qkv, slabs)


def _nb_bias_slabs(rel_bias):
    c = np.arange(GRID_W)[:, None]
    kc = np.arange(GRID_W)[None, :]
    cs = np.clip(c - NB_W // 2, 0, GRID_W - NB_W)
    col_ok = (kc >= cs) & (kc < cs + NB_W)
    dc = np.clip(kc - c + NB_W - 1, 0, 2 * NB_W - 2)
    bias = jnp.where(col_ok[None, None], rel_bias.astype(F32)[:, :, dc] * LOG2E, NEG_BIG)
    masked = jnp.full_like(bias, NEG_BIG)
    return jnp.concatenate([jnp.concatenate([bias[:, :-1], bias[:, 1:]], axis=-1),
                            jnp.concatenate([bias, masked], axis=-1),
                            jnp.concatenate([masked, bias], axis=-1),
                            jnp.concatenate([masked[:, :1], masked[:, :1]], axis=-1)], axis=1)


def _rope_angles(seq, dim):
    pos = jnp.arange(seq, dtype=F32)
    inv = ROPE_THETA ** (-jnp.arange(0, dim, 2, dtype=F32) / dim)
    ang = pos[:, None] * inv[None, :]
    return jnp.cos(ang), jnp.sin(ang)


def _rope_adjacent(seq, dim):
    half = dim // 2
    c, sn = _rope_angles(seq, dim)
    cos = jnp.ones((seq, LANES), F32).at[:, :dim].set(jnp.concatenate([c, c], axis=1))
    up = jnp.zeros((seq, LANES), F32).at[:, half:dim].set(sn)
    down = jnp.zeros((seq, LANES), F32).at[:, :half].set(-sn)
    return half, (cos, up, down)


def _rope_swapped(seq, dim):
    half = dim // 2
    c, sn = _rope_angles(seq, dim)
    cos = jnp.ones((seq, LANES), F32).at[:, :half].set(c).at[:, ROPE_SWAP:ROPE_SWAP + half].set(c)
    sin = jnp.zeros((seq, LANES), F32).at[:, :half].set(-sn).at[:, ROPE_SWAP:ROPE_SWAP + half].set(sn)
    return ROPE_SWAP, (cos, sin)


def _pair_split(w, dim):
    half = dim // 2
    gap = ROPE_SWAP - half
    return jnp.concatenate([w[..., :half], w[..., dim:dim + gap], w[..., half:dim],
                            w[..., dim + gap:]], axis=-1)


def kernel(x, pre_mix_norm, post_mix_norm, pre_ffn_norm, post_ffn_norm, a_w_qkv, a_sinks, a_w_o,
           b_w_down, b_q_norm, b_w_uq, b_kv_norm, b_w_ukv, b_w_o, c_w_qkv, c_rel_bias, c_w_o,
           ffn_w_in, ffn_w_out):
    bsz, seq, d = x.shape
    m = bsz * seq
    rope_a = _rope_adjacent(seq, A_ROT_DIM)
    rope_b = _rope_swapped(seq, B_ROPE_DIM)
    tn = 1024
    cols = tn // LANES

    xf = x.reshape(m, d)
    for i in range(DEPTH):
        kind, slot = i % N_MIXERS, i // N_MIXERS
        if kind == 0:
            q_ops = ((True, A_HEAD_DIM ** -0.5 * LOG2E),) * cols
            kv_ops = ((True, 1.0),) * A_KV_HEADS + (PLAIN,) * A_KV_HEADS
            qkv = norm_proj(xf, pre_mix_norm[i], a_w_qkv[slot].astype(BF16), rope_a, seq=seq,
                            tm=1024, tn=tn, kinds=((0, 2, q_ops), (2, 3, kv_ops)))
            o = window_attention(qkv.reshape(bsz, seq, -1), a_sinks[slot].astype(F32), nblk=4)
            w_o = a_w_o[slot]
        elif kind == 1:
            wd = b_w_down[slot]
            wkr = jnp.pad(wd[:, B_Q_RANK + B_KV_RANK:], ((0, 0), (0, LANES - B_ROPE_DIM)))
            cq, ckv = mla_down(xf, pre_mix_norm[i], wd[:, :B_Q_RANK].astype(BF16),
                               wd[:, B_Q_RANK:B_Q_RANK + B_KV_RANK].astype(BF16),
                               _pair_split(wkr, B_ROPE_DIM).astype(BF16),
                               b_q_norm[slot], b_kv_norm[slot], rope_b, seq=seq, tm=512)
            wuq = b_w_uq[slot].reshape(B_Q_RANK, B_HEADS, B_NOPE_DIM + B_ROPE_DIM)
            wuq_r = jnp.pad(wuq[:, :, B_NOPE_DIM:], ((0, 0), (0, 0), (0, LANES - B_ROPE_DIM)))
            wuq = jnp.concatenate([wuq[:, :, :B_NOPE_DIM], _pair_split(wuq_r, B_ROPE_DIM)], axis=2)
            scale = (B_NOPE_DIM + B_ROPE_DIM) ** -0.5 * LOG2E
            q_ops = ((False, scale), (True, scale)) * (cols // 2)
            qcat = proj(cq, wuq.reshape(B_Q_RANK, -1).astype(BF16), rope_b, seq=seq,
                        tm=1024, tn=tn, kinds=((0, 1, q_ops),))
            wukv = b_w_ukv[slot].reshape(B_KV_RANK, B_HEADS, B_NOPE_DIM + B_V_DIM)
            eye = jnp.broadcast_to(jnp.eye(LANES, dtype=F32)[:, None, :], (LANES, B_HEADS, LANES))
            wk = jnp.concatenate(
                [jnp.pad(wukv[:, :, :B_NOPE_DIM], ((0, LANES), (0, 0), (0, 0))),
                 jnp.pad(eye, ((B_KV_RANK, 0), (0, 0), (0, 0)))], axis=2)
            wv = jnp.pad(wukv[:, :, B_NOPE_DIM:], ((0, LANES), (0, 0), (0, 0)))
            wkv = jnp.concatenate([wk.reshape(B_KV_RANK + LANES, -1),
                                   wv.reshape(B_KV_RANK + LANES, -1)], axis=1).astype(BF16)
            kvcat = proj(ckv, wkv, rope_b, seq=seq, tm=1024, tn=tn,
                         kinds=((0, 1, (PLAIN,) * cols),))
            o = mla_attention(qcat.reshape(bsz, seq, -1), kvcat.reshape(bsz, seq, -1), tq=2048,
                              chunk=256)
            w_o = b_w_o[slot]
        else:
            q_ops = ((False, C_HEAD_DIM ** -0.5 * LOG2E),) * cols
            nq = C_HEADS * C_HEAD_DIM // tn
            qkv = norm_proj(xf, pre_mix_norm[i], c_w_qkv[slot].astype(BF16), rope_a, seq=seq,
                            tm=1024, tn=tn, kinds=((0, nq, q_ops), (nq, 3 * nq, (PLAIN,) * cols)))
            o = neighborhood_attention(qkv.reshape(bsz, seq, -1), _nb_bias_slabs(c_rel_bias[slot]))
            w_o = c_w_o[slot]
        xf = out_proj_residual(o.reshape(m, -1), w_o.astype(BF16), post_mix_norm[i], xf, tm=512)
        xf = ffn(xf, pre_ffn_norm[i], ffn_w_in[i].astype(BF16), ffn_w_out[i].astype(BF16),
                 post_ffn_norm[i], tm=1024, tf=256)
    return xf.reshape(bsz, seq, d)
```

```python
import functools

import jax
import jax.numpy as jnp
import numpy as np
from jax import lax
from jax.experimental import pallas as pl
from jax.experimental.pallas import tpu as pltpu

D_MODEL = 2048
DEPTH = 4
N_MIXERS = 3
GRID_W = 64
ROPE_THETA = 500000.0
NORM_EPS = 1e-6
D_FF = 5632

A_HEADS = 16
A_KV_HEADS = 4
A_GROUP = A_HEADS // A_KV_HEADS
A_HEAD_DIM = 128
A_ROT_DIM = 32
WINDOW = 128
A_BLOCK = 128

B_HEADS = 16
B_NOPE_DIM = 128
B_ROPE_DIM = 64
B_V_DIM = 128
B_Q_RANK = 512
B_KV_RANK = 512
B_QK_PAD = 256

C_HEADS = 16
C_HEAD_DIM = 128
NB_H = 8
NB_W = 16
C_QROWS = 8
C_KROWS = 16

LANES = 128
PROJ_CHUNK = 512
ROPE_SWAP = LANES // 2
V7X_VMEM_BYTES = 64 * 1024 * 1024
V7X_VMEM_LIMIT_BYTES = V7X_VMEM_BYTES - 4 * 1024 * 1024

NEG_BIG = -1e30
LOG2E = float(np.log2(np.e))
BF16 = jnp.bfloat16
F32 = jnp.float32
PLAIN = (False, 1.0)


def _cparams(semantics):
    return pltpu.CompilerParams(dimension_semantics=semantics,
                                vmem_limit_bytes=V7X_VMEM_LIMIT_BYTES)


def _rms(xf, g):
    ms = jnp.mean(xf * xf, axis=-1, keepdims=True)
    return xf * lax.rsqrt(ms + NORM_EPS) * g


def _dot(a, b):
    return jnp.dot(a, b, preferred_element_type=F32)


def _dot_nt(a, b):
    return lax.dot_general(a, b, (((1,), (1,)), ((), ())), preferred_element_type=F32)


def _rope(y, tabs, shift):
    r = y * tabs[0][...] + pltpu.roll(y, shift, 1) * tabs[1][...]
    if shift != ROPE_SWAP:
        r = r + pltpu.roll(y, LANES - shift, 1) * tabs[2][...]
    return r


def _project_tile(lhs, w_ref, o_ref, tabs, shift, ops):
    per_chunk = PROJ_CHUNK // LANES
    for c0 in range(0, len(ops), per_chunk):
        chunk = slice(c0 * LANES, (c0 + per_chunk) * LANES)
        y = _dot(lhs, w_ref[:, chunk])
        if all(op == PLAIN for op in ops[c0:c0 + per_chunk]):
            o_ref[:, chunk] = y.astype(o_ref.dtype)
            continue
        for c in range(per_chunk):
            rope, scale = ops[c0 + c]
            yc = y[:, c * LANES:(c + 1) * LANES]
            if scale != 1.0:
                yc = yc * scale
            if rope:
                yc = _rope(yc, tabs, shift)
            o_ref[:, (c0 + c) * LANES:(c0 + c + 1) * LANES] = yc.astype(o_ref.dtype)


def _project(lhs, w_ref, o_ref, tabs, shift, kinds):
    if len(kinds) == 1:
        _project_tile(lhs, w_ref, o_ref, tabs, shift, kinds[0][2])
        return
    j = pl.program_id(1)
    for lo, hi, ops in kinds:
        @pl.when((j >= lo) & (j < hi))
        def _(ops=ops):
            _project_tile(lhs, w_ref, o_ref, tabs, shift, ops)


def _weight_spec(k, n, tn):
    if n == tn:
        return pl.BlockSpec((k, tn), lambda i, j: (0, 0), pipeline_mode=pl.Buffered(1))
    return pl.BlockSpec((k, tn), lambda i, j: (0, j))


def _norm_proj_kernel(x_ref, g_ref, w_ref, *rest, kinds, shift):
    *tabs, o_ref, h_ref = rest

    @pl.when(pl.program_id(1) == 0)
    def _():
        h_ref[...] = _rms(x_ref[...], g_ref[...]).astype(BF16)

    _project(h_ref[...], w_ref, o_ref, tabs, shift, kinds)


def norm_proj(x, g, w, rope, *, seq, tm, tn, kinds):
    m, d = x.shape
    n = w.shape[1]
    shift, tabs = rope
    sblk = seq // tm
    tab_spec = pl.BlockSpec((tm, LANES), lambda i, j: (i % sblk, 0))
    return pl.pallas_call(
        functools.partial(_norm_proj_kernel, kinds=kinds, shift=shift),
        grid=(m // tm, n // tn),
        in_specs=[pl.BlockSpec((tm, d), lambda i, j: (i, 0)),
                  pl.BlockSpec((1, d), lambda i, j: (0, 0)),
                  _weight_spec(d, n, tn)] + [tab_spec] * len(tabs),
        out_specs=pl.BlockSpec((tm, tn), lambda i, j: (i, j)),
        out_shape=jax.ShapeDtypeStruct((m, n), BF16),
        scratch_shapes=[pltpu.VMEM((tm, d), BF16)],
        compiler_params=_cparams(("parallel", "arbitrary")),
        name="norm_proj",
    )(x, g.reshape(1, d), w, *tabs)


def _proj_kernel(a_ref, w_ref, *rest, kinds, shift):
    *tabs, o_ref = rest
    _project(a_ref[...], w_ref, o_ref, tabs, shift, kinds)


def proj(a, w, rope, *, seq, tm, tn, kinds):
    m, k = a.shape
    n = w.shape[1]
    shift, tabs = rope
    sblk = seq // tm
    tab_spec = pl.BlockSpec((tm, LANES), lambda i, j: (i % sblk, 0))
    return pl.pallas_call(
        functools.partial(_proj_kernel, kinds=kinds, shift=shift),
        grid=(m // tm, n // tn),
        in_specs=[pl.BlockSpec((tm, k), lambda i, j: (i, 0)),
                  _weight_spec(k, n, tn)] + [tab_spec] * len(tabs),
        out_specs=pl.BlockSpec((tm, tn), lambda i, j: (i, j)),
        out_shape=jax.ShapeDtypeStruct((m, n), BF16),
        compiler_params=_cparams(("parallel", "arbitrary")),
        name="proj",
    )(a, w, *tabs)


def _out_proj_kernel(a_ref, w_ref, g_ref, x_ref, o_ref):
    m = _dot(a_ref[...], w_ref[...])
    o_ref[...] = x_ref[...] + _rms(m, g_ref[...])


def out_proj_residual(a, w, g, x, *, tm):
    m, k = a.shape
    d = w.shape[1]
    return pl.pallas_call(
        _out_proj_kernel,
        grid=(m // tm,),
        in_specs=[pl.BlockSpec((tm, k), lambda i: (i, 0)),
                  pl.BlockSpec((k, d), lambda i: (0, 0), pipeline_mode=pl.Buffered(1)),
                  pl.BlockSpec((1, d), lambda i: (0, 0)),
                  pl.BlockSpec((tm, d), lambda i: (i, 0))],
        out_specs=pl.BlockSpec((tm, d), lambda i: (i, 0)),
        out_shape=jax.ShapeDtypeStruct((m, d), F32),
        compiler_params=_cparams(("parallel",)),
        name="out_proj_residual",
    )(a, w, g.reshape(1, d), x)


def _ffn_kernel(x_ref, gpre_ref, wg_ref, wu_ref, wo_ref, gpost_ref, o_ref, h_ref):
    j = pl.program_id(1)

    @pl.when(j == 0)
    def _():
        h_ref[...] = _rms(x_ref[...], gpre_ref[...]).astype(BF16)
        o_ref[...] = jnp.zeros_like(o_ref)

    h = h_ref[...]
    gate = _dot(h, wg_ref[...])
    up = _dot(h, wu_ref[...])
    act = (gate / (1.0 + jnp.exp(-gate)) * up).astype(BF16)
    o_ref[...] += _dot(act, wo_ref[...])

    @pl.when(j == pl.num_programs(1) - 1)
    def _():
        o_ref[...] = x_ref[...] + _rms(o_ref[...], gpost_ref[...])


def ffn(x, gpre, w_in, w_out, gpost, *, tm, tf):
    m, d = x.shape
    dff = w_out.shape[0]
    nf = dff // tf
    return pl.pallas_call(
        _ffn_kernel,
        grid=(m // tm, nf),
        in_specs=[pl.BlockSpec((tm, d), lambda i, j: (i, 0)),
                  pl.BlockSpec((1, d), lambda i, j: (0, 0)),
                  pl.BlockSpec((d, tf), lambda i, j: (0, j)),
                  pl.BlockSpec((d, tf), lambda i, j: (0, nf + j)),
                  pl.BlockSpec((tf, d), lambda i, j: (j, 0)),
                  pl.BlockSpec((1, d), lambda i, j: (0, 0))],
        out_specs=pl.BlockSpec((tm, d), lambda i, j: (i, 0)),
        out_shape=jax.ShapeDtypeStruct((m, d), F32),
        scratch_shapes=[pltpu.VMEM((tm, d), BF16)],
        compiler_params=_cparams(("parallel", "arbitrary")),
        name="ffn",
    )(x, gpre.reshape(1, d), w_in, w_in, w_out, gpost.reshape(1, d))


def _win_attn_kernel(sink_ref, q_ref, k_ref, v_ref, o_ref, *, seq, nblk):
    kwin = 3 * A_BLOCK
    cols = A_GROUP * A_BLOCK
    for jb in range(nblk):
        blk = pl.program_id(1) * nblk + jb
        start = pl.multiple_of(jnp.clip((blk - 1) * A_BLOCK, 0, seq - kwin), A_BLOCK)
        kpos = start + lax.broadcasted_iota(jnp.int32, (kwin, cols), 0)
        qpos = blk * A_BLOCK + (lax.broadcasted_iota(jnp.int32, (kwin, cols), 1) & (A_BLOCK - 1))
        valid = jnp.abs(kpos - qpos) <= WINDOW
        qrows = slice(jb * A_BLOCK, (jb + 1) * A_BLOCK)
        for kv in range(A_KV_HEADS):
            heads = range(kv * A_GROUP, (kv + 1) * A_GROUP)
            kvcols = slice(kv * A_HEAD_DIM, (kv + 1) * A_HEAD_DIM)
            q4 = jnp.concatenate(
                [q_ref[0, qrows, h * A_HEAD_DIM:(h + 1) * A_HEAD_DIM] for h in heads], axis=0)
            kw = k_ref[0, pl.ds(start, kwin), kvcols]
            vw = v_ref[0, pl.ds(start, kwin), kvcols]
            st = jnp.where(valid, _dot_nt(kw, q4), NEG_BIG)
            sink = jnp.concatenate(
                [jnp.full((1, A_BLOCK), sink_ref[h] * LOG2E, F32) for h in heads], axis=1)
            mx = jnp.maximum(jnp.max(st, axis=0, keepdims=True), sink)
            pt = jnp.exp2(st - mx)
            denom = jnp.sum(pt, axis=0, keepdims=True) + jnp.exp2(sink - mx)
            ot = lax.dot_general(vw, pt.astype(BF16), (((0,), (0,)), ((), ())),
                                 preferred_element_type=F32) / denom
            for gi, h in enumerate(heads):
                o_ref[0, qrows, h * A_HEAD_DIM:(h + 1) * A_HEAD_DIM] = (
                    ot[:, gi * A_BLOCK:(gi + 1) * A_BLOCK].T.astype(o_ref.dtype))


def window_attention(qkv, sinks, *, nblk):
    b, s, _ = qkv.shape
    dq = A_HEADS * A_HEAD_DIM
    dkv = A_KV_HEADS * A_HEAD_DIM
    tq = nblk * A_BLOCK
    return pl.pallas_call(
        functools.partial(_win_attn_kernel, seq=s, nblk=nblk),
        grid=(b, s // tq),
        in_specs=[pl.BlockSpec(memory_space=pltpu.SMEM),
                  pl.BlockSpec((1, tq, dq), lambda bi, i: (bi, i, 0)),
                  pl.BlockSpec((1, s, dkv), lambda bi, i: (bi, 0, dq // dkv)),
                  pl.BlockSpec((1, s, dkv), lambda bi, i: (bi, 0, dq // dkv + 1))],
        out_specs=pl.BlockSpec((1, tq, dq), lambda bi, i: (bi, i, 0)),
        out_shape=jax.ShapeDtypeStruct((b, s, dq), BF16),
        compiler_params=_cparams(("parallel", "arbitrary")),
        name="window_attention",
    )(sinks, qkv, qkv, qkv)


def _mla_down_kernel(x_ref, g_ref, wq_ref, wkv_ref, wkr_ref, qn_ref, kvn_ref, *rest, shift):
    *tabs, cq_ref, ckv_ref = rest
    h = _rms(x_ref[...], g_ref[...]).astype(BF16)
    cq_ref[...] = _rms(_dot(h, wq_ref[...]), qn_ref[...]).astype(BF16)
    ckv_ref[:, :B_KV_RANK] = _rms(_dot(h, wkv_ref[...]), kvn_ref[...]).astype(BF16)
    ckv_ref[:, B_KV_RANK:] = _rope(_dot(h, wkr_ref[...]), tabs, shift).astype(BF16)


def mla_down(x, g, wq, wkv, wkr, qn, kvn, rope, *, seq, tm):
    m, d = x.shape
    shift, tabs = rope
    sblk = seq // tm
    tab_spec = pl.BlockSpec((tm, LANES), lambda i: (i % sblk, 0))
    full = lambda shape: pl.BlockSpec(shape, lambda i: (0, 0))
    return pl.pallas_call(
        functools.partial(_mla_down_kernel, shift=shift),
        grid=(m // tm,),
        in_specs=[pl.BlockSpec((tm, d), lambda i: (i, 0)), full((1, d)),
                  full((d, B_Q_RANK)), full((d, B_KV_RANK)), full((d, LANES)),
                  full((1, B_Q_RANK)), full((1, B_KV_RANK))] + [tab_spec] * len(tabs),
        out_specs=[pl.BlockSpec((tm, B_Q_RANK), lambda i: (i, 0)),
                   pl.BlockSpec((tm, B_KV_RANK + LANES), lambda i: (i, 0))],
        out_shape=[jax.ShapeDtypeStruct((m, B_Q_RANK), BF16),
                   jax.ShapeDtypeStruct((m, B_KV_RANK + LANES), BF16)],
        compiler_params=_cparams(("parallel",)),
        name="mla_down",
    )(x, g.reshape(1, d), wq, wkv, wkr, qn.reshape(1, -1), kvn.reshape(1, -1), *tabs)


def _mla_attn_kernel(q_ref, k_ref, v_ref, o_ref, *, chunk):
    v = v_ref[0]
    v_ones = jnp.concatenate([v, jnp.ones_like(v)], axis=1)
    dv = v.shape[1]
    for c in range(q_ref.shape[1] // chunk):
        rows = slice(c * chunk, (c + 1) * chunk)
        s = _dot_nt(q_ref[0, rows, :], k_ref[0])
        p = jnp.exp2(s - jnp.max(s, axis=-1, keepdims=True))
        o = _dot(p.astype(BF16), v_ones)
        o_ref[0, rows, :] = (o[:, :dv] / o[:, dv:]).astype(o_ref.dtype)


def mla_attention(qcat, kvcat, *, tq, chunk):
    b, s, _ = qcat.shape
    vblk0 = B_HEADS * B_QK_PAD // B_V_DIM
    return pl.pallas_call(
        functools.partial(_mla_attn_kernel, chunk=chunk),
        grid=(b, B_HEADS, s // tq),
        in_specs=[pl.BlockSpec((1, tq, B_QK_PAD), lambda bi, h, qi: (bi, qi, h)),
                  pl.BlockSpec((1, s, B_QK_PAD), lambda bi, h, qi: (bi, 0, h)),
                  pl.BlockSpec((1, s, B_V_DIM), lambda bi, h, qi: (bi, 0, vblk0 + h))],
        out_specs=pl.BlockSpec((1, tq, B_V_DIM), lambda bi, h, qi: (bi, qi, h)),
        out_shape=jax.ShapeDtypeStruct((b, s, B_HEADS * B_V_DIM), BF16),
        compiler_params=_cparams(("parallel", "parallel", "arbitrary")),
        name="mla_attention",
    )(qcat, kvcat, kvcat)


N_DR = 2 * NB_H - 1
SLAB_LOW = N_DR - 1
SLAB_HIGH = SLAB_LOW + N_DR
SLAB_NONE = SLAB_HIGH + N_DR


def _nb_slab_plan(rows):
    plan = []
    for rb in range(rows // C_QROWS):
        w0 = min(max(rb * C_QROWS - NB_H // 2, 0), rows - C_KROWS)
        block = []
        for i in range(C_QROWS):
            r = rb * C_QROWS + i
            rs = min(max(r - NB_H // 2, 0), rows - NB_H)
            ids = []
            for pair in range(C_KROWS // 2):
                kr = w0 + 2 * pair
                ok0, ok1 = rs <= kr < rs + NB_H, rs <= kr + 1 < rs + NB_H
                dr = kr - r + NB_H - 1
                ids.append(dr if ok0 and ok1 else SLAB_LOW + dr if ok0
                           else SLAB_HIGH + dr + 1 if ok1 else SLAB_NONE)
            block.append(tuple(ids))
        plan.append(tuple(block))
    return tuple(plan)


def _nb_attn_kernel(q_ref, k_ref, v_ref, t_ref, o_ref, *, plan, rows):
    tq = C_QROWS * GRID_W
    nk = C_KROWS * GRID_W
    biases = {}
    for rb, block in enumerate(plan):
        if block not in biases:
            biases[block] = jnp.concatenate(
                [jnp.concatenate([t_ref[0, sid] for sid in ids], axis=1) for ids in block], axis=0)
        start = min(max(rb * C_QROWS - NB_H // 2, 0), rows - C_KROWS) * GRID_W
        qrows = slice(rb * tq, (rb + 1) * tq)
        kw = k_ref[0, start:start + nk, :]
        vw = v_ref[0, start:start + nk, :]
        s = _dot_nt(q_ref[0, qrows, :], kw) + biases[block]
        p = jnp.exp2(s - jnp.max(s, axis=-1, keepdims=True))
        o = _dot(p.astype(BF16), jnp.concatenate([vw, jnp.ones_like(vw)], axis=1))
        o_ref[0, qrows, :] = (o[:, :C_HEAD_DIM] / o[:, C_HEAD_DIM:]).astype(o_ref.dtype)


def neighborhood_attention(qkv, slabs):
    b, s, _ = qkv.shape
    rows = s // GRID_W
    head_spec = lambda col0: pl.BlockSpec((1, s, C_HEAD_DIM), lambda h, bi: (bi, 0, col0 + h))
    return pl.pallas_call(
        functools.partial(_nb_attn_kernel, plan=_nb_slab_plan(rows), rows=rows),
        grid=(C_HEADS, b),
        in_specs=[head_spec(0), head_spec(C_HEADS), head_spec(2 * C_HEADS),
                  pl.BlockSpec((1,) + slabs.shape[1:], lambda h, bi: (h, 0, 0, 0))],
        out_specs=head_spec(0),
        out_shape=jax.ShapeDtypeStruct((b, s, C_HEADS * C_HEAD_DIM), BF16),
        compiler_params=_cparams(("parallel", "arbitrary")),
        name="neighborhood_attention",
    )(qkv, qkv, qkv, slabs)


def _nb_bias_slabs(rel_bias):
    c = np.arange(GRID_W)[:, None]
    kc = np.arange(GRID_W)[None, :]
    cs = np.clip(c - NB_W // 2, 0, GRID_W - NB_W)
    col_ok = (kc >= cs) & (kc < cs + NB_W)
    dc = np.clip(kc - c + NB_W - 1, 0, 2 * NB_W - 2)
    bias = jnp.where(col_ok[None, None], rel_bias.astype(F32)[:, :, dc] * LOG2E, NEG_BIG)
    masked = jnp.full_like(bias, NEG_BIG)
    return jnp.concatenate([jnp.concatenate([bias[:, :-1], bias[:, 1:]], axis=-1),
                            jnp.concatenate([bias, masked], axis=-1),
                            jnp.concatenate([masked, bias], axis=-1),
                            jnp.concatenate([masked[:, :1], masked[:, :1]], axis=-1)], axis=1)


def _rope_angles(seq, dim):
    pos = jnp.arange(seq, dtype=F32)
    inv = ROPE_THETA ** (-jnp.arange(0, dim, 2, dtype=F32) / dim)
    ang = pos[:, None] * inv[None, :]
    return jnp.cos(ang), jnp.sin(ang)


def _rope_adjacent(seq, dim):
    half = dim // 2
    c, sn = _rope_angles(seq, dim)
    cos = jnp.ones((seq, LANES), F32).at[:, :dim].set(jnp.concatenate([c, c], axis=1))
    up = jnp.zeros((seq, LANES), F32).at[:, half:dim].set(sn)
    down = jnp.zeros((seq, LANES), F32).at[:, :half].set(-sn)
    return half, (cos, up, down)


def _rope_swapped(seq, dim):
    half = dim // 2
    c, sn = _rope_angles(seq, dim)
    cos = jnp.ones((seq, LANES), F32).at[:, :half].set(c).at[:, ROPE_SWAP:ROPE_SWAP + half].set(c)
    sin = jnp.zeros((seq, LANES), F32).at[:, :half].set(-sn).at[:, ROPE_SWAP:ROPE_SWAP + half].set(sn)
    return ROPE_SWAP, (cos, sin)


def _pair_split(w, dim):
    half = dim // 2
    gap = ROPE_SWAP - half
    return jnp.concatenate([w[..., :half], w[..., dim:dim + gap], w[..., half:dim],
                            w[..., dim + gap:]], axis=-1)


def kernel(x, pre_mix_norm, post_mix_norm, pre_ffn_norm, post_ffn_norm, a_w_qkv, a_sinks, a_w_o,
           b_w_down, b_q_norm, b_w_uq, b_kv_norm, b_w_ukv, b_w_o, c_w_qkv, c_rel_bias, c_w_o,
           ffn_w_in, ffn_w_out):
    bsz, seq, d = x.shape
    m = bsz * seq
    rope_a = _rope_adjacent(seq, A_ROT_DIM)
    rope_b = _rope_swapped(seq, B_ROPE_DIM)
    xf = x.reshape(m, d)
    for i in range(DEPTH):
        kind, slot = i % N_MIXERS, i // N_MIXERS
        if kind == 0:
            ops = (((True, A_HEAD_DIM ** -0.5 * LOG2E),) * A_HEADS + ((True, 1.0),) * A_KV_HEADS
                   + (PLAIN,) * A_KV_HEADS)
            qkv = norm_proj(xf, pre_mix_norm[i], a_w_qkv[slot].astype(BF16), rope_a, seq=seq,
                            tm=1024, tn=len(ops) * LANES, kinds=((0, 1, ops),))
            o = window_attention(qkv.reshape(bsz, seq, -1), a_sinks[slot].astype(F32), nblk=4)
            w_o = a_w_o[slot]
        elif kind == 1:
            wd = b_w_down[slot]
            wkr = jnp.pad(wd[:, B_Q_RANK + B_KV_RANK:], ((0, 0), (0, LANES - B_ROPE_DIM)))
            cq, ckv = mla_down(xf, pre_mix_norm[i], wd[:, :B_Q_RANK].astype(BF16),
                               wd[:, B_Q_RANK:B_Q_RANK + B_KV_RANK].astype(BF16),
                               _pair_split(wkr, B_ROPE_DIM).astype(BF16),
                               b_q_norm[slot], b_kv_norm[slot], rope_b, seq=seq, tm=512)
            tn = 2048
            wuq = b_w_uq[slot].reshape(B_Q_RANK, B_HEADS, B_NOPE_DIM + B_ROPE_DIM)
            wuq_r = jnp.pad(wuq[:, :, B_NOPE_DIM:], ((0, 0), (0, 0), (0, LANES - B_ROPE_DIM)))
            wuq = jnp.concatenate([wuq[:, :, :B_NOPE_DIM], _pair_split(wuq_r, B_ROPE_DIM)], axis=2)
            scale = (B_NOPE_DIM + B_ROPE_DIM) ** -0.5 * LOG2E
            q_ops = ((False, scale), (True, scale)) * (tn // B_QK_PAD)
            qcat = proj(cq, wuq.reshape(B_Q_RANK, -1).astype(BF16), rope_b, seq=seq,
                        tm=1024, tn=tn, kinds=((0, 1, q_ops),))
            wukv = b_w_ukv[slot].reshape(B_KV_RANK, B_HEADS, B_NOPE_DIM + B_V_DIM)
            eye = jnp.broadcast_to(jnp.eye(LANES, dtype=F32)[:, None, :], (LANES, B_HEADS, LANES))
            wk = jnp.concatenate(
                [jnp.pad(wukv[:, :, :B_NOPE_DIM], ((0, LANES), (0, 0), (0, 0))),
                 jnp.pad(eye, ((B_KV_RANK, 0), (0, 0), (0, 0)))], axis=2)
            wv = jnp.pad(wukv[:, :, B_NOPE_DIM:], ((0, LANES), (0, 0), (0, 0)))
            wkv = jnp.concatenate([wk.reshape(B_KV_RANK + LANES, -1),
                                   wv.reshape(B_KV_RANK + LANES, -1)], axis=1).astype(BF16)
            kvcat = proj(ckv, wkv, rope_b, seq=seq, tm=1024, tn=tn,
                         kinds=((0, 1, (PLAIN,) * (tn // LANES)),))
            o = mla_attention(qcat.reshape(bsz, seq, -1), kvcat.reshape(bsz, seq, -1), tq=2048,
                              chunk=256)
            w_o = b_w_o[slot]
        else:
            tn = C_HEADS * C_HEAD_DIM
            q_ops = ((False, C_HEAD_DIM ** -0.5 * LOG2E),) * C_HEADS
            qkv = norm_proj(xf, pre_mix_norm[i], c_w_qkv[slot].astype(BF16), rope_a, seq=seq,
                            tm=1024, tn=tn, kinds=((0, 1, q_ops), (1, 3, (PLAIN,) * C_HEADS)))
            o = neighborhood_attention(qkv.reshape(bsz, seq, -1), _nb_bias_slabs(c_rel_bias[slot]))
            w_o = c_w_o[slot]
        xf = out_proj_residual(o.reshape(m, -1), w_o.astype(BF16), post_mix_norm[i], xf, tm=512)
        xf = ffn(xf, pre_ffn_norm[i], ffn_w_in[i].astype(BF16), ffn_w_out[i].astype(BF16),
                 post_ffn_norm[i], tm=1024, tf=256)
    return xf.reshape(bsz, seq, d)
```

```python
import functools

import jax
import jax.numpy as jnp
import numpy as np
from jax import lax
from jax.experimental import pallas as pl
from jax.experimental.pallas import tpu as pltpu

D_MODEL = 2048
DEPTH = 4
N_MIXERS = 3
GRID_W = 64
ROPE_THETA = 500000.0
NORM_EPS = 1e-6
D_FF = 5632

A_HEADS = 16
A_KV_HEADS = 4
A_GROUP = A_HEADS // A_KV_HEADS
A_HEAD_DIM = 128
A_ROT_DIM = 32
WINDOW = 128
A_BLOCK = 128

B_HEADS = 16
B_NOPE_DIM = 128
B_ROPE_DIM = 64
B_V_DIM = 128
B_Q_RANK = 512
B_KV_RANK = 512
B_QK_PAD = 256

C_HEADS = 16
C_HEAD_DIM = 128
NB_H = 8
NB_W = 16
C_QROWS = 8
C_KROWS = 16

LANES = 128
PROJ_CHUNK = 512
ROPE_SWAP = LANES // 2
V7X_VMEM_BYTES = 64 * 1024 * 1024
V7X_VMEM_LIMIT_BYTES = V7X_VMEM_BYTES - 4 * 1024 * 1024

NEG_BIG = -1e30
LOG2E = float(np.log2(np.e))
BF16 = jnp.bfloat16
F32 = jnp.float32
PLAIN = (False, 1.0)


def _cparams(semantics):
    return pltpu.CompilerParams(dimension_semantics=semantics,
                                vmem_limit_bytes=V7X_VMEM_LIMIT_BYTES)


def _rms(xf, g):
    ms = jnp.mean(xf * xf, axis=-1, keepdims=True)
    return xf * lax.rsqrt(ms + NORM_EPS) * g


def _inv_rms(src_ref, stat_ref):
    xf = src_ref[...]
    stat_ref[...] = lax.rsqrt(jnp.mean(xf * xf, axis=-1, keepdims=True) + NORM_EPS)
    return stat_ref[...]


def _dot(a, b):
    return jnp.dot(a, b, preferred_element_type=F32)


def _dot_nt(a, b):
    return lax.dot_general(a, b, (((1,), (1,)), ((), ())), preferred_element_type=F32)


def _rope(y, tabs, shift):
    r = y * tabs[0][...] + pltpu.roll(y, shift, 1) * tabs[1][...]
    if shift != ROPE_SWAP:
        r = r + pltpu.roll(y, LANES - shift, 1) * tabs[2][...]
    return r


def _project_tile(lhs, w_ref, o_ref, tabs, shift, ops):
    per_chunk = PROJ_CHUNK // LANES
    for c0 in range(0, len(ops), per_chunk):
        chunk = slice(c0 * LANES, (c0 + per_chunk) * LANES)
        y = _dot(lhs, w_ref[:, chunk])
        if all(op == PLAIN for op in ops[c0:c0 + per_chunk]):
            o_ref[:, chunk] = y.astype(o_ref.dtype)
            continue
        for c in range(per_chunk):
            rope, scale = ops[c0 + c]
            yc = y[:, c * LANES:(c + 1) * LANES]
            if scale != 1.0:
                yc = yc * scale
            if rope:
                yc = _rope(yc, tabs, shift)
            o_ref[:, (c0 + c) * LANES:(c0 + c + 1) * LANES] = yc.astype(o_ref.dtype)


def _project(lhs, w_ref, o_ref, tabs, shift, kinds):
    if len(kinds) == 1:
        _project_tile(lhs, w_ref, o_ref, tabs, shift, kinds[0][2])
        return
    j = pl.program_id(1)
    for lo, hi, ops in kinds:
        @pl.when((j >= lo) & (j < hi))
        def _(ops=ops):
            _project_tile(lhs, w_ref, o_ref, tabs, shift, ops)


def _weight_spec(k, n, tn):
    if n == tn:
        return pl.BlockSpec((k, tn), lambda i, j: (0, 0), pipeline_mode=pl.Buffered(1))
    return pl.BlockSpec((k, tn), lambda i, j: (0, j))


def _norm_proj_kernel(x_ref, g_ref, w_ref, *rest, kinds, shift):
    *tabs, o_ref, h_ref = rest

    @pl.when(pl.program_id(1) == 0)
    def _():
        h_ref[...] = _rms(x_ref[...], g_ref[...]).astype(BF16)

    _project(h_ref[...], w_ref, o_ref, tabs, shift, kinds)


def norm_proj(x, g, w, rope, *, seq, tm, tn, kinds):
    m, d = x.shape
    n = w.shape[1]
    shift, tabs = rope
    sblk = seq // tm
    tab_spec = pl.BlockSpec((tm, LANES), lambda i, j: (i % sblk, 0))
    return pl.pallas_call(
        functools.partial(_norm_proj_kernel, kinds=kinds, shift=shift),
        grid=(m // tm, n // tn),
        in_specs=[pl.BlockSpec((tm, d), lambda i, j: (i, 0)),
                  pl.BlockSpec((1, d), lambda i, j: (0, 0)),
                  _weight_spec(d, n, tn)] + [tab_spec] * len(tabs),
        out_specs=pl.BlockSpec((tm, tn), lambda i, j: (i, j)),
        out_shape=jax.ShapeDtypeStruct((m, n), BF16),
        scratch_shapes=[pltpu.VMEM((tm, d), BF16)],
        compiler_params=_cparams(("parallel", "arbitrary")),
        name="norm_proj",
    )(x, g.reshape(1, d), w, *tabs)


def _proj_kernel(a_ref, w_ref, *rest, kinds, shift):
    *tabs, o_ref = rest
    _project(a_ref[...], w_ref, o_ref, tabs, shift, kinds)


def proj(a, w, rope, *, seq, tm, tn, kinds):
    m, k = a.shape
    n = w.shape[1]
    shift, tabs = rope
    sblk = seq // tm
    tab_spec = pl.BlockSpec((tm, LANES), lambda i, j: (i % sblk, 0))
    return pl.pallas_call(
        functools.partial(_proj_kernel, kinds=kinds, shift=shift),
        grid=(m // tm, n // tn),
        in_specs=[pl.BlockSpec((tm, k), lambda i, j: (i, 0)),
                  _weight_spec(k, n, tn)] + [tab_spec] * len(tabs),
        out_specs=pl.BlockSpec((tm, tn), lambda i, j: (i, j)),
        out_shape=jax.ShapeDtypeStruct((m, n), BF16),
        compiler_params=_cparams(("parallel", "arbitrary")),
        name="proj",
    )(a, w, *tabs)


def _out_proj_kernel(a_ref, w_ref, g_ref, x_ref, o_ref):
    m = _dot(a_ref[...], w_ref[...])
    o_ref[...] = x_ref[...] + _rms(m, g_ref[...])


def out_proj_residual(a, w, g, x, *, tm):
    m, k = a.shape
    d = w.shape[1]
    return pl.pallas_call(
        _out_proj_kernel,
        grid=(m // tm,),
        in_specs=[pl.BlockSpec((tm, k), lambda i: (i, 0)),
                  pl.BlockSpec((k, d), lambda i: (0, 0), pipeline_mode=pl.Buffered(1)),
                  pl.BlockSpec((1, d), lambda i: (0, 0)),
                  pl.BlockSpec((tm, d), lambda i: (i, 0))],
        out_specs=pl.BlockSpec((tm, d), lambda i: (i, 0)),
        out_shape=jax.ShapeDtypeStruct((m, d), F32),
        compiler_params=_cparams(("parallel",)),
        name="out_proj_residual",
    )(a, w, g.reshape(1, d), x)


def _ffn_kernel(x_ref, gpre_ref, wg_ref, wu_ref, wo_ref, gpost_ref, o_ref, h_ref, stat_ref):
    j = pl.program_id(1)

    @pl.when(j == 0)
    def _():
        h_ref[...] = _rms(x_ref[...], gpre_ref[...]).astype(BF16)
        o_ref[...] = jnp.zeros_like(o_ref)

    h = h_ref[...]
    gate = _dot(h, wg_ref[...])
    up = _dot(h, wu_ref[...])
    act = (gate / (1.0 + jnp.exp(-gate)) * up).astype(BF16)
    o_ref[...] += _dot(act, wo_ref[...])

    @pl.when(j == pl.num_programs(1) - 1)
    def _():
        o_ref[...] = x_ref[...] + o_ref[...] * _inv_rms(o_ref, stat_ref) * gpost_ref[...]


def ffn(x, gpre, w_in, w_out, gpost, *, tm, tf):
    m, d = x.shape
    dff = w_out.shape[0]
    nf = dff // tf
    return pl.pallas_call(
        _ffn_kernel,
        grid=(m // tm, nf),
        in_specs=[pl.BlockSpec((tm, d), lambda i, j: (i, 0), pipeline_mode=pl.Buffered(1)),
                  pl.BlockSpec((1, d), lambda i, j: (0, 0)),
                  pl.BlockSpec((d, tf), lambda i, j: (0, j)),
                  pl.BlockSpec((d, tf), lambda i, j: (0, nf + j)),
                  pl.BlockSpec((tf, d), lambda i, j: (j, 0)),
                  pl.BlockSpec((1, d), lambda i, j: (0, 0))],
        out_specs=pl.BlockSpec((tm, d), lambda i, j: (i, 0)),
        out_shape=jax.ShapeDtypeStruct((m, d), F32),
        scratch_shapes=[pltpu.VMEM((tm, d), BF16), pltpu.VMEM((tm, 1), F32)],
        compiler_params=_cparams(("parallel", "arbitrary")),
        name="ffn",
    )(x, gpre.reshape(1, d), w_in, w_in, w_out, gpost.reshape(1, d))


def _win_attn_kernel(sink_ref, q_ref, k_ref, v_ref, o_ref, *, seq, nblk):
    kwin = 3 * A_BLOCK
    cols = A_GROUP * A_BLOCK
    for jb in range(nblk):
        blk = pl.program_id(1) * nblk + jb
        start = pl.multiple_of(jnp.clip((blk - 1) * A_BLOCK, 0, seq - kwin), A_BLOCK)
        kpos = start + lax.broadcasted_iota(jnp.int32, (kwin, cols), 0)
        qpos = blk * A_BLOCK + (lax.broadcasted_iota(jnp.int32, (kwin, cols), 1) & (A_BLOCK - 1))
        valid = jnp.abs(kpos - qpos) <= WINDOW
        qrows = slice(jb * A_BLOCK, (jb + 1) * A_BLOCK)
        for kv in range(A_KV_HEADS):
            heads = range(kv * A_GROUP, (kv + 1) * A_GROUP)
            kvcols = slice(kv * A_HEAD_DIM, (kv + 1) * A_HEAD_DIM)
            q4 = jnp.concatenate(
                [q_ref[0, qrows, h * A_HEAD_DIM:(h + 1) * A_HEAD_DIM] for h in heads], axis=0)
            kw = k_ref[0, pl.ds(start, kwin), kvcols]
            vw = v_ref[0, pl.ds(start, kwin), kvcols]
            st = jnp.where(valid, _dot_nt(kw, q4), NEG_BIG)
            sink = jnp.concatenate(
                [jnp.full((1, A_BLOCK), sink_ref[h] * LOG2E, F32) for h in heads], axis=1)
            mx = jnp.maximum(jnp.max(st, axis=0, keepdims=True), sink)
            pt = jnp.exp2(st - mx)
            denom = jnp.sum(pt, axis=0, keepdims=True) + jnp.exp2(sink - mx)
            ot = lax.dot_general(vw, pt.astype(BF16), (((0,), (0,)), ((), ())),
                                 preferred_element_type=F32) / denom
            for gi, h in enumerate(heads):
                o_ref[0, qrows, h * A_HEAD_DIM:(h + 1) * A_HEAD_DIM] = (
                    ot[:, gi * A_BLOCK:(gi + 1) * A_BLOCK].T.astype(o_ref.dtype))


def window_attention(qkv, sinks, *, nblk):
    b, s, _ = qkv.shape
    dq = A_HEADS * A_HEAD_DIM
    dkv = A_KV_HEADS * A_HEAD_DIM
    tq = nblk * A_BLOCK
    return pl.pallas_call(
        functools.partial(_win_attn_kernel, seq=s, nblk=nblk),
        grid=(b, s // tq),
        in_specs=[pl.BlockSpec(memory_space=pltpu.SMEM),
                  pl.BlockSpec((1, tq, dq), lambda bi, i: (bi, i, 0)),
                  pl.BlockSpec((1, s, dkv), lambda bi, i: (bi, 0, dq // dkv)),
                  pl.BlockSpec((1, s, dkv), lambda bi, i: (bi, 0, dq // dkv + 1))],
        out_specs=pl.BlockSpec((1, tq, dq), lambda bi, i: (bi, i, 0)),
        out_shape=jax.ShapeDtypeStruct((b, s, dq), BF16),
        compiler_params=_cparams(("parallel", "arbitrary")),
        name="window_attention",
    )(sinks, qkv, qkv, qkv)


def _mla_down_kernel(x_ref, g_ref, wq_ref, wkv_ref, wkr_ref, qn_ref, kvn_ref, *rest, shift):
    *tabs, cq_ref, ckv_ref = rest
    h = _rms(x_ref[...], g_ref[...]).astype(BF16)
    cq_ref[...] = _rms(_dot(h, wq_ref[...]), qn_ref[...]).astype(BF16)
    ckv_ref[:, :B_KV_RANK] = _rms(_dot(h, wkv_ref[...]), kvn_ref[...]).astype(BF16)
    ckv_ref[:, B_KV_RANK:] = _rope(_dot(h, wkr_ref[...]), tabs, shift).astype(BF16)


def mla_down(x, g, wq, wkv, wkr, qn, kvn, rope, *, seq, tm):
    m, d = x.shape
    shift, tabs = rope
    sblk = seq // tm
    tab_spec = pl.BlockSpec((tm, LANES), lambda i: (i % sblk, 0))
    full = lambda shape: pl.BlockSpec(shape, lambda i: (0, 0))
    return pl.pallas_call(
        functools.partial(_mla_down_kernel, shift=shift),
        grid=(m // tm,),
        in_specs=[pl.BlockSpec((tm, d), lambda i: (i, 0)), full((1, d)),
                  full((d, B_Q_RANK)), full((d, B_KV_RANK)), full((d, LANES)),
                  full((1, B_Q_RANK)), full((1, B_KV_RANK))] + [tab_spec] * len(tabs),
        out_specs=[pl.BlockSpec((tm, B_Q_RANK), lambda i: (i, 0)),
                   pl.BlockSpec((tm, B_KV_RANK + LANES), lambda i: (i, 0))],
        out_shape=[jax.ShapeDtypeStruct((m, B_Q_RANK), BF16),
                   jax.ShapeDtypeStruct((m, B_KV_RANK + LANES), BF16)],
        compiler_params=_cparams(("parallel",)),
        name="mla_down",
    )(x, g.reshape(1, d), wq, wkv, wkr, qn.reshape(1, -1), kvn.reshape(1, -1), *tabs)


def _mla_attn_kernel(q_ref, k_ref, v_ref, o_ref, *, chunk):
    v = v_ref[0]
    v_ones = jnp.concatenate([v, jnp.ones_like(v)], axis=1)
    dv = v.shape[1]
    for c in range(q_ref.shape[1] // chunk):
        rows = slice(c * chunk, (c + 1) * chunk)
        s = _dot_nt(q_ref[0, rows, :], k_ref[0])
        p = jnp.exp2(s - jnp.max(s, axis=-1, keepdims=True))
        o = _dot(p.astype(BF16), v_ones)
        o_ref[0, rows, :] = (o[:, :dv] / o[:, dv:]).astype(o_ref.dtype)


def mla_attention(qcat, kvcat, *, tq, chunk):
    b, s, _ = qcat.shape
    vblk0 = B_HEADS * B_QK_PAD // B_V_DIM
    return pl.pallas_call(
        functools.partial(_mla_attn_kernel, chunk=chunk),
        grid=(b, B_HEADS, s // tq),
        in_specs=[pl.BlockSpec((1, tq, B_QK_PAD), lambda bi, h, qi: (bi, qi, h)),
                  pl.BlockSpec((1, s, B_QK_PAD), lambda bi, h, qi: (bi, 0, h)),
                  pl.BlockSpec((1, s, B_V_DIM), lambda bi, h, qi: (bi, 0, vblk0 + h))],
        out_specs=pl.BlockSpec((1, tq, B_V_DIM), lambda bi, h, qi: (bi, qi, h)),
        out_shape=jax.ShapeDtypeStruct((b, s, B_HEADS * B_V_DIM), BF16),
        compiler_params=_cparams(("parallel", "parallel", "arbitrary")),
        name="mla_attention",
    )(qcat, kvcat, kvcat)


N_DR = 2 * NB_H - 1
SLAB_LOW = N_DR - 1
SLAB_HIGH = SLAB_LOW + N_DR
SLAB_NONE = SLAB_HIGH + N_DR


def _nb_slab_plan(rows):
    plan = []
    for rb in range(rows // C_QROWS):
        w0 = min(max(rb * C_QROWS - NB_H // 2, 0), rows - C_KROWS)
        block = []
        for i in range(C_QROWS):
            r = rb * C_QROWS + i
            rs = min(max(r - NB_H // 2, 0), rows - NB_H)
            ids = []
            for pair in range(C_KROWS // 2):
                kr = w0 + 2 * pair
                ok0, ok1 = rs <= kr < rs + NB_H, rs <= kr + 1 < rs + NB_H
                dr = kr - r + NB_H - 1
                ids.append(dr if ok0 and ok1 else SLAB_LOW + dr if ok0
                           else SLAB_HIGH + dr + 1 if ok1 else SLAB_NONE)
            block.append(tuple(ids))
        plan.append(tuple(block))
    return tuple(plan)


def _nb_attn_kernel(q_ref, k_ref, v_ref, t_ref, o_ref, *, plan, rows):
    tq = C_QROWS * GRID_W
    nk = C_KROWS * GRID_W
    biases = {}
    for rb, block in enumerate(plan):
        if block not in biases:
            biases[block] = jnp.concatenate(
                [jnp.concatenate([t_ref[0, sid] for sid in ids], axis=1) for ids in block], axis=0)
        start = min(max(rb * C_QROWS - NB_H // 2, 0), rows - C_KROWS) * GRID_W
        qrows = slice(rb * tq, (rb + 1) * tq)
        kw = k_ref[0, start:start + nk, :]
        vw = v_ref[0, start:start + nk, :]
        s = _dot_nt(q_ref[0, qrows, :], kw) + biases[block]
        p = jnp.exp2(s - jnp.max(s, axis=-1, keepdims=True))
        o = _dot(p.astype(BF16), jnp.concatenate([vw, jnp.ones_like(vw)], axis=1))
        o_ref[0, qrows, :] = (o[:, :C_HEAD_DIM] / o[:, C_HEAD_DIM:]).astype(o_ref.dtype)


def neighborhood_attention(qkv, slabs):
    b, s, _ = qkv.shape
    rows = s // GRID_W
    head_spec = lambda col0: pl.BlockSpec((1, s, C_HEAD_DIM), lambda h, bi: (bi, 0, col0 + h))
    return pl.pallas_call(
        functools.partial(_nb_attn_kernel, plan=_nb_slab_plan(rows), rows=rows),
        grid=(C_HEADS, b),
        in_specs=[head_spec(0), head_spec(C_HEADS), head_spec(2 * C_HEADS),
                  pl.BlockSpec((1,) + slabs.shape[1:], lambda h, bi: (h, 0, 0, 0))],
        out_specs=head_spec(0),
        out_shape=jax.ShapeDtypeStruct((b, s, C_HEADS * C_HEAD_DIM), BF16),
        compiler_params=_cparams(("parallel", "arbitrary")),
        name="neighborhood_attention",
    )(qkv, qkv, qkv, slabs)


def _nb_bias_slabs(rel_bias):
    c = np.arange(GRID_W)[:, None]
    kc = np.arange(GRID_W)[None, :]
    cs = np.clip(c - NB_W // 2, 0, GRID_W - NB_W)
    col_ok = (kc >= cs) & (kc < cs + NB_W)
    dc = np.clip(kc - c + NB_W - 1, 0, 2 * NB_W - 2)
    bias = jnp.where(col_ok[None, None], rel_bias.astype(F32)[:, :, dc] * LOG2E, NEG_BIG)
    masked = jnp.full_like(bias, NEG_BIG)
    return jnp.concatenate([jnp.concatenate([bias[:, :-1], bias[:, 1:]], axis=-1),
                            jnp.concatenate([bias, masked], axis=-1),
                            jnp.concatenate([masked, bias], axis=-1),
                            jnp.concatenate([masked[:, :1], masked[:, :1]], axis=-1)], axis=1)


def _rope_angles(seq, dim):
    pos = jnp.arange(seq, dtype=F32)
    inv = ROPE_THETA ** (-jnp.arange(0, dim, 2, dtype=F32) / dim)
    ang = pos[:, None] * inv[None, :]
    return jnp.cos(ang), jnp.sin(ang)


def _rope_adjacent(seq, dim):
    half = dim // 2
    c, sn = _rope_angles(seq, dim)
    cos = jnp.ones((seq, LANES), F32).at[:, :dim].set(jnp.concatenate([c, c], axis=1))
    up = jnp.zeros((seq, LANES), F32).at[:, half:dim].set(sn)
    down = jnp.zeros((seq, LANES), F32).at[:, :half].set(-sn)
    return half, (cos, up, down)


def _rope_swapped(seq, dim):
    half = dim // 2
    c, sn = _rope_angles(seq, dim)
    cos = jnp.ones((seq, LANES), F32).at[:, :half].set(c).at[:, ROPE_SWAP:ROPE_SWAP + half].set(c)
    sin = jnp.zeros((seq, LANES), F32).at[:, :half].set(-sn).at[:, ROPE_SWAP:ROPE_SWAP + half].set(sn)
    return ROPE_SWAP, (cos, sin)


def _pair_split(w, dim):
    half = dim // 2
    gap = ROPE_SWAP - half
    return jnp.concatenate([w[..., :half], w[..., dim:dim + gap], w[..., half:dim],
                            w[..., dim + gap:]], axis=-1)


def kernel(x, pre_mix_norm, post_mix_norm, pre_ffn_norm, post_ffn_norm, a_w_qkv, a_sinks, a_w_o,
           b_w_down, b_q_norm, b_w_uq, b_kv_norm, b_w_ukv, b_w_o, c_w_qkv, c_rel_bias, c_w_o,
           ffn_w_in, ffn_w_out):
    bsz, seq, d = x.shape
    m = bsz * seq
    rope_a = _rope_adjacent(seq, A_ROT_DIM)
    rope_b = _rope_swapped(seq, B_ROPE_DIM)
    xf = x.reshape(m, d)
    for i in range(DEPTH):
        kind, slot = i % N_MIXERS, i // N_MIXERS
        if kind == 0:
            ops = (((True, A_HEAD_DIM ** -0.5 * LOG2E),) * A_HEADS + ((True, 1.0),) * A_KV_HEADS
                   + (PLAIN,) * A_KV_HEADS)
            qkv = norm_proj(xf, pre_mix_norm[i], a_w_qkv[slot].astype(BF16), rope_a, seq=seq,
                            tm=1024, tn=len(ops) * LANES, kinds=((0, 1, ops),))
            o = window_attention(qkv.reshape(bsz, seq, -1), a_sinks[slot].astype(F32), nblk=4)
            w_o = a_w_o[slot]
        elif kind == 1:
            wd = b_w_down[slot]
            wkr = jnp.pad(wd[:, B_Q_RANK + B_KV_RANK:], ((0, 0), (0, LANES - B_ROPE_DIM)))
            cq, ckv = mla_down(xf, pre_mix_norm[i], wd[:, :B_Q_RANK].astype(BF16),
                               wd[:, B_Q_RANK:B_Q_RANK + B_KV_RANK].astype(BF16),
                               _pair_split(wkr, B_ROPE_DIM).astype(BF16),
                               b_q_norm[slot], b_kv_norm[slot], rope_b, seq=seq, tm=512)
            tn = 2048
            wuq = b_w_uq[slot].reshape(B_Q_RANK, B_HEADS, B_NOPE_DIM + B_ROPE_DIM)
            wuq_r = jnp.pad(wuq[:, :, B_NOPE_DIM:], ((0, 0), (0, 0), (0, LANES - B_ROPE_DIM)))
            wuq = jnp.concatenate([wuq[:, :, :B_NOPE_DIM], _pair_split(wuq_r, B_ROPE_DIM)], axis=2)
            scale = (B_NOPE_DIM + B_ROPE_DIM) ** -0.5 * LOG2E
            q_ops = ((False, scale), (True, scale)) * (tn // B_QK_PAD)
            qcat = proj(cq, wuq.reshape(B_Q_RANK, -1).astype(BF16), rope_b, seq=seq,
                        tm=1024, tn=tn, kinds=((0, 1, q_ops),))
            wukv = b_w_ukv[slot].reshape(B_KV_RANK, B_HEADS, B_NOPE_DIM + B_V_DIM)
            eye = jnp.broadcast_to(jnp.eye(LANES, dtype=F32)[:, None, :], (LANES, B_HEADS, LANES))
            wk = jnp.concatenate(
                [jnp.pad(wukv[:, :, :B_NOPE_DIM], ((0, LANES), (0, 0), (0, 0))),
                 jnp.pad(eye, ((B_KV_RANK, 0), (0, 0), (0, 0)))], axis=2)
            wv = jnp.pad(wukv[:, :, B_NOPE_DIM:], ((0, LANES), (0, 0), (0, 0)))
            wkv = jnp.concatenate([wk.reshape(B_KV_RANK + LANES, -1),
                                   wv.reshape(B_KV_RANK + LANES, -1)], axis=1).astype(BF16)
            kvcat = proj(ckv, wkv, rope_b, seq=seq, tm=1024, tn=tn,
                         kinds=((0, 1, (PLAIN,) * (tn // LANES)),))
            o = mla_attention(qcat.reshape(bsz, seq, -1), kvcat.reshape(bsz, seq, -1), tq=2048,
                              chunk=256)
            w_o = b_w_o[slot]
        else:
            tn = C_HEADS * C_HEAD_DIM
            q_ops = ((False, C_HEAD_DIM ** -0.5 * LOG2E),) * C_HEADS
            qkv = norm_proj(xf, pre_mix_norm[i], c_w_qkv[slot].astype(BF16), rope_a, seq=seq,
                            tm=1024, tn=tn, kinds=((0, 1, q_ops), (1, 3, (PLAIN,) * C_HEADS)))
            o = neighborhood_attention(qkv.reshape(bsz, seq, -1), _nb_bias_slabs(c_rel_bias[slot]))
            w_o = c_w_o[slot]
        xf = out_proj_residual(o.reshape(m, -1), w_o.astype(BF16), post_mix_norm[i], xf, tm=512)
        xf = ffn(xf, pre_ffn_norm[i], ffn_w_in[i].astype(BF16), ffn_w_out[i].astype(BF16),
                 post_ffn_norm[i], tm=1024, tf=512)
    return xf.reshape(bsz, seq, d)
```

```python
import functools

import jax
import jax.numpy as jnp
import numpy as np
from jax import lax
from jax.experimental import pallas as pl
from jax.experimental.pallas import tpu as pltpu

D_MODEL = 2048
DEPTH = 4
N_MIXERS = 3
GRID_W = 64
ROPE_THETA = 500000.0
NORM_EPS = 1e-6
D_FF = 5632

A_HEADS = 16
A_KV_HEADS = 4
A_GROUP = A_HEADS // A_KV_HEADS
A_HEAD_DIM = 128
A_ROT_DIM = 32
WINDOW = 128
A_BLOCK = 128

B_HEADS = 16
B_NOPE_DIM = 128
B_ROPE_DIM = 64
B_V_DIM = 128
B_Q_RANK = 512
B_KV_RANK = 512
B_QK_PAD = 256

C_HEADS = 16
C_HEAD_DIM = 128
NB_H = 8
NB_W = 16
C_QROWS = 8
C_KROWS = 16

LANES = 128
FFN_PIECE = 256
PROJ_CHUNK = 512
ROPE_SWAP = LANES // 2
V7X_VMEM_BYTES = 64 * 1024 * 1024
V7X_VMEM_LIMIT_BYTES = V7X_VMEM_BYTES - 4 * 1024 * 1024

NEG_BIG = -1e30
LOG2E = float(np.log2(np.e))
BF16 = jnp.bfloat16
F32 = jnp.float32
PLAIN = (False, 1.0)


def _cparams(semantics):
    return pltpu.CompilerParams(dimension_semantics=semantics,
                                vmem_limit_bytes=V7X_VMEM_LIMIT_BYTES)


def _rms(xf, g):
    ms = jnp.mean(xf * xf, axis=-1, keepdims=True)
    return xf * lax.rsqrt(ms + NORM_EPS) * g


def _inv_rms(src_ref, stat_ref):
    xf = src_ref[...]
    stat_ref[...] = lax.rsqrt(jnp.mean(xf * xf, axis=-1, keepdims=True) + NORM_EPS)
    return stat_ref[...]


def _dot(a, b):
    return jnp.dot(a, b, preferred_element_type=F32)


def _dot_nt(a, b):
    return lax.dot_general(a, b, (((1,), (1,)), ((), ())), preferred_element_type=F32)


def _rope(y, tabs, shift):
    r = y * tabs[0][...] + pltpu.roll(y, shift, 1) * tabs[1][...]
    if shift != ROPE_SWAP:
        r = r + pltpu.roll(y, LANES - shift, 1) * tabs[2][...]
    return r


def _project_tile(lhs, w_ref, o_ref, tabs, shift, ops):
    per_chunk = PROJ_CHUNK // LANES
    for c0 in range(0, len(ops), per_chunk):
        chunk = slice(c0 * LANES, (c0 + per_chunk) * LANES)
        y = _dot(lhs, w_ref[:, chunk])
        if all(op == PLAIN for op in ops[c0:c0 + per_chunk]):
            o_ref[:, chunk] = y.astype(o_ref.dtype)
            continue
        for c in range(per_chunk):
            rope, scale = ops[c0 + c]
            yc = y[:, c * LANES:(c + 1) * LANES]
            if scale != 1.0:
                yc = yc * scale
            if rope:
                yc = _rope(yc, tabs, shift)
            o_ref[:, (c0 + c) * LANES:(c0 + c + 1) * LANES] = yc.astype(o_ref.dtype)


def _project(lhs, w_ref, o_ref, tabs, shift, kinds):
    if len(kinds) == 1:
        _project_tile(lhs, w_ref, o_ref, tabs, shift, kinds[0][2])
        return
    j = pl.program_id(1)
    for lo, hi, ops in kinds:
        @pl.when((j >= lo) & (j < hi))
        def _(ops=ops):
            _project_tile(lhs, w_ref, o_ref, tabs, shift, ops)


def _weight_spec(w, tn, layer):
    k, n = w.shape[-2:]
    lead = () if layer is None else (layer,)
    shape = (None,) * len(lead) + (k, tn)
    if n == tn:
        return pl.BlockSpec(shape, lambda i, j: lead + (0, 0), pipeline_mode=pl.Buffered(1))
    return pl.BlockSpec(shape, lambda i, j: lead + (0, j))


def _norm_proj_kernel(x_ref, g_ref, w_ref, *rest, kinds, shift):
    *tabs, o_ref, h_ref = rest

    @pl.when(pl.program_id(1) == 0)
    def _():
        h_ref[...] = _rms(x_ref[...], g_ref[...]).astype(BF16)

    _project(h_ref[...], w_ref, o_ref, tabs, shift, kinds)


def norm_proj(x, g, w, rope, *, seq, tm, tn, kinds, layer=None):
    m, d = x.shape
    n = w.shape[-1]
    shift, tabs = rope
    sblk = seq // tm
    tab_spec = pl.BlockSpec((tm, LANES), lambda i, j: (i % sblk, 0))
    return pl.pallas_call(
        functools.partial(_norm_proj_kernel, kinds=kinds, shift=shift),
        grid=(m // tm, n // tn),
        in_specs=[pl.BlockSpec((tm, d), lambda i, j: (i, 0)),
                  pl.BlockSpec((1, d), lambda i, j: (0, 0)),
                  _weight_spec(w, tn, layer)] + [tab_spec] * len(tabs),
        out_specs=pl.BlockSpec((tm, tn), lambda i, j: (i, j)),
        out_shape=jax.ShapeDtypeStruct((m, n), BF16),
        scratch_shapes=[pltpu.VMEM((tm, d), BF16)],
        compiler_params=_cparams(("parallel", "arbitrary")),
        name="norm_proj",
    )(x, g.reshape(1, d), w, *tabs)


def _proj_kernel(a_ref, w_ref, *rest, kinds, shift):
    *tabs, o_ref = rest
    _project(a_ref[...], w_ref, o_ref, tabs, shift, kinds)


def proj(a, w, rope, *, seq, tm, tn, kinds, layer=None):
    m, k = a.shape
    n = w.shape[-1]
    shift, tabs = rope
    sblk = seq // tm
    tab_spec = pl.BlockSpec((tm, LANES), lambda i, j: (i % sblk, 0))
    return pl.pallas_call(
        functools.partial(_proj_kernel, kinds=kinds, shift=shift),
        grid=(m // tm, n // tn),
        in_specs=[pl.BlockSpec((tm, k), lambda i, j: (i, 0)),
                  _weight_spec(w, tn, layer)] + [tab_spec] * len(tabs),
        out_specs=pl.BlockSpec((tm, tn), lambda i, j: (i, j)),
        out_shape=jax.ShapeDtypeStruct((m, n), BF16),
        compiler_params=_cparams(("parallel", "arbitrary")),
        name="proj",
    )(a, w, *tabs)


def _out_proj_kernel(a_ref, w_ref, g_ref, x_ref, o_ref):
    m = _dot(a_ref[...], w_ref[...])
    o_ref[...] = x_ref[...] + _rms(m, g_ref[...])


def out_proj_residual(a, w, g, x, *, layer, tm):
    m, k = a.shape
    d = w.shape[-1]
    return pl.pallas_call(
        _out_proj_kernel,
        grid=(m // tm,),
        in_specs=[pl.BlockSpec((tm, k), lambda i: (i, 0)),
                  pl.BlockSpec((None, k, d), lambda i: (layer, 0, 0), pipeline_mode=pl.Buffered(1)),
                  pl.BlockSpec((1, d), lambda i: (0, 0)),
                  pl.BlockSpec((tm, d), lambda i: (i, 0))],
        out_specs=pl.BlockSpec((tm, d), lambda i: (i, 0)),
        out_shape=jax.ShapeDtypeStruct((m, d), F32),
        compiler_params=_cparams(("parallel",)),
        name="out_proj_residual",
    )(a, w, g.reshape(1, d), x)


def _ffn_kernel(x_ref, gpre_ref, wg_ref, wu_ref, wo_ref, gpost_ref, o_ref, h_ref, stat_ref,
                act_ref):
    j = pl.program_id(1)

    @pl.when(j == 0)
    def _():
        h_ref[...] = _rms(x_ref[...], gpre_ref[...]).astype(BF16)
        o_ref[...] = jnp.zeros_like(o_ref)

    h = h_ref[...]
    for c in range(0, act_ref.shape[1], FFN_PIECE):
        cols = slice(c, c + FFN_PIECE)
        gate = _dot(h, wg_ref[:, cols])
        up = _dot(h, wu_ref[:, cols])
        act_ref[:, cols] = (gate / (1.0 + jnp.exp(-gate)) * up).astype(BF16)
    o_ref[...] += _dot(act_ref[...], wo_ref[...])

    @pl.when(j == pl.num_programs(1) - 1)
    def _():
        o_ref[...] = x_ref[...] + o_ref[...] * _inv_rms(o_ref, stat_ref) * gpost_ref[...]


def ffn(x, gpre, w_in, w_out, gpost, *, layer, tm, tf):
    m, d = x.shape
    dff = w_out.shape[1]
    nf = dff // tf
    return pl.pallas_call(
        _ffn_kernel,
        grid=(m // tm, nf),
        in_specs=[pl.BlockSpec((tm, d), lambda i, j: (i, 0)),
                  pl.BlockSpec((1, d), lambda i, j: (0, 0)),
                  pl.BlockSpec((None, d, tf), lambda i, j: (layer, 0, j)),
                  pl.BlockSpec((None, d, tf), lambda i, j: (layer, 0, nf + j)),
                  pl.BlockSpec((None, tf, d), lambda i, j: (layer, j, 0)),
                  pl.BlockSpec((1, d), lambda i, j: (0, 0))],
        out_specs=pl.BlockSpec((tm, d), lambda i, j: (i, 0)),
        out_shape=jax.ShapeDtypeStruct((m, d), F32),
        scratch_shapes=[pltpu.VMEM((tm, d), BF16), pltpu.VMEM((tm, 1), F32),
                        pltpu.VMEM((tm, tf), BF16)],
        compiler_params=_cparams(("parallel", "arbitrary")),
        name="ffn",
    )(x, gpre.reshape(1, d), w_in, w_in, w_out, gpost.reshape(1, d))


def _win_attn_kernel(sink_ref, q_ref, k_ref, v_ref, o_ref, *, seq, nblk):
    kwin = 3 * A_BLOCK
    cols = A_GROUP * A_BLOCK
    for jb in range(nblk):
        blk = pl.program_id(1) * nblk + jb
        start = pl.multiple_of(jnp.clip((blk - 1) * A_BLOCK, 0, seq - kwin), A_BLOCK)
        kpos = start + lax.broadcasted_iota(jnp.int32, (kwin, cols), 0)
        qpos = blk * A_BLOCK + (lax.broadcasted_iota(jnp.int32, (kwin, cols), 1) & (A_BLOCK - 1))
        valid = jnp.abs(kpos - qpos) <= WINDOW
        qrows = slice(jb * A_BLOCK, (jb + 1) * A_BLOCK)
        for kv in range(A_KV_HEADS):
            heads = range(kv * A_GROUP, (kv + 1) * A_GROUP)
            kvcols = slice(kv * A_HEAD_DIM, (kv + 1) * A_HEAD_DIM)
            q4 = jnp.concatenate(
                [q_ref[0, qrows, h * A_HEAD_DIM:(h + 1) * A_HEAD_DIM] for h in heads], axis=0)
            kw = k_ref[0, pl.ds(start, kwin), kvcols]
            vw = v_ref[0, pl.ds(start, kwin), kvcols]
            st = jnp.where(valid, _dot_nt(kw, q4), NEG_BIG)
            sink = jnp.concatenate(
                [jnp.full((1, A_BLOCK), sink_ref[h] * LOG2E, F32) for h in heads], axis=1)
            mx = jnp.maximum(jnp.max(st, axis=0, keepdims=True), sink)
            pt = jnp.exp2(st - mx)
            denom = jnp.sum(pt, axis=0, keepdims=True) + jnp.exp2(sink - mx)
            ot = lax.dot_general(vw, pt.astype(BF16), (((0,), (0,)), ((), ())),
                                 preferred_element_type=F32) / denom
            for gi, h in enumerate(heads):
                o_ref[0, qrows, h * A_HEAD_DIM:(h + 1) * A_HEAD_DIM] = (
                    ot[:, gi * A_BLOCK:(gi + 1) * A_BLOCK].T.astype(o_ref.dtype))


def window_attention(qkv, sinks, *, nblk):
    b, s, _ = qkv.shape
    dq = A_HEADS * A_HEAD_DIM
    dkv = A_KV_HEADS * A_HEAD_DIM
    tq = nblk * A_BLOCK
    return pl.pallas_call(
        functools.partial(_win_attn_kernel, seq=s, nblk=nblk),
        grid=(b, s // tq),
        in_specs=[pl.BlockSpec(memory_space=pltpu.SMEM),
                  pl.BlockSpec((1, tq, dq), lambda bi, i: (bi, i, 0)),
                  pl.BlockSpec((1, s, dkv), lambda bi, i: (bi, 0, dq // dkv)),
                  pl.BlockSpec((1, s, dkv), lambda bi, i: (bi, 0, dq // dkv + 1))],
        out_specs=pl.BlockSpec((1, tq, dq), lambda bi, i: (bi, i, 0)),
        out_shape=jax.ShapeDtypeStruct((b, s, dq), BF16),
        compiler_params=_cparams(("parallel", "arbitrary")),
        name="window_attention",
    )(sinks, qkv, qkv, qkv)


def _mla_down_kernel(x_ref, g_ref, wq_ref, wkv_ref, wkr_ref, qn_ref, kvn_ref, *rest, shift):
    *tabs, cq_ref, ckv_ref = rest
    h = _rms(x_ref[...], g_ref[...]).astype(BF16)
    cq_ref[...] = _rms(_dot(h, wq_ref[...]), qn_ref[...]).astype(BF16)
    ckv_ref[:, :B_KV_RANK] = _rms(_dot(h, wkv_ref[...]), kvn_ref[...]).astype(BF16)
    ckv_ref[:, B_KV_RANK:] = _rope(_dot(h, wkr_ref[...]), tabs, shift).astype(BF16)


def mla_down(x, g, wq, wkv, wkr, qn, kvn, rope, *, seq, tm):
    m, d = x.shape
    shift, tabs = rope
    sblk = seq // tm
    tab_spec = pl.BlockSpec((tm, LANES), lambda i: (i % sblk, 0))
    full = lambda shape: pl.BlockSpec(shape, lambda i: (0, 0))
    return pl.pallas_call(
        functools.partial(_mla_down_kernel, shift=shift),
        grid=(m // tm,),
        in_specs=[pl.BlockSpec((tm, d), lambda i: (i, 0)), full((1, d)),
                  full((d, B_Q_RANK)), full((d, B_KV_RANK)), full((d, LANES)),
                  full((1, B_Q_RANK)), full((1, B_KV_RANK))] + [tab_spec] * len(tabs),
        out_specs=[pl.BlockSpec((tm, B_Q_RANK), lambda i: (i, 0)),
                   pl.BlockSpec((tm, B_KV_RANK + LANES), lambda i: (i, 0))],
        out_shape=[jax.ShapeDtypeStruct((m, B_Q_RANK), BF16),
                   jax.ShapeDtypeStruct((m, B_KV_RANK + LANES), BF16)],
        compiler_params=_cparams(("parallel",)),
        name="mla_down",
    )(x, g.reshape(1, d), wq, wkv, wkr, qn.reshape(1, -1), kvn.reshape(1, -1), *tabs)


def _mla_attn_kernel(q_ref, k_ref, v_ref, o_ref, *, chunk):
    v = v_ref[0]
    v_ones = jnp.concatenate([v, jnp.ones_like(v)], axis=1)
    dv = v.shape[1]
    for c in range(q_ref.shape[1] // chunk):
        rows = slice(c * chunk, (c + 1) * chunk)
        s = _dot_nt(q_ref[0, rows, :], k_ref[0])
        p = jnp.exp2(s - jnp.max(s, axis=-1, keepdims=True))
        o = _dot(p.astype(BF16), v_ones)
        o_ref[0, rows, :] = (o[:, :dv] / o[:, dv:]).astype(o_ref.dtype)


def mla_attention(qcat, kvcat, *, tq, chunk):
    b, s, _ = qcat.shape
    vblk0 = B_HEADS * B_QK_PAD // B_V_DIM
    return pl.pallas_call(
        functools.partial(_mla_attn_kernel, chunk=chunk),
        grid=(b, B_HEADS, s // tq),
        in_specs=[pl.BlockSpec((1, tq, B_QK_PAD), lambda bi, h, qi: (bi, qi, h)),
                  pl.BlockSpec((1, s, B_QK_PAD), lambda bi, h, qi: (bi, 0, h)),
                  pl.BlockSpec((1, s, B_V_DIM), lambda bi, h, qi: (bi, 0, vblk0 + h))],
        out_specs=pl.BlockSpec((1, tq, B_V_DIM), lambda bi, h, qi: (bi, qi, h)),
        out_shape=jax.ShapeDtypeStruct((b, s, B_HEADS * B_V_DIM), BF16),
        compiler_params=_cparams(("parallel", "parallel", "arbitrary")),
        name="mla_attention",
    )(qcat, kvcat, kvcat)


N_DR = 2 * NB_H - 1
SLAB_LOW = N_DR - 1
SLAB_HIGH = SLAB_LOW + N_DR
SLAB_NONE = SLAB_HIGH + N_DR


def _nb_slab_plan(rows):
    plan = []
    for rb in range(rows // C_QROWS):
        w0 = min(max(rb * C_QROWS - NB_H // 2, 0), rows - C_KROWS)
        block = []
        for i in range(C_QROWS):
            r = rb * C_QROWS + i
            rs = min(max(r - NB_H // 2, 0), rows - NB_H)
            ids = []
            for pair in range(C_KROWS // 2):
                kr = w0 + 2 * pair
                ok0, ok1 = rs <= kr < rs + NB_H, rs <= kr + 1 < rs + NB_H
                dr = kr - r + NB_H - 1
                ids.append(dr if ok0 and ok1 else SLAB_LOW + dr if ok0
                           else SLAB_HIGH + dr + 1 if ok1 else SLAB_NONE)
            block.append(tuple(ids))
        plan.append(tuple(block))
    return tuple(plan)


def _nb_attn_kernel(q_ref, k_ref, v_ref, t_ref, o_ref, *, plan, rows):
    tq = C_QROWS * GRID_W
    nk = C_KROWS * GRID_W
    biases = {}
    for rb, block in enumerate(plan):
        if block not in biases:
            biases[block] = jnp.concatenate(
                [jnp.concatenate([t_ref[0, sid] for sid in ids], axis=1) for ids in block], axis=0)
        start = min(max(rb * C_QROWS - NB_H // 2, 0), rows - C_KROWS) * GRID_W
        qrows = slice(rb * tq, (rb + 1) * tq)
        kw = k_ref[0, start:start + nk, :]
        vw = v_ref[0, start:start + nk, :]
        s = _dot_nt(q_ref[0, qrows, :], kw) + biases[block]
        p = jnp.exp2(s - jnp.max(s, axis=-1, keepdims=True))
        o = _dot(p.astype(BF16), jnp.concatenate([vw, jnp.ones_like(vw)], axis=1))
        o_ref[0, qrows, :] = (o[:, :C_HEAD_DIM] / o[:, C_HEAD_DIM:]).astype(o_ref.dtype)


def neighborhood_attention(qkv, slabs):
    b, s, _ = qkv.shape
    rows = s // GRID_W
    head_spec = lambda col0: pl.BlockSpec((1, s, C_HEAD_DIM), lambda h, bi: (bi, 0, col0 + h))
    return pl.pallas_call(
        functools.partial(_nb_attn_kernel, plan=_nb_slab_plan(rows), rows=rows),
        grid=(C_HEADS, b),
        in_specs=[head_spec(0), head_spec(C_HEADS), head_spec(2 * C_HEADS),
                  pl.BlockSpec((1,) + slabs.shape[1:], lambda h, bi: (h, 0, 0, 0))],
        out_specs=head_spec(0),
        out_shape=jax.ShapeDtypeStruct((b, s, C_HEADS * C_HEAD_DIM), BF16),
        compiler_params=_cparams(("parallel", "arbitrary")),
        name="neighborhood_attention",
    )(qkv, qkv, qkv, slabs)


def _nb_bias_slabs(rel_bias):
    c = np.arange(GRID_W)[:, None]
    kc = np.arange(GRID_W)[None, :]
    cs = np.clip(c - NB_W // 2, 0, GRID_W - NB_W)
    col_ok = (kc >= cs) & (kc < cs + NB_W)
    dc = np.clip(kc - c + NB_W - 1, 0, 2 * NB_W - 2)
    bias = jnp.where(col_ok[None, None], rel_bias.astype(F32)[:, :, dc] * LOG2E, NEG_BIG)
    masked = jnp.full_like(bias, NEG_BIG)
    return jnp.concatenate([jnp.concatenate([bias[:, :-1], bias[:, 1:]], axis=-1),
                            jnp.concatenate([bias, masked], axis=-1),
                            jnp.concatenate([masked, bias], axis=-1),
                            jnp.concatenate([masked[:, :1], masked[:, :1]], axis=-1)], axis=1)


def _rope_angles(seq, dim):
    pos = jnp.arange(seq, dtype=F32)
    inv = ROPE_THETA ** (-jnp.arange(0, dim, 2, dtype=F32) / dim)
    ang = pos[:, None] * inv[None, :]
    return jnp.cos(ang), jnp.sin(ang)


def _rope_adjacent(seq, dim):
    half = dim // 2
    c, sn = _rope_angles(seq, dim)
    cos = jnp.ones((seq, LANES), F32).at[:, :dim].set(jnp.concatenate([c, c], axis=1))
    up = jnp.zeros((seq, LANES), F32).at[:, half:dim].set(sn)
    down = jnp.zeros((seq, LANES), F32).at[:, :half].set(-sn)
    return half, (cos, up, down)


def _rope_swapped(seq, dim):
    half = dim // 2
    c, sn = _rope_angles(seq, dim)
    cos = jnp.ones((seq, LANES), F32).at[:, :half].set(c).at[:, ROPE_SWAP:ROPE_SWAP + half].set(c)
    sin = jnp.zeros((seq, LANES), F32).at[:, :half].set(-sn).at[:, ROPE_SWAP:ROPE_SWAP + half].set(sn)
    return ROPE_SWAP, (cos, sin)


def _pair_split(w, dim):
    half = dim // 2
    gap = ROPE_SWAP - half
    return jnp.concatenate([w[..., :half], w[..., dim:dim + gap], w[..., half:dim],
                            w[..., dim + gap:]], axis=-1)


def kernel(x, pre_mix_norm, post_mix_norm, pre_ffn_norm, post_ffn_norm, a_w_qkv, a_sinks, a_w_o,
           b_w_down, b_q_norm, b_w_uq, b_kv_norm, b_w_ukv, b_w_o, c_w_qkv, c_rel_bias, c_w_o,
           ffn_w_in, ffn_w_out):
    bsz, seq, d = x.shape
    m = bsz * seq
    rope_a = _rope_adjacent(seq, A_ROT_DIM)
    rope_b = _rope_swapped(seq, B_ROPE_DIM)
    w_in, w_out = ffn_w_in.astype(BF16), ffn_w_out.astype(BF16)
    xf = x.reshape(m, d)
    for i in range(DEPTH):
        kind, slot = i % N_MIXERS, i // N_MIXERS
        if kind == 0:
            ops = (((True, A_HEAD_DIM ** -0.5 * LOG2E),) * A_HEADS + ((True, 1.0),) * A_KV_HEADS
                   + (PLAIN,) * A_KV_HEADS)
            qkv = norm_proj(xf, pre_mix_norm[i], a_w_qkv.astype(BF16), rope_a, seq=seq, tm=1024,
                            tn=len(ops) * LANES, kinds=((0, 1, ops),), layer=slot)
            o = window_attention(qkv.reshape(bsz, seq, -1), a_sinks[slot].astype(F32), nblk=4)
            w_o = a_w_o
        elif kind == 1:
            wd = b_w_down[slot]
            wkr = jnp.pad(wd[:, B_Q_RANK + B_KV_RANK:], ((0, 0), (0, LANES - B_ROPE_DIM)))
            cq, ckv = mla_down(xf, pre_mix_norm[i], wd[:, :B_Q_RANK].astype(BF16),
                               wd[:, B_Q_RANK:B_Q_RANK + B_KV_RANK].astype(BF16),
                               _pair_split(wkr, B_ROPE_DIM).astype(BF16),
                               b_q_norm[slot], b_kv_norm[slot], rope_b, seq=seq, tm=512)
            tn = 2048
            wuq = b_w_uq[slot].reshape(B_Q_RANK, B_HEADS, B_NOPE_DIM + B_ROPE_DIM)
            wuq_r = jnp.pad(wuq[:, :, B_NOPE_DIM:], ((0, 0), (0, 0), (0, LANES - B_ROPE_DIM)))
            wuq = jnp.concatenate([wuq[:, :, :B_NOPE_DIM], _pair_split(wuq_r, B_ROPE_DIM)], axis=2)
            scale = (B_NOPE_DIM + B_ROPE_DIM) ** -0.5 * LOG2E
            q_ops = ((False, scale), (True, scale)) * (tn // B_QK_PAD)
            qcat = proj(cq, wuq.reshape(B_Q_RANK, -1).astype(BF16), rope_b, seq=seq,
                        tm=1024, tn=tn, kinds=((0, 1, q_ops),))
            wukv = b_w_ukv[slot].reshape(B_KV_RANK, B_HEADS, B_NOPE_DIM + B_V_DIM)
            eye = jnp.broadcast_to(jnp.eye(LANES, dtype=F32)[:, None, :], (LANES, B_HEADS, LANES))
            wk = jnp.concatenate(
                [jnp.pad(wukv[:, :, :B_NOPE_DIM], ((0, LANES), (0, 0), (0, 0))),
                 jnp.pad(eye, ((B_KV_RANK, 0), (0, 0), (0, 0)))], axis=2)
            wv = jnp.pad(wukv[:, :, B_NOPE_DIM:], ((0, LANES), (0, 0), (0, 0)))
            wkv = jnp.concatenate([wk.reshape(B_KV_RANK + LANES, -1),
                                   wv.reshape(B_KV_RANK + LANES, -1)], axis=1).astype(BF16)
            kvcat = proj(ckv, wkv, rope_b, seq=seq, tm=1024, tn=tn,
                         kinds=((0, 1, (PLAIN,) * (tn // LANES)),))
            o = mla_attention(qcat.reshape(bsz, seq, -1), kvcat.reshape(bsz, seq, -1), tq=2048,
                              chunk=256)
            w_o = b_w_o
        else:
            tn = C_HEADS * C_HEAD_DIM
            q_ops = ((False, C_HEAD_DIM ** -0.5 * LOG2E),) * C_HEADS
            qkv = norm_proj(xf, pre_mix_norm[i], c_w_qkv.astype(BF16), rope_a, seq=seq, tm=1024,
                            tn=tn, kinds=((0, 1, q_ops), (1, 3, (PLAIN,) * C_HEADS)), layer=slot)
            o = neighborhood_attention(qkv.reshape(bsz, seq, -1), _nb_bias_slabs(c_rel_bias[slot]))
            w_o = c_w_o
        xf = out_proj_residual(o.reshape(m, -1), w_o.astype(BF16), post_mix_norm[i], xf,
                               layer=slot, tm=512)
        xf = ffn(xf, pre_ffn_norm[i], w_in, w_out, post_ffn_norm[i], layer=i, tm=1024, tf=512)
    return xf.reshape(bsz, seq, d)
```

```python
import functools

import jax
import jax.numpy as jnp
import numpy as np
from jax import lax
from jax.experimental import pallas as pl
from jax.experimental.pallas import tpu as pltpu

D_MODEL = 2048
DEPTH = 4
N_MIXERS = 3
GRID_W = 64
ROPE_THETA = 500000.0
NORM_EPS = 1e-6
D_FF = 5632

A_HEADS = 16
A_KV_HEADS = 4
A_GROUP = A_HEADS // A_KV_HEADS
A_HEAD_DIM = 128
A_ROT_DIM = 32
WINDOW = 128
A_BLOCK = 128

B_HEADS = 16
B_NOPE_DIM = 128
B_ROPE_DIM = 64
B_V_DIM = 128
B_Q_RANK = 512
B_KV_RANK = 512
B_QK_PAD = 256

C_HEADS = 16
C_HEAD_DIM = 128
NB_H = 8
NB_W = 16
C_QROWS = 8
C_KROWS = 16
C_HEADS_PER_STEP = 1

LANES = 128
OUT_ROW_PIECES = 2
FFN_PIECE = 256
PROJ_CHUNK = 512
ROPE_SWAP = LANES // 2
V7X_VMEM_BYTES = 64 * 1024 * 1024
V7X_VMEM_LIMIT_BYTES = V7X_VMEM_BYTES - 4 * 1024 * 1024

NEG_BIG = -1e30
LOG2E = float(np.log2(np.e))
BF16 = jnp.bfloat16
F32 = jnp.float32
PLAIN = (False, 1.0)


def _cparams(semantics):
    return pltpu.CompilerParams(dimension_semantics=semantics,
                                vmem_limit_bytes=V7X_VMEM_LIMIT_BYTES)


def _rms(xf, g):
    ms = jnp.mean(xf * xf, axis=-1, keepdims=True)
    return xf * lax.rsqrt(ms + NORM_EPS) * g


def _inv_rms(src_ref, stat_ref):
    xf = src_ref[...]
    stat_ref[...] = lax.rsqrt(jnp.mean(xf * xf, axis=-1, keepdims=True) + NORM_EPS)
    return stat_ref[...]


def _dot(a, b):
    return jnp.dot(a, b, preferred_element_type=F32)


def _dot_nt(a, b):
    return lax.dot_general(a, b, (((1,), (1,)), ((), ())), preferred_element_type=F32)


def _rope(y, tabs, shift):
    r = y * tabs[0][...] + pltpu.roll(y, shift, 1) * tabs[1][...]
    if shift != ROPE_SWAP:
        r = r + pltpu.roll(y, LANES - shift, 1) * tabs[2][...]
    return r


def _project_tile(lhs, w_ref, o_ref, tabs, shift, ops):
    per_chunk = PROJ_CHUNK // LANES
    for c0 in range(0, len(ops), per_chunk):
        chunk = slice(c0 * LANES, (c0 + per_chunk) * LANES)
        y = _dot(lhs, w_ref[:, chunk])
        if all(op == PLAIN for op in ops[c0:c0 + per_chunk]):
            o_ref[:, chunk] = y.astype(o_ref.dtype)
            continue
        for c in range(per_chunk):
            rope, scale = ops[c0 + c]
            yc = y[:, c * LANES:(c + 1) * LANES]
            if scale != 1.0:
                yc = yc * scale
            if rope:
                yc = _rope(yc, tabs, shift)
            o_ref[:, (c0 + c) * LANES:(c0 + c + 1) * LANES] = yc.astype(o_ref.dtype)


def _project(lhs, w_ref, o_ref, tabs, shift, kinds):
    if len(kinds) == 1:
        _project_tile(lhs, w_ref, o_ref, tabs, shift, kinds[0][2])
        return
    j = pl.program_id(1)
    for lo, hi, ops in kinds:
        @pl.when((j >= lo) & (j < hi))
        def _(ops=ops):
            _project_tile(lhs, w_ref, o_ref, tabs, shift, ops)


def _weight_spec(w, tn, layer):
    k, n = w.shape[-2:]
    lead = () if layer is None else (layer,)
    shape = (None,) * len(lead) + (k, tn)
    if n == tn:
        return pl.BlockSpec(shape, lambda i, j: lead + (0, 0), pipeline_mode=pl.Buffered(1))
    return pl.BlockSpec(shape, lambda i, j: lead + (0, j))


def _norm_proj_kernel(x_ref, g_ref, w_ref, *rest, kinds, shift):
    *tabs, o_ref, h_ref = rest

    @pl.when(pl.program_id(1) == 0)
    def _():
        h_ref[...] = _rms(x_ref[...], g_ref[...]).astype(BF16)

    _project(h_ref[...], w_ref, o_ref, tabs, shift, kinds)


def norm_proj(x, g, w, rope, *, seq, tm, tn, kinds, layer=None):
    m, d = x.shape
    n = w.shape[-1]
    shift, tabs = rope
    sblk = seq // tm
    tab_spec = pl.BlockSpec((tm, LANES), lambda i, j: (i % sblk, 0))
    return pl.pallas_call(
        functools.partial(_norm_proj_kernel, kinds=kinds, shift=shift),
        grid=(m // tm, n // tn),
        in_specs=[pl.BlockSpec((tm, d), lambda i, j: (i, 0)),
                  pl.BlockSpec((1, d), lambda i, j: (0, 0)),
                  _weight_spec(w, tn, layer)] + [tab_spec] * len(tabs),
        out_specs=pl.BlockSpec((tm, tn), lambda i, j: (i, j)),
        out_shape=jax.ShapeDtypeStruct((m, n), BF16),
        scratch_shapes=[pltpu.VMEM((tm, d), BF16)],
        compiler_params=_cparams(("parallel", "arbitrary")),
        name="norm_proj",
    )(x, g.reshape(1, d), w, *tabs)


def _proj_kernel(a_ref, w_ref, *rest, kinds, shift):
    *tabs, o_ref = rest
    _project(a_ref[...], w_ref, o_ref, tabs, shift, kinds)


def proj(a, w, rope, *, seq, tm, tn, kinds, layer=None):
    m, k = a.shape
    n = w.shape[-1]
    shift, tabs = rope
    sblk = seq // tm
    tab_spec = pl.BlockSpec((tm, LANES), lambda i, j: (i % sblk, 0))
    return pl.pallas_call(
        functools.partial(_proj_kernel, kinds=kinds, shift=shift),
        grid=(m // tm, n // tn),
        in_specs=[pl.BlockSpec((tm, k), lambda i, j: (i, 0)),
                  _weight_spec(w, tn, layer)] + [tab_spec] * len(tabs),
        out_specs=pl.BlockSpec((tm, tn), lambda i, j: (i, j)),
        out_shape=jax.ShapeDtypeStruct((m, n), BF16),
        compiler_params=_cparams(("parallel", "arbitrary")),
        name="proj",
    )(a, w, *tabs)


def _out_proj_kernel(a_ref, w_ref, g_ref, x_ref, o_ref):
    piece = a_ref.shape[0] // OUT_ROW_PIECES
    for r in range(OUT_ROW_PIECES):
        rows = slice(r * piece, (r + 1) * piece)
        a = a_ref[rows, :]
        m = jnp.concatenate([_dot(a, w_ref[:, c:c + PROJ_CHUNK])
                             for c in range(0, w_ref.shape[1], PROJ_CHUNK)], axis=1)
        o_ref[rows, :] = x_ref[rows, :] + _rms(m, g_ref[...])


def out_proj_residual(a, w, g, x, *, layer, tm):
    m, k = a.shape
    d = w.shape[-1]
    return pl.pallas_call(
        _out_proj_kernel,
        grid=(m // tm,),
        in_specs=[pl.BlockSpec((tm, k), lambda i: (i, 0)),
                  pl.BlockSpec((None, k, d), lambda i: (layer, 0, 0), pipeline_mode=pl.Buffered(1)),
                  pl.BlockSpec((1, d), lambda i: (0, 0)),
                  pl.BlockSpec((tm, d), lambda i: (i, 0))],
        out_specs=pl.BlockSpec((tm, d), lambda i: (i, 0)),
        out_shape=jax.ShapeDtypeStruct((m, d), F32),
        compiler_params=_cparams(("parallel",)),
        name="out_proj_residual",
    )(a, w, g.reshape(1, d), x)


def _ffn_kernel(x_ref, gpre_ref, wg_ref, wu_ref, wo_ref, gpost_ref, o_ref, h_ref, stat_ref,
                act_ref):
    j = pl.program_id(1)

    @pl.when(j == 0)
    def _():
        h_ref[...] = _rms(x_ref[...], gpre_ref[...]).astype(BF16)
        o_ref[...] = jnp.zeros_like(o_ref)

    h = h_ref[...]
    for c in range(0, act_ref.shape[1], FFN_PIECE):
        cols = slice(c, c + FFN_PIECE)
        gate = _dot(h, wg_ref[:, cols])
        up = _dot(h, wu_ref[:, cols])
        act_ref[:, cols] = (gate / (1.0 + jnp.exp(-gate)) * up).astype(BF16)
    o_ref[...] += _dot(act_ref[...], wo_ref[...])

    @pl.when(j == pl.num_programs(1) - 1)
    def _():
        o_ref[...] = x_ref[...] + o_ref[...] * _inv_rms(o_ref, stat_ref) * gpost_ref[...]


def ffn(x, gpre, w_in, w_out, gpost, *, layer, tm, tf):
    m, d = x.shape
    dff = w_out.shape[1]
    nf = dff // tf
    return pl.pallas_call(
        _ffn_kernel,
        grid=(m // tm, nf),
        in_specs=[pl.BlockSpec((tm, d), lambda i, j: (i, 0)),
                  pl.BlockSpec((1, d), lambda i, j: (0, 0)),
                  pl.BlockSpec((None, d, tf), lambda i, j: (layer, 0, j)),
                  pl.BlockSpec((None, d, tf), lambda i, j: (layer, 0, nf + j)),
                  pl.BlockSpec((None, tf, d), lambda i, j: (layer, j, 0)),
                  pl.BlockSpec((1, d), lambda i, j: (0, 0))],
        out_specs=pl.BlockSpec((tm, d), lambda i, j: (i, 0)),
        out_shape=jax.ShapeDtypeStruct((m, d), F32),
        scratch_shapes=[pltpu.VMEM((tm, d), BF16), pltpu.VMEM((tm, 1), F32),
                        pltpu.VMEM((tm, tf), BF16)],
        compiler_params=_cparams(("parallel", "arbitrary")),
        name="ffn",
    )(x, gpre.reshape(1, d), w_in, w_in, w_out, gpost.reshape(1, d))


def _win_attn_kernel(sink_ref, q_ref, k_ref, v_ref, o_ref, *, seq, nblk):
    kwin = 3 * A_BLOCK
    cols = A_GROUP * A_BLOCK
    for jb in range(nblk):
        blk = pl.program_id(1) * nblk + jb
        start = pl.multiple_of(jnp.clip((blk - 1) * A_BLOCK, 0, seq - kwin), A_BLOCK)
        kpos = start + lax.broadcasted_iota(jnp.int32, (kwin, cols), 0)
        qpos = blk * A_BLOCK + (lax.broadcasted_iota(jnp.int32, (kwin, cols), 1) & (A_BLOCK - 1))
        mask = jnp.where(jnp.abs(kpos - qpos) <= WINDOW, 0.0, NEG_BIG)
        qrows = slice(jb * A_BLOCK, (jb + 1) * A_BLOCK)
        for kv in range(A_KV_HEADS):
            heads = range(kv * A_GROUP, (kv + 1) * A_GROUP)
            kvcols = slice(kv * A_HEAD_DIM, (kv + 1) * A_HEAD_DIM)
            q4 = jnp.concatenate(
                [q_ref[0, qrows, h * A_HEAD_DIM:(h + 1) * A_HEAD_DIM] for h in heads], axis=0)
            kw = k_ref[0, pl.ds(start, kwin), kvcols]
            vw = v_ref[0, pl.ds(start, kwin), kvcols]
            st = _dot_nt(kw, q4) + mask
            sink = jnp.concatenate(
                [jnp.full((1, A_BLOCK), sink_ref[h] * LOG2E, F32) for h in heads], axis=1)
            mx = jnp.maximum(jnp.max(st, axis=0, keepdims=True), sink)
            pt = jnp.exp2(st - mx)
            denom = jnp.sum(pt, axis=0, keepdims=True) + jnp.exp2(sink - mx)
            ot = lax.dot_general(vw, pt.astype(BF16), (((0,), (0,)), ((), ())),
                                 preferred_element_type=F32) / denom
            for gi, h in enumerate(heads):
                o_ref[0, qrows, h * A_HEAD_DIM:(h + 1) * A_HEAD_DIM] = (
                    ot[:, gi * A_BLOCK:(gi + 1) * A_BLOCK].T.astype(o_ref.dtype))


def window_attention(qkv, sinks, *, nblk):
    b, s, _ = qkv.shape
    dq = A_HEADS * A_HEAD_DIM
    dkv = A_KV_HEADS * A_HEAD_DIM
    tq = nblk * A_BLOCK
    return pl.pallas_call(
        functools.partial(_win_attn_kernel, seq=s, nblk=nblk),
        grid=(b, s // tq),
        in_specs=[pl.BlockSpec(memory_space=pltpu.SMEM),
                  pl.BlockSpec((1, tq, dq), lambda bi, i: (bi, i, 0)),
                  pl.BlockSpec((1, s, dkv), lambda bi, i: (bi, 0, dq // dkv)),
                  pl.BlockSpec((1, s, dkv), lambda bi, i: (bi, 0, dq // dkv + 1))],
        out_specs=pl.BlockSpec((1, tq, dq), lambda bi, i: (bi, i, 0)),
        out_shape=jax.ShapeDtypeStruct((b, s, dq), BF16),
        compiler_params=_cparams(("parallel", "arbitrary")),
        name="window_attention",
    )(sinks, qkv, qkv, qkv)


def _mla_down_kernel(x_ref, g_ref, wq_ref, wkv_ref, wkr_ref, qn_ref, kvn_ref, *rest, shift):
    *tabs, cq_ref, ckv_ref = rest
    h = _rms(x_ref[...], g_ref[...]).astype(BF16)
    cq_ref[...] = _rms(_dot(h, wq_ref[...]), qn_ref[...]).astype(BF16)
    ckv_ref[:, :B_KV_RANK] = _rms(_dot(h, wkv_ref[...]), kvn_ref[...]).astype(BF16)
    ckv_ref[:, B_KV_RANK:] = _rope(_dot(h, wkr_ref[...]), tabs, shift).astype(BF16)


def mla_down(x, g, wq, wkv, wkr, qn, kvn, rope, *, seq, tm):
    m, d = x.shape
    shift, tabs = rope
    sblk = seq // tm
    tab_spec = pl.BlockSpec((tm, LANES), lambda i: (i % sblk, 0))
    full = lambda shape: pl.BlockSpec(shape, lambda i: (0, 0))
    return pl.pallas_call(
        functools.partial(_mla_down_kernel, shift=shift),
        grid=(m // tm,),
        in_specs=[pl.BlockSpec((tm, d), lambda i: (i, 0)), full((1, d)),
                  full((d, B_Q_RANK)), full((d, B_KV_RANK)), full((d, LANES)),
                  full((1, B_Q_RANK)), full((1, B_KV_RANK))] + [tab_spec] * len(tabs),
        out_specs=[pl.BlockSpec((tm, B_Q_RANK), lambda i: (i, 0)),
                   pl.BlockSpec((tm, B_KV_RANK + LANES), lambda i: (i, 0))],
        out_shape=[jax.ShapeDtypeStruct((m, B_Q_RANK), BF16),
                   jax.ShapeDtypeStruct((m, B_KV_RANK + LANES), BF16)],
        compiler_params=_cparams(("parallel",)),
        name="mla_down",
    )(x, g.reshape(1, d), wq, wkv, wkr, qn.reshape(1, -1), kvn.reshape(1, -1), *tabs)


def _mla_attn_kernel(q_ref, k_ref, v_ref, o_ref, *, chunk):
    v = v_ref[0]
    v_ones = jnp.concatenate([v, jnp.ones_like(v)], axis=1)
    dv = v.shape[1]
    for c in range(q_ref.shape[1] // chunk):
        rows = slice(c * chunk, (c + 1) * chunk)
        s = _dot_nt(q_ref[0, rows, :], k_ref[0])
        p = jnp.exp2(s - jnp.max(s, axis=-1, keepdims=True))
        o = _dot(p.astype(BF16), v_ones)
        o_ref[0, rows, :] = (o[:, :dv] / o[:, dv:]).astype(o_ref.dtype)


def mla_attention(qcat, kvcat, *, tq, chunk):
    b, s, _ = qcat.shape
    vblk0 = B_HEADS * B_QK_PAD // B_V_DIM
    return pl.pallas_call(
        functools.partial(_mla_attn_kernel, chunk=chunk),
        grid=(b, B_HEADS, s // tq),
        in_specs=[pl.BlockSpec((1, tq, B_QK_PAD), lambda bi, h, qi: (bi, qi, h)),
                  pl.BlockSpec((1, s, B_QK_PAD), lambda bi, h, qi: (bi, 0, h)),
                  pl.BlockSpec((1, s, B_V_DIM), lambda bi, h, qi: (bi, 0, vblk0 + h))],
        out_specs=pl.BlockSpec((1, tq, B_V_DIM), lambda bi, h, qi: (bi, qi, h)),
        out_shape=jax.ShapeDtypeStruct((b, s, B_HEADS * B_V_DIM), BF16),
        compiler_params=_cparams(("parallel", "parallel", "arbitrary")),
        name="mla_attention",
    )(qcat, kvcat, kvcat)


N_DR = 2 * NB_H - 1
SLAB_LOW = N_DR - 1
SLAB_HIGH = SLAB_LOW + N_DR
SLAB_NONE = SLAB_HIGH + N_DR


def _nb_slab_plan(rows):
    plan = []
    for rb in range(rows // C_QROWS):
        w0 = min(max(rb * C_QROWS - NB_H // 2, 0), rows - C_KROWS)
        block = []
        for i in range(C_QROWS):
            r = rb * C_QROWS + i
            rs = min(max(r - NB_H // 2, 0), rows - NB_H)
            ids = []
            for pair in range(C_KROWS // 2):
                kr = w0 + 2 * pair
                ok0, ok1 = rs <= kr < rs + NB_H, rs <= kr + 1 < rs + NB_H
                dr = kr - r + NB_H - 1
                ids.append(dr if ok0 and ok1 else SLAB_LOW + dr if ok0
                           else SLAB_HIGH + dr + 1 if ok1 else SLAB_NONE)
            block.append(tuple(ids))
        plan.append(tuple(block))
    return tuple(plan)


def _nb_attn_kernel(q_ref, k_ref, v_ref, t_ref, o_ref, *, plan, rows):
    tq = C_QROWS * GRID_W
    nk = C_KROWS * GRID_W
    for hh in range(C_HEADS_PER_STEP):
        cols = slice(hh * C_HEAD_DIM, (hh + 1) * C_HEAD_DIM)
        biases = {}
        for rb, block in enumerate(plan):
            if block not in biases:
                biases[block] = jnp.concatenate(
                    [jnp.concatenate([t_ref[hh, sid] for sid in ids], axis=1) for ids in block],
                    axis=0)
            start = min(max(rb * C_QROWS - NB_H // 2, 0), rows - C_KROWS) * GRID_W
            qrows = slice(rb * tq, (rb + 1) * tq)
            kw = k_ref[0, start:start + nk, cols]
            vw = v_ref[0, start:start + nk, cols]
            s = _dot_nt(q_ref[0, qrows, cols], kw) + biases[block]
            p = jnp.exp2(s - jnp.max(s, axis=-1, keepdims=True))
            o = _dot(p.astype(BF16), jnp.concatenate([vw, jnp.ones_like(vw)], axis=1))
            o_ref[0, qrows, cols] = (o[:, :C_HEAD_DIM] / o[:, C_HEAD_DIM:]).astype(o_ref.dtype)


def neighborhood_attention(qkv, slabs):
    b, s, _ = qkv.shape
    rows = s // GRID_W
    width = C_HEADS_PER_STEP * C_HEAD_DIM
    nhb = C_HEADS // C_HEADS_PER_STEP
    head_spec = lambda blk0: pl.BlockSpec((1, s, width), lambda h, bi: (bi, 0, blk0 + h))
    return pl.pallas_call(
        functools.partial(_nb_attn_kernel, plan=_nb_slab_plan(rows), rows=rows),
        grid=(nhb, b),
        in_specs=[head_spec(0), head_spec(nhb), head_spec(2 * nhb),
                  pl.BlockSpec((C_HEADS_PER_STEP,) + slabs.shape[1:], lambda h, bi: (h, 0, 0, 0))],
        out_specs=head_spec(0),
        out_shape=jax.ShapeDtypeStruct((b, s, C_HEADS * C_HEAD_DIM), BF16),
        compiler_params=_cparams(("parallel", "arbitrary")),
        name="neighborhood_attention",
    )(qkv, qkv, qkv, slabs)


def _nb_bias_slabs(rel_bias):
    c = np.arange(GRID_W)[:, None]
    kc = np.arange(GRID_W)[None, :]
    cs = np.clip(c - NB_W // 2, 0, GRID_W - NB_W)
    col_ok = (kc >= cs) & (kc < cs + NB_W)
    dc = np.clip(kc - c + NB_W - 1, 0, 2 * NB_W - 2)
    bias = jnp.where(col_ok[None, None], rel_bias.astype(F32)[:, :, dc] * LOG2E, NEG_BIG)
    masked = jnp.full_like(bias, NEG_BIG)
    return jnp.concatenate([jnp.concatenate([bias[:, :-1], bias[:, 1:]], axis=-1),
                            jnp.concatenate([bias, masked], axis=-1),
                            jnp.concatenate([masked, bias], axis=-1),
                            jnp.concatenate([masked[:, :1], masked[:, :1]], axis=-1)], axis=1)


def _rope_angles(seq, dim):
    pos = jnp.arange(seq, dtype=F32)
    inv = ROPE_THETA ** (-jnp.arange(0, dim, 2, dtype=F32) / dim)
    ang = pos[:, None] * inv[None, :]
    return jnp.cos(ang), jnp.sin(ang)


def _rope_adjacent(seq, dim):
    half = dim // 2
    c, sn = _rope_angles(seq, dim)
    cos = jnp.ones((seq, LANES), F32).at[:, :dim].set(jnp.concatenate([c, c], axis=1))
    up = jnp.zeros((seq, LANES), F32).at[:, half:dim].set(sn)
    down = jnp.zeros((seq, LANES), F32).at[:, :half].set(-sn)
    return half, (cos, up, down)


def _rope_swapped(seq, dim):
    half = dim // 2
    c, sn = _rope_angles(seq, dim)
    cos = jnp.ones((seq, LANES), F32).at[:, :half].set(c).at[:, ROPE_SWAP:ROPE_SWAP + half].set(c)
    sin = jnp.zeros((seq, LANES), F32).at[:, :half].set(-sn).at[:, ROPE_SWAP:ROPE_SWAP + half].set(sn)
    return ROPE_SWAP, (cos, sin)


def _pair_split(w, dim):
    half = dim // 2
    gap = ROPE_SWAP - half
    return jnp.concatenate([w[..., :half], w[..., dim:dim + gap], w[..., half:dim],
                            w[..., dim + gap:]], axis=-1)


def kernel(x, pre_mix_norm, post_mix_norm, pre_ffn_norm, post_ffn_norm, a_w_qkv, a_sinks, a_w_o,
           b_w_down, b_q_norm, b_w_uq, b_kv_norm, b_w_ukv, b_w_o, c_w_qkv, c_rel_bias, c_w_o,
           ffn_w_in, ffn_w_out):
    bsz, seq, d = x.shape
    m = bsz * seq
    rope_a = _rope_adjacent(seq, A_ROT_DIM)
    rope_b = _rope_swapped(seq, B_ROPE_DIM)
    w_in, w_out = ffn_w_in.astype(BF16), ffn_w_out.astype(BF16)
    xf = x.reshape(m, d)
    for i in range(DEPTH):
        kind, slot = i % N_MIXERS, i // N_MIXERS
        if kind == 0:
            ops = (((True, A_HEAD_DIM ** -0.5 * LOG2E),) * A_HEADS + ((True, 1.0),) * A_KV_HEADS
                   + (PLAIN,) * A_KV_HEADS)
            qkv = norm_proj(xf, pre_mix_norm[i], a_w_qkv.astype(BF16), rope_a, seq=seq, tm=1024,
                            tn=len(ops) * LANES, kinds=((0, 1, ops),), layer=slot)
            o = window_attention(qkv.reshape(bsz, seq, -1), a_sinks[slot].astype(F32), nblk=4)
            w_o = a_w_o
        elif kind == 1:
            wd = b_w_down[slot]
            wkr = jnp.pad(wd[:, B_Q_RANK + B_KV_RANK:], ((0, 0), (0, LANES - B_ROPE_DIM)))
            cq, ckv = mla_down(xf, pre_mix_norm[i], wd[:, :B_Q_RANK].astype(BF16),
                               wd[:, B_Q_RANK:B_Q_RANK + B_KV_RANK].astype(BF16),
                               _pair_split(wkr, B_ROPE_DIM).astype(BF16),
                               b_q_norm[slot], b_kv_norm[slot], rope_b, seq=seq, tm=1024)
            tn = 2048
            wuq = b_w_uq[slot].reshape(B_Q_RANK, B_HEADS, B_NOPE_DIM + B_ROPE_DIM)
            wuq_r = jnp.pad(wuq[:, :, B_NOPE_DIM:], ((0, 0), (0, 0), (0, LANES - B_ROPE_DIM)))
            wuq = jnp.concatenate([wuq[:, :, :B_NOPE_DIM], _pair_split(wuq_r, B_ROPE_DIM)], axis=2)
            scale = (B_NOPE_DIM + B_ROPE_DIM) ** -0.5 * LOG2E
            q_ops = ((False, scale), (True, scale)) * (tn // B_QK_PAD)
            qcat = proj(cq, wuq.reshape(B_Q_RANK, -1).astype(BF16), rope_b, seq=seq,
                        tm=1024, tn=tn, kinds=((0, 1, q_ops),))
            wukv = b_w_ukv[slot].reshape(B_KV_RANK, B_HEADS, B_NOPE_DIM + B_V_DIM)
            eye = jnp.broadcast_to(jnp.eye(LANES, dtype=F32)[:, None, :], (LANES, B_HEADS, LANES))
            wk = jnp.concatenate(
                [jnp.pad(wukv[:, :, :B_NOPE_DIM], ((0, LANES), (0, 0), (0, 0))),
                 jnp.pad(eye, ((B_KV_RANK, 0), (0, 0), (0, 0)))], axis=2)
            wv = jnp.pad(wukv[:, :, B_NOPE_DIM:], ((0, LANES), (0, 0), (0, 0)))
            wkv = jnp.concatenate([wk.reshape(B_KV_RANK + LANES, -1),
                                   wv.reshape(B_KV_RANK + LANES, -1)], axis=1).astype(BF16)
            kvcat = proj(ckv, wkv, rope_b, seq=seq, tm=1024, tn=tn,
                         kinds=((0, 1, (PLAIN,) * (tn // LANES)),))
            o = mla_attention(qcat.reshape(bsz, seq, -1), kvcat.reshape(bsz, seq, -1), tq=2048,
                              chunk=256)
            w_o = b_w_o
        else:
            tn = C_HEADS * C_HEAD_DIM
            q_ops = ((False, C_HEAD_DIM ** -0.5 * LOG2E),) * C_HEADS
            qkv = norm_proj(xf, pre_mix_norm[i], c_w_qkv.astype(BF16), rope_a, seq=seq, tm=1024,
                            tn=tn, kinds=((0, 1, q_ops), (1, 3, (PLAIN,) * C_HEADS)), layer=slot)
            o = neighborhood_attention(qkv.reshape(bsz, seq, -1), _nb_bias_slabs(c_rel_bias[slot]))
            w_o = c_w_o
        xf = out_proj_residual(o.reshape(m, -1), w_o.astype(BF16), post_mix_norm[i], xf,
                               layer=slot, tm=1024)
        xf = ffn(xf, pre_ffn_norm[i], w_in, w_out, post_ffn_norm[i], layer=i, tm=1024, tf=512)
    return xf.reshape(bsz, seq, d)
```

```python
import functools

import jax
import jax.numpy as jnp
import numpy as np
from jax import lax
from jax.experimental import pallas as pl
from jax.experimental.pallas import tpu as pltpu

D_MODEL = 2048
DEPTH = 4
N_MIXERS = 3
GRID_W = 64
ROPE_THETA = 500000.0
NORM_EPS = 1e-6
D_FF = 5632

A_HEADS = 16
A_KV_HEADS = 4
A_GROUP = A_HEADS // A_KV_HEADS
A_HEAD_DIM = 128
A_ROT_DIM = 32
WINDOW = 128
A_BLOCK = 128

B_HEADS = 16
B_NOPE_DIM = 128
B_ROPE_DIM = 64
B_V_DIM = 128
B_Q_RANK = 512
B_KV_RANK = 512
B_QK_PAD = 256

C_HEADS = 16
C_HEAD_DIM = 128
NB_H = 8
NB_W = 16
C_QROWS = 8
C_KROWS = 16
C_HEADS_PER_STEP = 1

LANES = 128
OUT_ROW_PIECES = 2
FFN_PIECE = 256
PROJ_CHUNK = 512
ROPE_SWAP = LANES // 2
V7X_VMEM_BYTES = 64 * 1024 * 1024
V7X_VMEM_LIMIT_BYTES = V7X_VMEM_BYTES - 4 * 1024 * 1024

NEG_BIG = -1e30
LOG2E = float(np.log2(np.e))
BF16 = jnp.bfloat16
F32 = jnp.float32
PLAIN = (False, 1.0)


def _cparams(semantics):
    return pltpu.CompilerParams(dimension_semantics=semantics,
                                vmem_limit_bytes=V7X_VMEM_LIMIT_BYTES)


def _rms(xf, g):
    ms = jnp.mean(xf * xf, axis=-1, keepdims=True)
    return xf * lax.rsqrt(ms + NORM_EPS) * g


def _inv_rms(src_ref, stat_ref):
    xf = src_ref[...]
    stat_ref[...] = lax.rsqrt(jnp.mean(xf * xf, axis=-1, keepdims=True) + NORM_EPS)
    return stat_ref[...]


def _dot(a, b):
    return jnp.dot(a, b, preferred_element_type=F32)


def _dot_nt(a, b):
    return lax.dot_general(a, b, (((1,), (1,)), ((), ())), preferred_element_type=F32)


def _rope(y, tabs, shift):
    r = y * tabs[0][...] + pltpu.roll(y, shift, 1) * tabs[1][...]
    if shift != ROPE_SWAP:
        r = r + pltpu.roll(y, LANES - shift, 1) * tabs[2][...]
    return r


def _project_tile(lhs, w_ref, o_ref, tabs, shift, ops):
    per_chunk = PROJ_CHUNK // LANES
    for c0 in range(0, len(ops), per_chunk):
        chunk = slice(c0 * LANES, (c0 + per_chunk) * LANES)
        y = _dot(lhs, w_ref[:, chunk])
        if all(op == PLAIN for op in ops[c0:c0 + per_chunk]):
            o_ref[:, chunk] = y.astype(o_ref.dtype)
            continue
        for c in range(per_chunk):
            rope, scale = ops[c0 + c]
            yc = y[:, c * LANES:(c + 1) * LANES]
            if scale != 1.0:
                yc = yc * scale
            if rope:
                yc = _rope(yc, tabs, shift)
            o_ref[:, (c0 + c) * LANES:(c0 + c + 1) * LANES] = yc.astype(o_ref.dtype)


def _project(lhs, w_ref, o_ref, tabs, shift, kinds):
    if len(kinds) == 1:
        _project_tile(lhs, w_ref, o_ref, tabs, shift, kinds[0][2])
        return
    j = pl.program_id(1)
    for lo, hi, ops in kinds:
        @pl.when((j >= lo) & (j < hi))
        def _(ops=ops):
            _project_tile(lhs, w_ref, o_ref, tabs, shift, ops)


def _weight_spec(w, tn, layer):
    k, n = w.shape[-2:]
    lead = () if layer is None else (layer,)
    shape = (None,) * len(lead) + (k, tn)
    if n == tn:
        return pl.BlockSpec(shape, lambda i, j: lead + (0, 0), pipeline_mode=pl.Buffered(1))
    return pl.BlockSpec(shape, lambda i, j: lead + (0, j))


def _norm_proj_kernel(x_ref, g_ref, w_ref, *rest, kinds, shift):
    *tabs, o_ref, h_ref = rest

    @pl.when(pl.program_id(1) == 0)
    def _():
        h_ref[...] = _rms(x_ref[...], g_ref[...]).astype(BF16)

    _project(h_ref[...], w_ref, o_ref, tabs, shift, kinds)


def norm_proj(x, g, w, rope, *, seq, tm, tn, kinds, layer=None):
    m, d = x.shape
    n = w.shape[-1]
    shift, tabs = rope
    sblk = seq // tm
    tab_spec = pl.BlockSpec((tm, LANES), lambda i, j: (i % sblk, 0))
    return pl.pallas_call(
        functools.partial(_norm_proj_kernel, kinds=kinds, shift=shift),
        grid=(m // tm, n // tn),
        in_specs=[pl.BlockSpec((tm, d), lambda i, j: (i, 0)),
                  pl.BlockSpec((1, d), lambda i, j: (0, 0)),
                  _weight_spec(w, tn, layer)] + [tab_spec] * len(tabs),
        out_specs=pl.BlockSpec((tm, tn), lambda i, j: (i, j)),
        out_shape=jax.ShapeDtypeStruct((m, n), BF16),
        scratch_shapes=[pltpu.VMEM((tm, d), BF16)],
        compiler_params=_cparams(("parallel", "arbitrary")),
        name="norm_proj",
    )(x, g.reshape(1, d), w, *tabs)


def _proj_kernel(a_ref, w_ref, *rest, kinds, shift):
    *tabs, o_ref = rest
    _project(a_ref[...], w_ref, o_ref, tabs, shift, kinds)


def proj(a, w, rope, *, seq, tm, tn, kinds, layer=None):
    m, k = a.shape
    n = w.shape[-1]
    shift, tabs = rope
    sblk = seq // tm
    tab_spec = pl.BlockSpec((tm, LANES), lambda i, j: (i % sblk, 0))
    return pl.pallas_call(
        functools.partial(_proj_kernel, kinds=kinds, shift=shift),
        grid=(m // tm, n // tn),
        in_specs=[pl.BlockSpec((tm, k), lambda i, j: (i, 0)),
                  _weight_spec(w, tn, layer)] + [tab_spec] * len(tabs),
        out_specs=pl.BlockSpec((tm, tn), lambda i, j: (i, j)),
        out_shape=jax.ShapeDtypeStruct((m, n), BF16),
        compiler_params=_cparams(("parallel", "arbitrary")),
        name="proj",
    )(a, w, *tabs)


def _out_proj_kernel(a_ref, w_ref, g_ref, x_ref, o_ref):
    piece = a_ref.shape[0] // OUT_ROW_PIECES
    for r in range(OUT_ROW_PIECES):
        rows = slice(r * piece, (r + 1) * piece)
        a = a_ref[rows, :]
        m = jnp.concatenate([_dot(a, w_ref[:, c:c + PROJ_CHUNK])
                             for c in range(0, w_ref.shape[1], PROJ_CHUNK)], axis=1)
        o_ref[rows, :] = x_ref[rows, :] + _rms(m, g_ref[...])


def out_proj_residual(a, w, g, x, *, layer, tm):
    m, k = a.shape
    d = w.shape[-1]
    return pl.pallas_call(
        _out_proj_kernel,
        grid=(m // tm,),
        in_specs=[pl.BlockSpec((tm, k), lambda i: (i, 0)),
                  pl.BlockSpec((None, k, d), lambda i: (layer, 0, 0), pipeline_mode=pl.Buffered(1)),
                  pl.BlockSpec((1, d), lambda i: (0, 0)),
                  pl.BlockSpec((tm, d), lambda i: (i, 0))],
        out_specs=pl.BlockSpec((tm, d), lambda i: (i, 0)),
        out_shape=jax.ShapeDtypeStruct((m, d), F32),
        compiler_params=_cparams(("parallel",)),
        name="out_proj_residual",
    )(a, w, g.reshape(1, d), x)


def _ffn_kernel(x_ref, gpre_ref, wg_ref, wu_ref, wo_ref, gpost_ref, o_ref, h_ref, stat_ref,
                act_ref):
    j = pl.program_id(1)

    @pl.when(j == 0)
    def _():
        h_ref[...] = _rms(x_ref[...], gpre_ref[...]).astype(BF16)
        o_ref[...] = jnp.zeros_like(o_ref)

    h = h_ref[...]
    for c in range(0, act_ref.shape[1], FFN_PIECE):
        cols = slice(c, c + FFN_PIECE)
        gate = _dot(h, wg_ref[:, cols])
        up = _dot(h, wu_ref[:, cols])
        act_ref[:, cols] = (gate / (1.0 + jnp.exp(-gate)) * up).astype(BF16)
    o_ref[...] += _dot(act_ref[...], wo_ref[...])

    @pl.when(j == pl.num_programs(1) - 1)
    def _():
        o_ref[...] = x_ref[...] + o_ref[...] * _inv_rms(o_ref, stat_ref) * gpost_ref[...]


def ffn(x, gpre, w_in, w_out, gpost, *, layer, tm, tf):
    m, d = x.shape
    dff = w_out.shape[1]
    nf = dff // tf
    return pl.pallas_call(
        _ffn_kernel,
        grid=(m // tm, nf),
        in_specs=[pl.BlockSpec((tm, d), lambda i, j: (i, 0)),
                  pl.BlockSpec((1, d), lambda i, j: (0, 0)),
                  pl.BlockSpec((None, d, tf), lambda i, j: (layer, 0, j)),
                  pl.BlockSpec((None, d, tf), lambda i, j: (layer, 0, nf + j)),
                  pl.BlockSpec((None, tf, d), lambda i, j: (layer, j, 0)),
                  pl.BlockSpec((1, d), lambda i, j: (0, 0))],
        out_specs=pl.BlockSpec((tm, d), lambda i, j: (i, 0)),
        out_shape=jax.ShapeDtypeStruct((m, d), F32),
        scratch_shapes=[pltpu.VMEM((tm, d), BF16), pltpu.VMEM((tm, 1), F32),
                        pltpu.VMEM((tm, tf), BF16)],
        compiler_params=_cparams(("parallel", "arbitrary")),
        name="ffn",
    )(x, gpre.reshape(1, d), w_in, w_in, w_out, gpost.reshape(1, d))


def _win_attn_kernel(sink_ref, q_ref, k_ref, v_ref, o_ref, *, seq, nblk):
    kwin = 3 * A_BLOCK
    cols = A_GROUP * A_BLOCK
    for jb in range(nblk):
        blk = pl.program_id(1) * nblk + jb
        start = pl.multiple_of(jnp.clip((blk - 1) * A_BLOCK, 0, seq - kwin), A_BLOCK)
        kpos = start + lax.broadcasted_iota(jnp.int32, (kwin, cols), 0)
        qpos = blk * A_BLOCK + (lax.broadcasted_iota(jnp.int32, (kwin, cols), 1) & (A_BLOCK - 1))
        mask = jnp.where(jnp.abs(kpos - qpos) <= WINDOW, 0.0, NEG_BIG)
        qrows = slice(jb * A_BLOCK, (jb + 1) * A_BLOCK)
        for kv in range(A_KV_HEADS):
            heads = range(kv * A_GROUP, (kv + 1) * A_GROUP)
            kvcols = slice(kv * A_HEAD_DIM, (kv + 1) * A_HEAD_DIM)
            q4 = jnp.concatenate(
                [q_ref[0, qrows, h * A_HEAD_DIM:(h + 1) * A_HEAD_DIM] for h in heads], axis=0)
            kw = k_ref[0, pl.ds(start, kwin), kvcols]
            vw = v_ref[0, pl.ds(start, kwin), kvcols]
            st = _dot_nt(kw, q4) + mask
            sink = jnp.concatenate(
                [jnp.full((1, A_BLOCK), sink_ref[h] * LOG2E, F32) for h in heads], axis=1)
            mx = jnp.maximum(jnp.max(st, axis=0, keepdims=True), sink)
            pt = jnp.exp2(st - mx)
            denom = jnp.sum(pt, axis=0, keepdims=True) + jnp.exp2(sink - mx)
            ot = lax.dot_general(vw, pt.astype(BF16), (((0,), (0,)), ((), ())),
                                 preferred_element_type=F32) / denom
            for gi, h in enumerate(heads):
                o_ref[0, qrows, h * A_HEAD_DIM:(h + 1) * A_HEAD_DIM] = (
                    ot[:, gi * A_BLOCK:(gi + 1) * A_BLOCK].T.astype(o_ref.dtype))


def window_attention(qkv, sinks, *, nblk):
    b, s, _ = qkv.shape
    dq = A_HEADS * A_HEAD_DIM
    dkv = A_KV_HEADS * A_HEAD_DIM
    tq = nblk * A_BLOCK
    return pl.pallas_call(
        functools.partial(_win_attn_kernel, seq=s, nblk=nblk),
        grid=(b, s // tq),
        in_specs=[pl.BlockSpec(memory_space=pltpu.SMEM),
                  pl.BlockSpec((1, tq, dq), lambda bi, i: (bi, i, 0)),
                  pl.BlockSpec((1, s, dkv), lambda bi, i: (bi, 0, dq // dkv)),
                  pl.BlockSpec((1, s, dkv), lambda bi, i: (bi, 0, dq // dkv + 1))],
        out_specs=pl.BlockSpec((1, tq, dq), lambda bi, i: (bi, i, 0)),
        out_shape=jax.ShapeDtypeStruct((b, s, dq), BF16),
        compiler_params=_cparams(("parallel", "arbitrary")),
        name="window_attention",
    )(sinks, qkv, qkv, qkv)


def _mla_down_kernel(x_ref, g_ref, wq_ref, wkv_ref, wkr_ref, qn_ref, kvn_ref, *rest, shift):
    *tabs, cq_ref, ckv_ref, kr_ref = rest
    h = _rms(x_ref[...], g_ref[...]).astype(BF16)
    cq_ref[...] = _rms(_dot(h, wq_ref[...]), qn_ref[...]).astype(BF16)
    ckv_ref[...] = _rms(_dot(h, wkv_ref[...]), kvn_ref[...]).astype(BF16)
    kr_ref[...] = _rope(_dot(h, wkr_ref[...]), tabs, shift).astype(BF16)


def mla_down(x, g, wq, wkv, wkr, qn, kvn, rope, *, seq, tm):
    m, d = x.shape
    shift, tabs = rope
    sblk = seq // tm
    tab_spec = pl.BlockSpec((tm, LANES), lambda i: (i % sblk, 0))
    full = lambda shape: pl.BlockSpec(shape, lambda i: (0, 0))
    widths = (B_Q_RANK, B_KV_RANK, LANES)
    return pl.pallas_call(
        functools.partial(_mla_down_kernel, shift=shift),
        grid=(m // tm,),
        in_specs=[pl.BlockSpec((tm, d), lambda i: (i, 0)), full((1, d)),
                  full((d, B_Q_RANK)), full((d, B_KV_RANK)), full((d, LANES)),
                  full((1, B_Q_RANK)), full((1, B_KV_RANK))] + [tab_spec] * len(tabs),
        out_specs=[pl.BlockSpec((tm, w), lambda i: (i, 0)) for w in widths],
        out_shape=[jax.ShapeDtypeStruct((m, w), BF16) for w in widths],
        compiler_params=_cparams(("parallel",)),
        name="mla_down",
    )(x, g.reshape(1, d), wq, wkv, wkr, qn.reshape(1, -1), kvn.reshape(1, -1), *tabs)


def _mla_kv_up_kernel(c_ref, kr_ref, w_ref, k_ref, v_ref):
    c = c_ref[...]
    kr = kr_ref[...]
    head_w = B_NOPE_DIM + B_V_DIM
    per_piece = PROJ_CHUNK // head_w
    for p0 in range(0, w_ref.shape[1] // head_w, per_piece):
        y = _dot(c, w_ref[:, p0 * head_w:(p0 + per_piece) * head_w])
        for hh in range(per_piece):
            head = p0 + hh
            k0 = head * B_QK_PAD
            k_ref[:, k0:k0 + B_NOPE_DIM] = y[:, hh * head_w:hh * head_w + B_NOPE_DIM].astype(BF16)
            k_ref[:, k0 + B_NOPE_DIM:k0 + B_QK_PAD] = kr
            v_ref[:, head * B_V_DIM:(head + 1) * B_V_DIM] = (
                y[:, hh * head_w + B_NOPE_DIM:(hh + 1) * head_w].astype(BF16))


def mla_kv_up(ckv, kr, w, *, tm, heads_per_tile):
    m, k = ckv.shape
    tn = heads_per_tile * (B_NOPE_DIM + B_V_DIM)
    return pl.pallas_call(
        _mla_kv_up_kernel,
        grid=(m // tm, B_HEADS // heads_per_tile),
        in_specs=[pl.BlockSpec((tm, k), lambda i, j: (i, 0)),
                  pl.BlockSpec((tm, LANES), lambda i, j: (i, 0)),
                  pl.BlockSpec((k, tn), lambda i, j: (0, j))],
        out_specs=[pl.BlockSpec((tm, heads_per_tile * B_QK_PAD), lambda i, j: (i, j)),
                   pl.BlockSpec((tm, heads_per_tile * B_V_DIM), lambda i, j: (i, j))],
        out_shape=[jax.ShapeDtypeStruct((m, B_HEADS * B_QK_PAD), BF16),
                   jax.ShapeDtypeStruct((m, B_HEADS * B_V_DIM), BF16)],
        compiler_params=_cparams(("parallel", "arbitrary")),
        name="mla_kv_up",
    )(ckv, kr, w)


def _mla_attn_kernel(q_ref, k_ref, v_ref, o_ref, *, chunk):
    v = v_ref[0]
    v_ones = jnp.concatenate([v, jnp.ones_like(v)], axis=1)
    dv = v.shape[1]
    for c in range(q_ref.shape[1] // chunk):
        rows = slice(c * chunk, (c + 1) * chunk)
        s = _dot_nt(q_ref[0, rows, :], k_ref[0])
        p = jnp.exp2(s - jnp.max(s, axis=-1, keepdims=True))
        o = _dot(p.astype(BF16), v_ones)
        o_ref[0, rows, :] = (o[:, :dv] / o[:, dv:]).astype(o_ref.dtype)


def mla_attention(qcat, kcat, v, *, tq, chunk):
    b, s, _ = qcat.shape
    return pl.pallas_call(
        functools.partial(_mla_attn_kernel, chunk=chunk),
        grid=(b, B_HEADS, s // tq),
        in_specs=[pl.BlockSpec((1, tq, B_QK_PAD), lambda bi, h, qi: (bi, qi, h)),
                  pl.BlockSpec((1, s, B_QK_PAD), lambda bi, h, qi: (bi, 0, h)),
                  pl.BlockSpec((1, s, B_V_DIM), lambda bi, h, qi: (bi, 0, h))],
        out_specs=pl.BlockSpec((1, tq, B_V_DIM), lambda bi, h, qi: (bi, qi, h)),
        out_shape=jax.ShapeDtypeStruct((b, s, B_HEADS * B_V_DIM), BF16),
        compiler_params=_cparams(("parallel", "parallel", "arbitrary")),
        name="mla_attention",
    )(qcat, kcat, v)


N_DR = 2 * NB_H - 1
SLAB_LOW = N_DR - 1
SLAB_HIGH = SLAB_LOW + N_DR
SLAB_NONE = SLAB_HIGH + N_DR


def _nb_slab_plan(rows):
    plan = []
    for rb in range(rows // C_QROWS):
        w0 = min(max(rb * C_QROWS - NB_H // 2, 0), rows - C_KROWS)
        block = []
        for i in range(C_QROWS):
            r = rb * C_QROWS + i
            rs = min(max(r - NB_H // 2, 0), rows - NB_H)
            ids = []
            for pair in range(C_KROWS // 2):
                kr = w0 + 2 * pair
                ok0, ok1 = rs <= kr < rs + NB_H, rs <= kr + 1 < rs + NB_H
                dr = kr - r + NB_H - 1
                ids.append(dr if ok0 and ok1 else SLAB_LOW + dr if ok0
                           else SLAB_HIGH + dr + 1 if ok1 else SLAB_NONE)
            block.append(tuple(ids))
        plan.append(tuple(block))
    return tuple(plan)


def _nb_attn_kernel(q_ref, k_ref, v_ref, t_ref, o_ref, *, plan, rows):
    tq = C_QROWS * GRID_W
    nk = C_KROWS * GRID_W
    for hh in range(C_HEADS_PER_STEP):
        cols = slice(hh * C_HEAD_DIM, (hh + 1) * C_HEAD_DIM)
        biases = {}
        for rb, block in enumerate(plan):
            if block not in biases:
                biases[block] = jnp.concatenate(
                    [jnp.concatenate([t_ref[hh, sid] for sid in ids], axis=1) for ids in block],
                    axis=0)
            start = min(max(rb * C_QROWS - NB_H // 2, 0), rows - C_KROWS) * GRID_W
            qrows = slice(rb * tq, (rb + 1) * tq)
            kw = k_ref[0, start:start + nk, cols]
            vw = v_ref[0, start:start + nk, cols]
            s = _dot_nt(q_ref[0, qrows, cols], kw) + biases[block]
            p = jnp.exp2(s - jnp.max(s, axis=-1, keepdims=True))
            o = _dot(p.astype(BF16), jnp.concatenate([vw, jnp.ones_like(vw)], axis=1))
            o_ref[0, qrows, cols] = (o[:, :C_HEAD_DIM] / o[:, C_HEAD_DIM:]).astype(o_ref.dtype)


def neighborhood_attention(qkv, slabs):
    b, s, _ = qkv.shape
    rows = s // GRID_W
    width = C_HEADS_PER_STEP * C_HEAD_DIM
    nhb = C_HEADS // C_HEADS_PER_STEP
    head_spec = lambda blk0: pl.BlockSpec((1, s, width), lambda h, bi: (bi, 0, blk0 + h))
    return pl.pallas_call(
        functools.partial(_nb_attn_kernel, plan=_nb_slab_plan(rows), rows=rows),
        grid=(nhb, b),
        in_specs=[head_spec(0), head_spec(nhb), head_spec(2 * nhb),
                  pl.BlockSpec((C_HEADS_PER_STEP,) + slabs.shape[1:], lambda h, bi: (h, 0, 0, 0))],
        out_specs=head_spec(0),
        out_shape=jax.ShapeDtypeStruct((b, s, C_HEADS * C_HEAD_DIM), BF16),
        compiler_params=_cparams(("parallel", "arbitrary")),
        name="neighborhood_attention",
    )(qkv, qkv, qkv, slabs)


def _nb_bias_slabs(rel_bias):
    c = np.arange(GRID_W)[:, None]
    kc = np.arange(GRID_W)[None, :]
    cs = np.clip(c - NB_W // 2, 0, GRID_W - NB_W)
    col_ok = (kc >= cs) & (kc < cs + NB_W)
    dc = np.clip(kc - c + NB_W - 1, 0, 2 * NB_W - 2)
    bias = jnp.where(col_ok[None, None], rel_bias.astype(F32)[:, :, dc] * LOG2E, NEG_BIG)
    masked = jnp.full_like(bias, NEG_BIG)
    return jnp.concatenate([jnp.concatenate([bias[:, :-1], bias[:, 1:]], axis=-1),
                            jnp.concatenate([bias, masked], axis=-1),
                            jnp.concatenate([masked, bias], axis=-1),
                            jnp.concatenate([masked[:, :1], masked[:, :1]], axis=-1)], axis=1)


def _rope_angles(seq, dim):
    pos = jnp.arange(seq, dtype=F32)
    inv = ROPE_THETA ** (-jnp.arange(0, dim, 2, dtype=F32) / dim)
    ang = pos[:, None] * inv[None, :]
    return jnp.cos(ang), jnp.sin(ang)


def _rope_adjacent(seq, dim):
    half = dim // 2
    c, sn = _rope_angles(seq, dim)
    cos = jnp.ones((seq, LANES), F32).at[:, :dim].set(jnp.concatenate([c, c], axis=1))
    up = jnp.zeros((seq, LANES), F32).at[:, half:dim].set(sn)
    down = jnp.zeros((seq, LANES), F32).at[:, :half].set(-sn)
    return half, (cos, up, down)


def _rope_swapped(seq, dim):
    half = dim // 2
    c, sn = _rope_angles(seq, dim)
    cos = jnp.ones((seq, LANES), F32).at[:, :half].set(c).at[:, ROPE_SWAP:ROPE_SWAP + half].set(c)
    sin = jnp.zeros((seq, LANES), F32).at[:, :half].set(-sn).at[:, ROPE_SWAP:ROPE_SWAP + half].set(sn)
    return ROPE_SWAP, (cos, sin)


def _pair_split(w, dim):
    half = dim // 2
    gap = ROPE_SWAP - half
    return jnp.concatenate([w[..., :half], w[..., dim:dim + gap], w[..., half:dim],
                            w[..., dim + gap:]], axis=-1)


def kernel(x, pre_mix_norm, post_mix_norm, pre_ffn_norm, post_ffn_norm, a_w_qkv, a_sinks, a_w_o,
           b_w_down, b_q_norm, b_w_uq, b_kv_norm, b_w_ukv, b_w_o, c_w_qkv, c_rel_bias, c_w_o,
           ffn_w_in, ffn_w_out):
    bsz, seq, d = x.shape
    m = bsz * seq
    rope_a = _rope_adjacent(seq, A_ROT_DIM)
    rope_b = _rope_swapped(seq, B_ROPE_DIM)
    w_in, w_out = ffn_w_in.astype(BF16), ffn_w_out.astype(BF16)
    xf = x.reshape(m, d)
    for i in range(DEPTH):
        kind, slot = i % N_MIXERS, i // N_MIXERS
        if kind == 0:
            ops = (((True, A_HEAD_DIM ** -0.5 * LOG2E),) * A_HEADS + ((True, 1.0),) * A_KV_HEADS
                   + (PLAIN,) * A_KV_HEADS)
            qkv = norm_proj(xf, pre_mix_norm[i], a_w_qkv.astype(BF16), rope_a, seq=seq, tm=1024,
                            tn=len(ops) * LANES, kinds=((0, 1, ops),), layer=slot)
            o = window_attention(qkv.reshape(bsz, seq, -1), a_sinks[slot].astype(F32), nblk=4)
            w_o = a_w_o
        elif kind == 1:
            wd = b_w_down[slot]
            wkr = jnp.pad(wd[:, B_Q_RANK + B_KV_RANK:], ((0, 0), (0, LANES - B_ROPE_DIM)))
            cq, ckv, kr = mla_down(xf, pre_mix_norm[i], wd[:, :B_Q_RANK].astype(BF16),
                                   wd[:, B_Q_RANK:B_Q_RANK + B_KV_RANK].astype(BF16),
                                   _pair_split(wkr, B_ROPE_DIM).astype(BF16),
                                   b_q_norm[slot], b_kv_norm[slot], rope_b, seq=seq, tm=1024)
            tn = 2048
            wuq = b_w_uq[slot].reshape(B_Q_RANK, B_HEADS, B_NOPE_DIM + B_ROPE_DIM)
            wuq_r = jnp.pad(wuq[:, :, B_NOPE_DIM:], ((0, 0), (0, 0), (0, LANES - B_ROPE_DIM)))
            wuq = jnp.concatenate([wuq[:, :, :B_NOPE_DIM], _pair_split(wuq_r, B_ROPE_DIM)], axis=2)
            scale = (B_NOPE_DIM + B_ROPE_DIM) ** -0.5 * LOG2E
            q_ops = ((False, scale), (True, scale)) * (tn // B_QK_PAD)
            qcat = proj(cq, wuq.reshape(B_Q_RANK, -1).astype(BF16), rope_b, seq=seq,
                        tm=1024, tn=tn, kinds=((0, 1, q_ops),))
            kcat, v = mla_kv_up(ckv, kr, b_w_ukv[slot].astype(BF16), tm=1024, heads_per_tile=4)
            o = mla_attention(qcat.reshape(bsz, seq, -1), kcat.reshape(bsz, seq, -1),
                              v.reshape(bsz, seq, -1), tq=2048, chunk=256)
            w_o = b_w_o
        else:
            tn = C_HEADS * C_HEAD_DIM
            q_ops = ((False, C_HEAD_DIM ** -0.5 * LOG2E),) * C_HEADS
            qkv = norm_proj(xf, pre_mix_norm[i], c_w_qkv.astype(BF16), rope_a, seq=seq, tm=1024,
                            tn=tn, kinds=((0, 1, q_ops), (1, 3, (PLAIN,) * C_HEADS)), layer=slot)
            o = neighborhood_attention(qkv.reshape(bsz, seq, -1), _nb_bias_slabs(c_rel_bias[slot]))
            w_o = c_w_o
        xf = out_proj_residual(o.reshape(m, -1), w_o.astype(BF16), post_mix_norm[i], xf,
                               layer=slot, tm=1024)
        xf = ffn(xf, pre_ffn_norm[i], w_in, w_out, post_ffn_norm[i], layer=i, tm=1024, tf=512)
    return xf.reshape(bsz, seq, d)
```

```python
import functools

import jax
import jax.numpy as jnp
import numpy as np
from jax import lax
from jax.experimental import pallas as pl
from jax.experimental.pallas import tpu as pltpu

D_MODEL = 2048
DEPTH = 4
N_MIXERS = 3
GRID_W = 64
ROPE_THETA = 500000.0
NORM_EPS = 1e-6
D_FF = 5632

A_HEADS = 16
A_KV_HEADS = 4
A_GROUP = A_HEADS // A_KV_HEADS
A_HEAD_DIM = 128
A_ROT_DIM = 32
WINDOW = 128
A_BLOCK = 128

B_HEADS = 16
B_NOPE_DIM = 128
B_ROPE_DIM = 64
B_V_DIM = 128
B_Q_RANK = 512
B_KV_RANK = 512
B_QK_PAD = 256

C_HEADS = 16
C_HEAD_DIM = 128
NB_H = 8
NB_W = 16
C_QROWS = 8
C_KROWS = 16
C_HEADS_PER_STEP = 2

LANES = 128
NORM_ROW_PIECES = 2
OUT_ROW_PIECES = 2
FFN_PIECE = 256
PROJ_CHUNK = 512
ROPE_SWAP = LANES // 2
V7X_VMEM_BYTES = 64 * 1024 * 1024
V7X_VMEM_LIMIT_BYTES = V7X_VMEM_BYTES - 4 * 1024 * 1024

NEG_BIG = -1e30
LOG2E = float(np.log2(np.e))
BF16 = jnp.bfloat16
F32 = jnp.float32
PLAIN = (False, 1.0)


def _cparams(semantics):
    return pltpu.CompilerParams(dimension_semantics=semantics,
                                vmem_limit_bytes=V7X_VMEM_LIMIT_BYTES)


def _rms(xf, g):
    ms = jnp.mean(xf * xf, axis=-1, keepdims=True)
    return xf * lax.rsqrt(ms + NORM_EPS) * g


def _inv_rms(src_ref, stat_ref):
    xf = src_ref[...]
    stat_ref[...] = lax.rsqrt(jnp.mean(xf * xf, axis=-1, keepdims=True) + NORM_EPS)
    return stat_ref[...]


def _dot(a, b):
    return jnp.dot(a, b, preferred_element_type=F32)


def _dot_nt(a, b):
    return lax.dot_general(a, b, (((1,), (1,)), ((), ())), preferred_element_type=F32)


def _rope(y, tabs, shift, rows=slice(None)):
    r = y * tabs[0][rows, :] + pltpu.roll(y, shift, 1) * tabs[1][rows, :]
    if shift != ROPE_SWAP:
        r = r + pltpu.roll(y, LANES - shift, 1) * tabs[2][rows, :]
    return r


def _project_tile(lhs, w_ref, o_ref, tabs, shift, ops, rows=slice(None)):
    per_chunk = PROJ_CHUNK // LANES
    for c0 in range(0, len(ops), per_chunk):
        chunk = slice(c0 * LANES, (c0 + per_chunk) * LANES)
        y = _dot(lhs, w_ref[:, chunk])
        if all(op == PLAIN for op in ops[c0:c0 + per_chunk]):
            o_ref[rows, chunk] = y.astype(o_ref.dtype)
            continue
        for c in range(per_chunk):
            rope, scale = ops[c0 + c]
            yc = y[:, c * LANES:(c + 1) * LANES]
            if scale != 1.0:
                yc = yc * scale
            if rope:
                yc = _rope(yc, tabs, shift, rows)
            o_ref[rows, (c0 + c) * LANES:(c0 + c + 1) * LANES] = yc.astype(o_ref.dtype)


def _project(lhs, w_ref, o_ref, tabs, shift, kinds):
    if len(kinds) == 1:
        _project_tile(lhs, w_ref, o_ref, tabs, shift, kinds[0][2])
        return
    j = pl.program_id(1)
    for lo, hi, ops in kinds:
        @pl.when((j >= lo) & (j < hi))
        def _(ops=ops):
            _project_tile(lhs, w_ref, o_ref, tabs, shift, ops)


def _weight_spec(w, tn, layer):
    k, n = w.shape[-2:]
    lead = () if layer is None else (layer,)
    shape = (None,) * len(lead) + (k, tn)
    if n == tn:
        return pl.BlockSpec(shape, lambda i, j: lead + (0, 0), pipeline_mode=pl.Buffered(1))
    return pl.BlockSpec(shape, lambda i, j: lead + (0, j))


def _norm_proj_kernel(x_ref, g_ref, w_ref, *rest, kinds, shift, n_tiles):
    *tabs, o_ref, h_ref = rest
    j = pl.program_id(1)
    piece = x_ref.shape[0] // NORM_ROW_PIECES

    @pl.when(j == 0)
    def _():
        for r in range(NORM_ROW_PIECES):
            rows = slice(r * piece, (r + 1) * piece)
            h = _rms(x_ref[rows, :], g_ref[...]).astype(BF16)
            if n_tiles > 1:
                h_ref[rows, :] = h
            _project_tile(h, w_ref, o_ref, tabs, shift, kinds[0][2], rows)

    for lo, hi, ops in kinds:
        if max(lo, 1) < hi:
            @pl.when((j >= max(lo, 1)) & (j < hi))
            def _(ops=ops):
                _project_tile(h_ref[...], w_ref, o_ref, tabs, shift, ops)


def norm_proj(x, g, w, rope, *, seq, tm, tn, kinds, layer=None):
    m, d = x.shape
    n = w.shape[-1]
    shift, tabs = rope
    sblk = seq // tm
    tab_spec = pl.BlockSpec((tm, LANES), lambda i, j: (i % sblk, 0))
    return pl.pallas_call(
        functools.partial(_norm_proj_kernel, kinds=kinds, shift=shift, n_tiles=n // tn),
        grid=(m // tm, n // tn),
        in_specs=[pl.BlockSpec((tm, d), lambda i, j: (i, 0)),
                  pl.BlockSpec((1, d), lambda i, j: (0, 0)),
                  _weight_spec(w, tn, layer)] + [tab_spec] * len(tabs),
        out_specs=pl.BlockSpec((tm, tn), lambda i, j: (i, j)),
        out_shape=jax.ShapeDtypeStruct((m, n), BF16),
        scratch_shapes=[pltpu.VMEM((tm, d), BF16)],
        compiler_params=_cparams(("parallel", "arbitrary")),
        name="norm_proj",
    )(x, g.reshape(1, d), w, *tabs)


def _proj_kernel(a_ref, w_ref, *rest, kinds, shift):
    *tabs, o_ref = rest
    _project(a_ref[...], w_ref, o_ref, tabs, shift, kinds)


def proj(a, w, rope, *, seq, tm, tn, kinds, layer=None):
    m, k = a.shape
    n = w.shape[-1]
    shift, tabs = rope
    sblk = seq // tm
    tab_spec = pl.BlockSpec((tm, LANES), lambda i, j: (i % sblk, 0))
    return pl.pallas_call(
        functools.partial(_proj_kernel, kinds=kinds, shift=shift),
        grid=(m // tm, n // tn),
        in_specs=[pl.BlockSpec((tm, k), lambda i, j: (i, 0)),
                  _weight_spec(w, tn, layer)] + [tab_spec] * len(tabs),
        out_specs=pl.BlockSpec((tm, tn), lambda i, j: (i, j)),
        out_shape=jax.ShapeDtypeStruct((m, n), BF16),
        compiler_params=_cparams(("parallel", "arbitrary")),
        name="proj",
    )(a, w, *tabs)


def _out_proj_kernel(a_ref, w_ref, g_ref, x_ref, o_ref):
    piece = a_ref.shape[0] // OUT_ROW_PIECES
    for r in range(OUT_ROW_PIECES):
        rows = slice(r * piece, (r + 1) * piece)
        a = a_ref[rows, :]
        m = jnp.concatenate([_dot(a, w_ref[:, c:c + PROJ_CHUNK])
                             for c in range(0, w_ref.shape[1], PROJ_CHUNK)], axis=1)
        o_ref[rows, :] = x_ref[rows, :] + _rms(m, g_ref[...])


def out_proj_residual(a, w, g, x, *, layer, tm):
    m, k = a.shape
    d = w.shape[-1]
    return pl.pallas_call(
        _out_proj_kernel,
        grid=(m // tm,),
        in_specs=[pl.BlockSpec((tm, k), lambda i: (i, 0)),
                  pl.BlockSpec((None, k, d), lambda i: (layer, 0, 0), pipeline_mode=pl.Buffered(1)),
                  pl.BlockSpec((1, d), lambda i: (0, 0)),
                  pl.BlockSpec((tm, d), lambda i: (i, 0))],
        out_specs=pl.BlockSpec((tm, d), lambda i: (i, 0)),
        out_shape=jax.ShapeDtypeStruct((m, d), F32),
        compiler_params=_cparams(("parallel",)),
        name="out_proj_residual",
    )(a, w, g.reshape(1, d), x)


def _ffn_kernel(x_ref, gpre_ref, wg_ref, wu_ref, wo_ref, gpost_ref, o_ref, h_ref, stat_ref,
                act_ref):
    j = pl.program_id(1)

    @pl.when(j == 0)
    def _():
        h_ref[...] = _rms(x_ref[...], gpre_ref[...]).astype(BF16)
        o_ref[...] = jnp.zeros_like(o_ref)

    h = h_ref[...]
    for c in range(0, act_ref.shape[1], FFN_PIECE):
        cols = slice(c, c + FFN_PIECE)
        gate = _dot(h, wg_ref[:, cols])
        up = _dot(h, wu_ref[:, cols])
        act_ref[:, cols] = (gate / (1.0 + jnp.exp(-gate)) * up).astype(BF16)
    o_ref[...] += _dot(act_ref[...], wo_ref[...])

    @pl.when(j == pl.num_programs(1) - 1)
    def _():
        o_ref[...] = x_ref[...] + o_ref[...] * _inv_rms(o_ref, stat_ref) * gpost_ref[...]


def ffn(x, gpre, w_in, w_out, gpost, *, layer, tm, tf):
    m, d = x.shape
    dff = w_out.shape[1]
    nf = dff // tf
    return pl.pallas_call(
        _ffn_kernel,
        grid=(m // tm, nf),
        in_specs=[pl.BlockSpec((tm, d), lambda i, j: (i, 0)),
                  pl.BlockSpec((1, d), lambda i, j: (0, 0)),
                  pl.BlockSpec((None, d, tf), lambda i, j: (layer, 0, j)),
                  pl.BlockSpec((None, d, tf), lambda i, j: (layer, 0, nf + j)),
                  pl.BlockSpec((None, tf, d), lambda i, j: (layer, j, 0)),
                  pl.BlockSpec((1, d), lambda i, j: (0, 0))],
        out_specs=pl.BlockSpec((tm, d), lambda i, j: (i, 0)),
        out_shape=jax.ShapeDtypeStruct((m, d), F32),
        scratch_shapes=[pltpu.VMEM((tm, d), BF16), pltpu.VMEM((tm, 1), F32),
                        pltpu.VMEM((tm, tf), BF16)],
        compiler_params=_cparams(("parallel", "arbitrary")),
        name="ffn",
    )(x, gpre.reshape(1, d), w_in, w_in, w_out, gpost.reshape(1, d))


def _win_attn_kernel(sink_ref, q_ref, k_ref, v_ref, o_ref, *, seq, nblk):
    kwin = 3 * A_BLOCK
    cols = A_GROUP * A_BLOCK
    for jb in range(nblk):
        blk = pl.program_id(1) * nblk + jb
        start = pl.multiple_of(jnp.clip((blk - 1) * A_BLOCK, 0, seq - kwin), A_BLOCK)
        kpos = start + lax.broadcasted_iota(jnp.int32, (kwin, cols), 0)
        qpos = blk * A_BLOCK + (lax.broadcasted_iota(jnp.int32, (kwin, cols), 1) & (A_BLOCK - 1))
        mask = jnp.where(jnp.abs(kpos - qpos) <= WINDOW, 0.0, NEG_BIG)
        qrows = slice(jb * A_BLOCK, (jb + 1) * A_BLOCK)
        for kv in range(A_KV_HEADS):
            heads = range(kv * A_GROUP, (kv + 1) * A_GROUP)
            kvcols = slice(kv * A_HEAD_DIM, (kv + 1) * A_HEAD_DIM)
            q4 = jnp.concatenate(
                [q_ref[0, qrows, h * A_HEAD_DIM:(h + 1) * A_HEAD_DIM] for h in heads], axis=0)
            kw = k_ref[0, pl.ds(start, kwin), kvcols]
            vw = v_ref[0, pl.ds(start, kwin), kvcols]
            st = _dot_nt(kw, q4) + mask
            sink = jnp.concatenate(
                [jnp.full((1, A_BLOCK), sink_ref[h] * LOG2E, F32) for h in heads], axis=1)
            mx = jnp.maximum(jnp.max(st, axis=0, keepdims=True), sink)
            pt = jnp.exp2(st - mx)
            denom = jnp.sum(pt, axis=0, keepdims=True) + jnp.exp2(sink - mx)
            ot = lax.dot_general(vw, pt.astype(BF16), (((0,), (0,)), ((), ())),
                                 preferred_element_type=F32) / denom
            for gi, h in enumerate(heads):
                o_ref[0, qrows, h * A_HEAD_DIM:(h + 1) * A_HEAD_DIM] = (
                    ot[:, gi * A_BLOCK:(gi + 1) * A_BLOCK].T.astype(o_ref.dtype))


def window_attention(qkv, sinks, *, nblk):
    b, s, _ = qkv.shape
    dq = A_HEADS * A_HEAD_DIM
    dkv = A_KV_HEADS * A_HEAD_DIM
    tq = nblk * A_BLOCK
    return pl.pallas_call(
        functools.partial(_win_attn_kernel, seq=s, nblk=nblk),
        grid=(b, s // tq),
        in_specs=[pl.BlockSpec(memory_space=pltpu.SMEM),
                  pl.BlockSpec((1, tq, dq), lambda bi, i: (bi, i, 0)),
                  pl.BlockSpec((1, s, dkv), lambda bi, i: (bi, 0, dq // dkv)),
                  pl.BlockSpec((1, s, dkv), lambda bi, i: (bi, 0, dq // dkv + 1))],
        out_specs=pl.BlockSpec((1, tq, dq), lambda bi, i: (bi, i, 0)),
        out_shape=jax.ShapeDtypeStruct((b, s, dq), BF16),
        compiler_params=_cparams(("parallel", "arbitrary")),
        name="window_attention",
    )(sinks, qkv, qkv, qkv)


def _mla_down_kernel(x_ref, g_ref, wq_ref, wkv_ref, wkr_ref, qn_ref, kvn_ref, *rest, shift):
    *tabs, cq_ref, ckv_ref, kr_ref = rest
    h = _rms(x_ref[...], g_ref[...]).astype(BF16)
    cq_ref[...] = _rms(_dot(h, wq_ref[...]), qn_ref[...]).astype(BF16)
    ckv_ref[...] = _rms(_dot(h, wkv_ref[...]), kvn_ref[...]).astype(BF16)
    kr_ref[...] = _rope(_dot(h, wkr_ref[...]), tabs, shift).astype(BF16)


def mla_down(x, g, wq, wkv, wkr, qn, kvn, rope, *, seq, tm):
    m, d = x.shape
    shift, tabs = rope
    sblk = seq // tm
    tab_spec = pl.BlockSpec((tm, LANES), lambda i: (i % sblk, 0))
    full = lambda shape: pl.BlockSpec(shape, lambda i: (0, 0))
    widths = (B_Q_RANK, B_KV_RANK, LANES)
    return pl.pallas_call(
        functools.partial(_mla_down_kernel, shift=shift),
        grid=(m // tm,),
        in_specs=[pl.BlockSpec((tm, d), lambda i: (i, 0)), full((1, d)),
                  full((d, B_Q_RANK)), full((d, B_KV_RANK)), full((d, LANES)),
                  full((1, B_Q_RANK)), full((1, B_KV_RANK))] + [tab_spec] * len(tabs),
        out_specs=[pl.BlockSpec((tm, w), lambda i: (i, 0)) for w in widths],
        out_shape=[jax.ShapeDtypeStruct((m, w), BF16) for w in widths],
        compiler_params=_cparams(("parallel",)),
        name="mla_down",
    )(x, g.reshape(1, d), wq, wkv, wkr, qn.reshape(1, -1), kvn.reshape(1, -1), *tabs)


def _mla_kv_up_kernel(c_ref, kr_ref, w_ref, k_ref, v_ref):
    c = c_ref[...]
    kr = kr_ref[...]
    head_w = B_NOPE_DIM + B_V_DIM
    per_piece = PROJ_CHUNK // head_w
    for p0 in range(0, w_ref.shape[1] // head_w, per_piece):
        y = _dot(c, w_ref[:, p0 * head_w:(p0 + per_piece) * head_w])
        for hh in range(per_piece):
            head = p0 + hh
            k0 = head * B_QK_PAD
            k_ref[:, k0:k0 + B_NOPE_DIM] = y[:, hh * head_w:hh * head_w + B_NOPE_DIM].astype(BF16)
            k_ref[:, k0 + B_NOPE_DIM:k0 + B_QK_PAD] = kr
            v_ref[:, head * B_V_DIM:(head + 1) * B_V_DIM] = (
                y[:, hh * head_w + B_NOPE_DIM:(hh + 1) * head_w].astype(BF16))


def mla_kv_up(ckv, kr, w, *, tm, heads_per_tile):
    m, k = ckv.shape
    tn = heads_per_tile * (B_NOPE_DIM + B_V_DIM)
    return pl.pallas_call(
        _mla_kv_up_kernel,
        grid=(m // tm, B_HEADS // heads_per_tile),
        in_specs=[pl.BlockSpec((tm, k), lambda i, j: (i, 0)),
                  pl.BlockSpec((tm, LANES), lambda i, j: (i, 0)),
                  pl.BlockSpec((k, tn), lambda i, j: (0, j))],
        out_specs=[pl.BlockSpec((tm, heads_per_tile * B_QK_PAD), lambda i, j: (i, j)),
                   pl.BlockSpec((tm, heads_per_tile * B_V_DIM), lambda i, j: (i, j))],
        out_shape=[jax.ShapeDtypeStruct((m, B_HEADS * B_QK_PAD), BF16),
                   jax.ShapeDtypeStruct((m, B_HEADS * B_V_DIM), BF16)],
        compiler_params=_cparams(("parallel", "arbitrary")),
        name="mla_kv_up",
    )(ckv, kr, w)


def _mla_attn_kernel(q_ref, k_ref, v_ref, o_ref, *, chunk):
    v = v_ref[0]
    v_ones = jnp.concatenate([v, jnp.ones_like(v)], axis=1)
    dv = v.shape[1]
    for c in range(q_ref.shape[1] // chunk):
        rows = slice(c * chunk, (c + 1) * chunk)
        s = _dot_nt(q_ref[0, rows, :], k_ref[0])
        p = jnp.exp2(s - jnp.max(s, axis=-1, keepdims=True))
        o = _dot(p.astype(BF16), v_ones)
        o_ref[0, rows, :] = (o[:, :dv] / o[:, dv:]).astype(o_ref.dtype)


def mla_attention(qcat, kcat, v, *, tq, chunk):
    b, s, _ = qcat.shape
    return pl.pallas_call(
        functools.partial(_mla_attn_kernel, chunk=chunk),
        grid=(b, B_HEADS, s // tq),
        in_specs=[pl.BlockSpec((1, tq, B_QK_PAD), lambda bi, h, qi: (bi, qi, h)),
                  pl.BlockSpec((1, s, B_QK_PAD), lambda bi, h, qi: (bi, 0, h)),
                  pl.BlockSpec((1, s, B_V_DIM), lambda bi, h, qi: (bi, 0, h))],
        out_specs=pl.BlockSpec((1, tq, B_V_DIM), lambda bi, h, qi: (bi, qi, h)),
        out_shape=jax.ShapeDtypeStruct((b, s, B_HEADS * B_V_DIM), BF16),
        compiler_params=_cparams(("parallel", "parallel", "arbitrary")),
        name="mla_attention",
    )(qcat, kcat, v)


N_DR = 2 * NB_H - 1
SLAB_LOW = N_DR - 1
SLAB_HIGH = SLAB_LOW + N_DR
SLAB_NONE = SLAB_HIGH + N_DR


def _nb_slab_plan(rows):
    plan = []
    for rb in range(rows // C_QROWS):
        w0 = min(max(rb * C_QROWS - NB_H // 2, 0), rows - C_KROWS)
        block = []
        for i in range(C_QROWS):
            r = rb * C_QROWS + i
            rs = min(max(r - NB_H // 2, 0), rows - NB_H)
            ids = []
            for pair in range(C_KROWS // 2):
                kr = w0 + 2 * pair
                ok0, ok1 = rs <= kr < rs + NB_H, rs <= kr + 1 < rs + NB_H
                dr = kr - r + NB_H - 1
                ids.append(dr if ok0 and ok1 else SLAB_LOW + dr if ok0
                           else SLAB_HIGH + dr + 1 if ok1 else SLAB_NONE)
            block.append(tuple(ids))
        plan.append(tuple(block))
    return tuple(plan)


def _nb_attn_kernel(q_ref, k_ref, v_ref, t_ref, o_ref, *, plan, rows):
    tq = C_QROWS * GRID_W
    nk = C_KROWS * GRID_W
    for hh in range(C_HEADS_PER_STEP):
        cols = slice(hh * C_HEAD_DIM, (hh + 1) * C_HEAD_DIM)
        biases = {}
        for rb, block in enumerate(plan):
            if block not in biases:
                biases[block] = jnp.concatenate(
                    [jnp.concatenate([t_ref[hh, sid] for sid in ids], axis=1) for ids in block],
                    axis=0)
            start = min(max(rb * C_QROWS - NB_H // 2, 0), rows - C_KROWS) * GRID_W
            qrows = slice(rb * tq, (rb + 1) * tq)
            kw = k_ref[0, start:start + nk, cols]
            vw = v_ref[0, start:start + nk, cols]
            s = _dot_nt(q_ref[0, qrows, cols], kw) + biases[block]
            p = jnp.exp2(s - jnp.max(s, axis=-1, keepdims=True))
            o = _dot(p.astype(BF16), jnp.concatenate([vw, jnp.ones_like(vw)], axis=1))
            o_ref[0, qrows, cols] = (o[:, :C_HEAD_DIM] / o[:, C_HEAD_DIM:]).astype(o_ref.dtype)


def neighborhood_attention(qkv, slabs):
    b, s, _ = qkv.shape
    rows = s // GRID_W
    width = C_HEADS_PER_STEP * C_HEAD_DIM
    nhb = C_HEADS // C_HEADS_PER_STEP
    head_spec = lambda blk0: pl.BlockSpec((1, s, width), lambda h, bi: (bi, 0, blk0 + h))
    return pl.pallas_call(
        functools.partial(_nb_attn_kernel, plan=_nb_slab_plan(rows), rows=rows),
        grid=(nhb, b),
        in_specs=[head_spec(0), head_spec(nhb), head_spec(2 * nhb),
                  pl.BlockSpec((C_HEADS_PER_STEP,) + slabs.shape[1:], lambda h, bi: (h, 0, 0, 0))],
        out_specs=head_spec(0),
        out_shape=jax.ShapeDtypeStruct((b, s, C_HEADS * C_HEAD_DIM), BF16),
        compiler_params=_cparams(("parallel", "arbitrary")),
        name="neighborhood_attention",
    )(qkv, qkv, qkv, slabs)


def _nb_bias_slabs(rel_bias):
    c = np.arange(GRID_W)[:, None]
    kc = np.arange(GRID_W)[None, :]
    cs = np.clip(c - NB_W // 2, 0, GRID_W - NB_W)
    col_ok = (kc >= cs) & (kc < cs + NB_W)
    dc = np.clip(kc - c + NB_W - 1, 0, 2 * NB_W - 2)
    bias = jnp.where(col_ok[None, None], rel_bias.astype(F32)[:, :, dc] * LOG2E, NEG_BIG)
    masked = jnp.full_like(bias, NEG_BIG)
    return jnp.concatenate([jnp.concatenate([bias[:, :-1], bias[:, 1:]], axis=-1),
                            jnp.concatenate([bias, masked], axis=-1),
                            jnp.concatenate([masked, bias], axis=-1),
                            jnp.concatenate([masked[:, :1], masked[:, :1]], axis=-1)], axis=1)


def _rope_angles(seq, dim):
    pos = jnp.arange(seq, dtype=F32)
    inv = ROPE_THETA ** (-jnp.arange(0, dim, 2, dtype=F32) / dim)
    ang = pos[:, None] * inv[None, :]
    return jnp.cos(ang), jnp.sin(ang)


def _rope_adjacent(seq, dim):
    half = dim // 2
    c, sn = _rope_angles(seq, dim)
    cos = jnp.ones((seq, LANES), F32).at[:, :dim].set(jnp.concatenate([c, c], axis=1))
    up = jnp.zeros((seq, LANES), F32).at[:, half:dim].set(sn)
    down = jnp.zeros((seq, LANES), F32).at[:, :half].set(-sn)
    return half, (cos, up, down)


def _rope_swapped(seq, dim):
    half = dim // 2
    c, sn = _rope_angles(seq, dim)
    cos = jnp.ones((seq, LANES), F32).at[:, :half].set(c).at[:, ROPE_SWAP:ROPE_SWAP + half].set(c)
    sin = jnp.zeros((seq, LANES), F32).at[:, :half].set(-sn).at[:, ROPE_SWAP:ROPE_SWAP + half].set(sn)
    return ROPE_SWAP, (cos, sin)


def _pair_split(w, dim):
    half = dim // 2
    gap = ROPE_SWAP - half
    return jnp.concatenate([w[..., :half], w[..., dim:dim + gap], w[..., half:dim],
                            w[..., dim + gap:]], axis=-1)


def kernel(x, pre_mix_norm, post_mix_norm, pre_ffn_norm, post_ffn_norm, a_w_qkv, a_sinks, a_w_o,
           b_w_down, b_q_norm, b_w_uq, b_kv_norm, b_w_ukv, b_w_o, c_w_qkv, c_rel_bias, c_w_o,
           ffn_w_in, ffn_w_out):
    bsz, seq, d = x.shape
    m = bsz * seq
    rope_a = _rope_adjacent(seq, A_ROT_DIM)
    rope_b = _rope_swapped(seq, B_ROPE_DIM)
    w_in, w_out = ffn_w_in.astype(BF16), ffn_w_out.astype(BF16)
    xf = x.reshape(m, d)
    for i in range(DEPTH):
        kind, slot = i % N_MIXERS, i // N_MIXERS
        if kind == 0:
            ops = (((True, A_HEAD_DIM ** -0.5 * LOG2E),) * A_HEADS + ((True, 1.0),) * A_KV_HEADS
                   + (PLAIN,) * A_KV_HEADS)
            qkv = norm_proj(xf, pre_mix_norm[i], a_w_qkv.astype(BF16), rope_a, seq=seq, tm=1024,
                            tn=len(ops) * LANES, kinds=((0, 1, ops),), layer=slot)
            o = window_attention(qkv.reshape(bsz, seq, -1), a_sinks[slot].astype(F32), nblk=8)
            w_o = a_w_o
        elif kind == 1:
            wd = b_w_down[slot]
            wkr = jnp.pad(wd[:, B_Q_RANK + B_KV_RANK:], ((0, 0), (0, LANES - B_ROPE_DIM)))
            cq, ckv, kr = mla_down(xf, pre_mix_norm[i], wd[:, :B_Q_RANK].astype(BF16),
                                   wd[:, B_Q_RANK:B_Q_RANK + B_KV_RANK].astype(BF16),
                                   _pair_split(wkr, B_ROPE_DIM).astype(BF16),
                                   b_q_norm[slot], b_kv_norm[slot], rope_b, seq=seq, tm=1024)
            tn = 2048
            wuq = b_w_uq[slot].reshape(B_Q_RANK, B_HEADS, B_NOPE_DIM + B_ROPE_DIM)
            wuq_r = jnp.pad(wuq[:, :, B_NOPE_DIM:], ((0, 0), (0, 0), (0, LANES - B_ROPE_DIM)))
            wuq = jnp.concatenate([wuq[:, :, :B_NOPE_DIM], _pair_split(wuq_r, B_ROPE_DIM)], axis=2)
            scale = (B_NOPE_DIM + B_ROPE_DIM) ** -0.5 * LOG2E
            q_ops = ((False, scale), (True, scale)) * (tn // B_QK_PAD)
            qcat = proj(cq, wuq.reshape(B_Q_RANK, -1).astype(BF16), rope_b, seq=seq,
                        tm=1024, tn=tn, kinds=((0, 1, q_ops),))
            kcat, v = mla_kv_up(ckv, kr, b_w_ukv[slot].astype(BF16), tm=1024, heads_per_tile=4)
            o = mla_attention(qcat.reshape(bsz, seq, -1), kcat.reshape(bsz, seq, -1),
                              v.reshape(bsz, seq, -1), tq=4096, chunk=256)
            w_o = b_w_o
        else:
            tn = C_HEADS * C_HEAD_DIM
            q_ops = ((False, C_HEAD_DIM ** -0.5 * LOG2E),) * C_HEADS
            qkv = norm_proj(xf, pre_mix_norm[i], c_w_qkv.astype(BF16), rope_a, seq=seq, tm=1024,
                            tn=tn, kinds=((0, 1, q_ops), (1, 3, (PLAIN,) * C_HEADS)), layer=slot)
            o = neighborhood_attention(qkv.reshape(bsz, seq, -1), _nb_bias_slabs(c_rel_bias[slot]))
            w_o = c_w_o
        xf = out_proj_residual(o.reshape(m, -1), w_o.astype(BF16), post_mix_norm[i], xf,
                               layer=slot, tm=1024)
        xf = ffn(xf, pre_ffn_norm[i], w_in, w_out, post_ffn_norm[i], layer=i, tm=1024, tf=512)
    return xf.reshape(bsz, seq, d)
```

```python
import functools

import jax
import jax.numpy as jnp
import numpy as np
from jax import lax
from jax.experimental import pallas as pl
from jax.experimental.pallas import tpu as pltpu

D_MODEL = 2048
DEPTH = 4
N_MIXERS = 3
GRID_W = 64
ROPE_THETA = 500000.0
NORM_EPS = 1e-6
D_FF = 5632

A_HEADS = 16
A_KV_HEADS = 4
A_GROUP = A_HEADS // A_KV_HEADS
A_HEAD_DIM = 128
A_ROT_DIM = 32
WINDOW = 128
A_BLOCK = 128

B_HEADS = 16
B_NOPE_DIM = 128
B_ROPE_DIM = 64
B_V_DIM = 128
B_Q_RANK = 512
B_KV_RANK = 512
B_QK_PAD = 256

C_HEADS = 16
C_HEAD_DIM = 128
NB_H = 8
NB_W = 16

LANES = 128
MXU_WIDTH = 256
V7X_VMEM_BYTES = 64 * 1024 * 1024
V7X_VMEM_RESERVE_BYTES = 4 * 1024 * 1024
V7X_VMEM_LIMIT_BYTES = V7X_VMEM_BYTES - V7X_VMEM_RESERVE_BYTES

TM = 1024
PROJ_CHUNK = 2 * MXU_WIDTH
FFN_PIECE = MXU_WIDTH
FFN_TF = 2 * FFN_PIECE
NORM_ROW_PIECES = 2
OUT_ROW_PIECES = 2
MLA_TN = 2048
MLA_KV_HEADS_PER_TILE = 4
MLA_CHUNK = 256
SINK_ROWS = 16
A_BLOCKS_PER_STEP = 8
C_QROWS = 8
C_KROWS = 16
C_HEADS_PER_STEP = 2
ROPE_SWAP = LANES // 2

NEG_BIG = -1e30
LOG2E = float(np.log2(np.e))
BF16 = jnp.bfloat16
F32 = jnp.float32
PLAIN = (False, 1.0)


def _cparams(semantics):
    return pltpu.CompilerParams(dimension_semantics=semantics,
                                vmem_limit_bytes=V7X_VMEM_LIMIT_BYTES)


def _rms(xf, g):
    ms = jnp.mean(xf * xf, axis=-1, keepdims=True)
    return xf * lax.rsqrt(ms + NORM_EPS) * g


def _inv_rms(src_ref, stat_ref):
    xf = src_ref[...]
    stat_ref[...] = lax.rsqrt(jnp.mean(xf * xf, axis=-1, keepdims=True) + NORM_EPS)
    return stat_ref[...]


def _dot(a, b):
    return jnp.dot(a, b, preferred_element_type=F32)


def _dot_nt(a, b):
    return lax.dot_general(a, b, (((1,), (1,)), ((), ())), preferred_element_type=F32)


def _rope(y, tabs, shift, rows=slice(None)):
    r = y * tabs[0][rows, :] + pltpu.roll(y, shift, 1) * tabs[1][rows, :]
    if shift != ROPE_SWAP:
        r = r + pltpu.roll(y, LANES - shift, 1) * tabs[2][rows, :]
    return r


def _project_tile(lhs, w_ref, o_ref, tabs, shift, ops, rows=slice(None)):
    per_chunk = PROJ_CHUNK // LANES
    for c0 in range(0, len(ops), per_chunk):
        chunk = slice(c0 * LANES, (c0 + per_chunk) * LANES)
        y = _dot(lhs, w_ref[:, chunk])
        if all(op == PLAIN for op in ops[c0:c0 + per_chunk]):
            o_ref[rows, chunk] = y.astype(o_ref.dtype)
            continue
        for c in range(per_chunk):
            rope, scale = ops[c0 + c]
            yc = y[:, c * LANES:(c + 1) * LANES]
            if scale != 1.0:
                yc = yc * scale
            if rope:
                yc = _rope(yc, tabs, shift, rows)
            o_ref[rows, (c0 + c) * LANES:(c0 + c + 1) * LANES] = yc.astype(o_ref.dtype)


def _project(lhs, w_ref, o_ref, tabs, shift, kinds):
    if len(kinds) == 1:
        _project_tile(lhs, w_ref, o_ref, tabs, shift, kinds[0][2])
        return
    j = pl.program_id(1)
    for lo, hi, ops in kinds:
        @pl.when((j >= lo) & (j < hi))
        def _(ops=ops):
            _project_tile(lhs, w_ref, o_ref, tabs, shift, ops)


def _weight_spec(w, tn, layer):
    k, n = w.shape[-2:]
    lead = () if layer is None else (layer,)
    shape = (None,) * len(lead) + (k, tn)
    if n == tn:
        return pl.BlockSpec(shape, lambda i, j: lead + (0, 0), pipeline_mode=pl.Buffered(1))
    return pl.BlockSpec(shape, lambda i, j: lead + (0, j))


def _norm_proj_kernel(x_ref, g_ref, w_ref, *rest, kinds, shift, n_tiles):
    *tabs, o_ref, h_ref = rest
    j = pl.program_id(1)
    piece = x_ref.shape[0] // NORM_ROW_PIECES

    @pl.when(j == 0)
    def _():
        for r in range(NORM_ROW_PIECES):
            rows = slice(r * piece, (r + 1) * piece)
            h = _rms(x_ref[rows, :], g_ref[...]).astype(BF16)
            if n_tiles > 1:
                h_ref[rows, :] = h
            _project_tile(h, w_ref, o_ref, tabs, shift, kinds[0][2], rows)

    for lo, hi, ops in kinds:
        if max(lo, 1) < hi:
            @pl.when((j >= max(lo, 1)) & (j < hi))
            def _(ops=ops):
                _project_tile(h_ref[...], w_ref, o_ref, tabs, shift, ops)


def norm_proj(x, g, w, rope, *, seq, tm, tn, kinds, layer=None):
    m, d = x.shape
    n = w.shape[-1]
    shift, tabs = rope
    sblk = seq // tm
    tab_spec = pl.BlockSpec((tm, LANES), lambda i, j: (i % sblk, 0))
    return pl.pallas_call(
        functools.partial(_norm_proj_kernel, kinds=kinds, shift=shift, n_tiles=n // tn),
        grid=(m // tm, n // tn),
        in_specs=[pl.BlockSpec((tm, d), lambda i, j: (i, 0)),
                  pl.BlockSpec((1, d), lambda i, j: (0, 0)),
                  _weight_spec(w, tn, layer)] + [tab_spec] * len(tabs),
        out_specs=pl.BlockSpec((tm, tn), lambda i, j: (i, j)),
        out_shape=jax.ShapeDtypeStruct((m, n), BF16),
        scratch_shapes=[pltpu.VMEM((tm, d), BF16)],
        compiler_params=_cparams(("parallel", "arbitrary")),
        name="norm_proj",
    )(x, g.reshape(1, d), w, *tabs)


def _proj_kernel(a_ref, w_ref, *rest, kinds, shift):
    *tabs, o_ref = rest
    _project(a_ref[...], w_ref, o_ref, tabs, shift, kinds)


def proj(a, w, rope, *, seq, tm, tn, kinds, layer=None):
    m, k = a.shape
    n = w.shape[-1]
    shift, tabs = rope
    sblk = seq // tm
    tab_spec = pl.BlockSpec((tm, LANES), lambda i, j: (i % sblk, 0))
    return pl.pallas_call(
        functools.partial(_proj_kernel, kinds=kinds, shift=shift),
        grid=(m // tm, n // tn),
        in_specs=[pl.BlockSpec((tm, k), lambda i, j: (i, 0)),
                  _weight_spec(w, tn, layer)] + [tab_spec] * len(tabs),
        out_specs=pl.BlockSpec((tm, tn), lambda i, j: (i, j)),
        out_shape=jax.ShapeDtypeStruct((m, n), BF16),
        compiler_params=_cparams(("parallel", "arbitrary")),
        name="proj",
    )(a, w, *tabs)


def _out_proj_kernel(a_ref, w_ref, g_ref, x_ref, o_ref):
    piece = a_ref.shape[0] // OUT_ROW_PIECES
    for r in range(OUT_ROW_PIECES):
        rows = slice(r * piece, (r + 1) * piece)
        a = a_ref[rows, :]
        m = jnp.concatenate([_dot(a, w_ref[:, c:c + PROJ_CHUNK])
                             for c in range(0, w_ref.shape[1], PROJ_CHUNK)], axis=1)
        o_ref[rows, :] = x_ref[rows, :] + _rms(m, g_ref[...])


def out_proj_residual(a, w, g, x, *, layer, tm):
    m, k = a.shape
    d = w.shape[-1]
    return pl.pallas_call(
        _out_proj_kernel,
        grid=(m // tm,),
        in_specs=[pl.BlockSpec((tm, k), lambda i: (i, 0)),
                  pl.BlockSpec((None, k, d), lambda i: (layer, 0, 0), pipeline_mode=pl.Buffered(1)),
                  pl.BlockSpec((1, d), lambda i: (0, 0)),
                  pl.BlockSpec((tm, d), lambda i: (i, 0))],
        out_specs=pl.BlockSpec((tm, d), lambda i: (i, 0)),
        out_shape=jax.ShapeDtypeStruct((m, d), F32),
        compiler_params=_cparams(("parallel",)),
        name="out_proj_residual",
    )(a, w, g.reshape(1, d), x)


def _ffn_kernel(x_ref, gpre_ref, wg_ref, wu_ref, wo_ref, gpost_ref, o_ref, h_ref, stat_ref,
                act_ref):
    j = pl.program_id(1)

    @pl.when(j == 0)
    def _():
        h_ref[...] = _rms(x_ref[...], gpre_ref[...]).astype(BF16)
        o_ref[...] = jnp.zeros_like(o_ref)

    h = h_ref[...]
    for c in range(0, act_ref.shape[1], FFN_PIECE):
        cols = slice(c, c + FFN_PIECE)
        gate = _dot(h, wg_ref[:, cols])
        up = _dot(h, wu_ref[:, cols])
        act_ref[:, cols] = (gate / (1.0 + jnp.exp(-gate)) * up).astype(BF16)
    o_ref[...] += _dot(act_ref[...], wo_ref[...])

    @pl.when(j == pl.num_programs(1) - 1)
    def _():
        o_ref[...] = x_ref[...] + o_ref[...] * _inv_rms(o_ref, stat_ref) * gpost_ref[...]


def ffn(x, gpre, w_in, w_out, gpost, *, layer, tm, tf):
    m, d = x.shape
    dff = w_out.shape[1]
    nf = dff // tf
    return pl.pallas_call(
        _ffn_kernel,
        grid=(m // tm, nf),
        in_specs=[pl.BlockSpec((tm, d), lambda i, j: (i, 0)),
                  pl.BlockSpec((1, d), lambda i, j: (0, 0)),
                  pl.BlockSpec((None, d, tf), lambda i, j: (layer, 0, j)),
                  pl.BlockSpec((None, d, tf), lambda i, j: (layer, 0, nf + j)),
                  pl.BlockSpec((None, tf, d), lambda i, j: (layer, j, 0)),
                  pl.BlockSpec((1, d), lambda i, j: (0, 0))],
        out_specs=pl.BlockSpec((tm, d), lambda i, j: (i, 0)),
        out_shape=jax.ShapeDtypeStruct((m, d), F32),
        scratch_shapes=[pltpu.VMEM((tm, d), BF16), pltpu.VMEM((tm, 1), F32),
                        pltpu.VMEM((tm, tf), BF16)],
        compiler_params=_cparams(("parallel", "arbitrary")),
        name="ffn",
    )(x, gpre.reshape(1, d), w_in, w_in, w_out, gpost.reshape(1, d))


def _win_attn_kernel(sink_ref, q_ref, k_ref, v_ref, o_ref, *, seq, nblk):
    kwin = 3 * A_BLOCK
    cols = A_GROUP * A_BLOCK
    sink_values = jnp.concatenate([jnp.zeros((SINK_ROWS, A_HEAD_DIM), BF16),
                                   jnp.full((SINK_ROWS, A_HEAD_DIM), 1.0 / SINK_ROWS, BF16)], axis=1)
    for jb in range(nblk):
        blk = pl.program_id(1) * nblk + jb
        start = pl.multiple_of(jnp.clip((blk - 1) * A_BLOCK, 0, seq - kwin), A_BLOCK)
        kpos = start + lax.broadcasted_iota(jnp.int32, (kwin, cols), 0)
        qpos = blk * A_BLOCK + (lax.broadcasted_iota(jnp.int32, (kwin, cols), 1) & (A_BLOCK - 1))
        mask = jnp.where(jnp.abs(kpos - qpos) <= WINDOW, 0.0, NEG_BIG)
        qrows = slice(jb * A_BLOCK, (jb + 1) * A_BLOCK)
        for kv in range(A_KV_HEADS):
            heads = range(kv * A_GROUP, (kv + 1) * A_GROUP)
            kvcols = slice(kv * A_HEAD_DIM, (kv + 1) * A_HEAD_DIM)
            q4 = jnp.concatenate(
                [q_ref[0, qrows, h * A_HEAD_DIM:(h + 1) * A_HEAD_DIM] for h in heads], axis=0)
            kw = k_ref[0, pl.ds(start, kwin), kvcols]
            vw = v_ref[0, pl.ds(start, kwin), kvcols]
            st = _dot_nt(kw, q4) + mask
            sink = jnp.concatenate(
                [jnp.full((1, A_BLOCK), sink_ref[h] * LOG2E, F32) for h in heads], axis=1)
            mx = jnp.maximum(jnp.max(st, axis=0, keepdims=True), sink)
            sink_rows = jnp.broadcast_to(jnp.exp2(sink - mx), (SINK_ROWS, cols))
            pt = jnp.concatenate([jnp.exp2(st - mx), sink_rows], axis=0).astype(BF16)
            v_ext = jnp.concatenate(
                [jnp.concatenate([vw, jnp.ones_like(vw)], axis=1), sink_values], axis=0)
            o = lax.dot_general(pt, v_ext, (((0,), (0,)), ((), ())), preferred_element_type=F32)
            o = o[:, :A_HEAD_DIM] / o[:, A_HEAD_DIM:]
            for gi, h in enumerate(heads):
                o_ref[0, qrows, h * A_HEAD_DIM:(h + 1) * A_HEAD_DIM] = (
                    o[gi * A_BLOCK:(gi + 1) * A_BLOCK, :].astype(o_ref.dtype))


def window_attention(qkv, sinks, *, nblk):
    b, s, _ = qkv.shape
    dq = A_HEADS * A_HEAD_DIM
    dkv = A_KV_HEADS * A_HEAD_DIM
    tq = nblk * A_BLOCK
    return pl.pallas_call(
        functools.partial(_win_attn_kernel, seq=s, nblk=nblk),
        grid=(b, s // tq),
        in_specs=[pl.BlockSpec(memory_space=pltpu.SMEM),
                  pl.BlockSpec((1, tq, dq), lambda bi, i: (bi, i, 0)),
                  pl.BlockSpec((1, s, dkv), lambda bi, i: (bi, 0, dq // dkv)),
                  pl.BlockSpec((1, s, dkv), lambda bi, i: (bi, 0, dq // dkv + 1))],
        out_specs=pl.BlockSpec((1, tq, dq), lambda bi, i: (bi, i, 0)),
        out_shape=jax.ShapeDtypeStruct((b, s, dq), BF16),
        compiler_params=_cparams(("parallel", "arbitrary")),
        name="window_attention",
    )(sinks, qkv, qkv, qkv)


def _mla_down_kernel(x_ref, g_ref, wq_ref, wkv_ref, wkr_ref, qn_ref, kvn_ref, *rest, shift):
    *tabs, cq_ref, ckv_ref, kr_ref = rest
    h = _rms(x_ref[...], g_ref[...]).astype(BF16)
    cq_ref[...] = _rms(_dot(h, wq_ref[...]), qn_ref[...]).astype(BF16)
    ckv_ref[...] = _rms(_dot(h, wkv_ref[...]), kvn_ref[...]).astype(BF16)
    kr_ref[...] = _rope(_dot(h, wkr_ref[...]), tabs, shift).astype(BF16)


def mla_down(x, g, wq, wkv, wkr, qn, kvn, rope, *, seq, tm):
    m, d = x.shape
    shift, tabs = rope
    sblk = seq // tm
    tab_spec = pl.BlockSpec((tm, LANES), lambda i: (i % sblk, 0))
    full = lambda shape: pl.BlockSpec(shape, lambda i: (0, 0))
    widths = (B_Q_RANK, B_KV_RANK, LANES)
    return pl.pallas_call(
        functools.partial(_mla_down_kernel, shift=shift),
        grid=(m // tm,),
        in_specs=[pl.BlockSpec((tm, d), lambda i: (i, 0)), full((1, d)),
                  full((d, B_Q_RANK)), full((d, B_KV_RANK)), full((d, LANES)),
                  full((1, B_Q_RANK)), full((1, B_KV_RANK))] + [tab_spec] * len(tabs),
        out_specs=[pl.BlockSpec((tm, w), lambda i: (i, 0)) for w in widths],
        out_shape=[jax.ShapeDtypeStruct((m, w), BF16) for w in widths],
        compiler_params=_cparams(("parallel",)),
        name="mla_down",
    )(x, g.reshape(1, d), wq, wkv, wkr, qn.reshape(1, -1), kvn.reshape(1, -1), *tabs)


def _mla_kv_up_kernel(c_ref, kr_ref, w_ref, k_ref, v_ref):
    c = c_ref[...]
    kr = kr_ref[...]
    head_w = B_NOPE_DIM + B_V_DIM
    per_piece = PROJ_CHUNK // head_w
    for p0 in range(0, w_ref.shape[1] // head_w, per_piece):
        y = _dot(c, w_ref[:, p0 * head_w:(p0 + per_piece) * head_w])
        for hh in range(per_piece):
            head = p0 + hh
            k0 = head * B_QK_PAD
            k_ref[:, k0:k0 + B_NOPE_DIM] = y[:, hh * head_w:hh * head_w + B_NOPE_DIM].astype(BF16)
            k_ref[:, k0 + B_NOPE_DIM:k0 + B_QK_PAD] = kr
            v_ref[:, head * B_V_DIM:(head + 1) * B_V_DIM] = (
                y[:, hh * head_w + B_NOPE_DIM:(hh + 1) * head_w].astype(BF16))


def mla_kv_up(ckv, kr, w, *, tm, heads_per_tile):
    m, k = ckv.shape
    tn = heads_per_tile * (B_NOPE_DIM + B_V_DIM)
    return pl.pallas_call(
        _mla_kv_up_kernel,
        grid=(m // tm, B_HEADS // heads_per_tile),
        in_specs=[pl.BlockSpec((tm, k), lambda i, j: (i, 0)),
                  pl.BlockSpec((tm, LANES), lambda i, j: (i, 0)),
                  pl.BlockSpec((k, tn), lambda i, j: (0, j))],
        out_specs=[pl.BlockSpec((tm, heads_per_tile * B_QK_PAD), lambda i, j: (i, j)),
                   pl.BlockSpec((tm, heads_per_tile * B_V_DIM), lambda i, j: (i, j))],
        out_shape=[jax.ShapeDtypeStruct((m, B_HEADS * B_QK_PAD), BF16),
                   jax.ShapeDtypeStruct((m, B_HEADS * B_V_DIM), BF16)],
        compiler_params=_cparams(("parallel", "arbitrary")),
        name="mla_kv_up",
    )(ckv, kr, w)


def _mla_attn_kernel(q_ref, k_ref, v_ref, o_ref, *, chunk):
    v = v_ref[0]
    v_ones = jnp.concatenate([v, jnp.ones_like(v)], axis=1)
    dv = v.shape[1]
    for c in range(q_ref.shape[1] // chunk):
        rows = slice(c * chunk, (c + 1) * chunk)
        s = _dot_nt(q_ref[0, rows, :], k_ref[0])
        p = jnp.exp2(s - jnp.max(s, axis=-1, keepdims=True))
        o = _dot(p.astype(BF16), v_ones)
        o_ref[0, rows, :] = (o[:, :dv] / o[:, dv:]).astype(o_ref.dtype)


def mla_attention(qcat, kcat, v, *, tq, chunk):
    b, s, _ = qcat.shape
    return pl.pallas_call(
        functools.partial(_mla_attn_kernel, chunk=chunk),
        grid=(b, B_HEADS, s // tq),
        in_specs=[pl.BlockSpec((1, tq, B_QK_PAD), lambda bi, h, qi: (bi, qi, h)),
                  pl.BlockSpec((1, s, B_QK_PAD), lambda bi, h, qi: (bi, 0, h)),
                  pl.BlockSpec((1, s, B_V_DIM), lambda bi, h, qi: (bi, 0, h))],
        out_specs=pl.BlockSpec((1, tq, B_V_DIM), lambda bi, h, qi: (bi, qi, h)),
        out_shape=jax.ShapeDtypeStruct((b, s, B_HEADS * B_V_DIM), BF16),
        compiler_params=_cparams(("parallel", "parallel", "arbitrary")),
        name="mla_attention",
    )(qcat, kcat, v)


N_DR = 2 * NB_H - 1
SLAB_LOW = N_DR - 1
SLAB_HIGH = SLAB_LOW + N_DR
SLAB_NONE = SLAB_HIGH + N_DR


def _nb_slab_plan(rows):
    plan = []
    for rb in range(rows // C_QROWS):
        w0 = min(max(rb * C_QROWS - NB_H // 2, 0), rows - C_KROWS)
        block = []
        for i in range(C_QROWS):
            r = rb * C_QROWS + i
            rs = min(max(r - NB_H // 2, 0), rows - NB_H)
            ids = []
            for pair in range(C_KROWS // 2):
                kr = w0 + 2 * pair
                ok0, ok1 = rs <= kr < rs + NB_H, rs <= kr + 1 < rs + NB_H
                dr = kr - r + NB_H - 1
                ids.append(dr if ok0 and ok1 else SLAB_LOW + dr if ok0
                           else SLAB_HIGH + dr + 1 if ok1 else SLAB_NONE)
            block.append(tuple(ids))
        plan.append(tuple(block))
    return tuple(plan)


def _nb_attn_kernel(q_ref, k_ref, v_ref, t_ref, o_ref, *, plan, rows):
    tq = C_QROWS * GRID_W
    nk = C_KROWS * GRID_W
    for hh in range(C_HEADS_PER_STEP):
        cols = slice(hh * C_HEAD_DIM, (hh + 1) * C_HEAD_DIM)
        biases = {}
        for rb, block in enumerate(plan):
            if block not in biases:
                biases[block] = jnp.concatenate(
                    [jnp.concatenate([t_ref[hh, sid] for sid in ids], axis=1) for ids in block],
                    axis=0)
            start = min(max(rb * C_QROWS - NB_H // 2, 0), rows - C_KROWS) * GRID_W
            qrows = slice(rb * tq, (rb + 1) * tq)
            kw = k_ref[0, start:start + nk, cols]
            vw = v_ref[0, start:start + nk, cols]
            s = _dot_nt(q_ref[0, qrows, cols], kw) + biases[block]
            p = jnp.exp2(s - jnp.max(s, axis=-1, keepdims=True))
            o = _dot(p.astype(BF16), jnp.concatenate([vw, jnp.ones_like(vw)], axis=1))
            o_ref[0, qrows, cols] = (o[:, :C_HEAD_DIM] / o[:, C_HEAD_DIM:]).astype(o_ref.dtype)


def neighborhood_attention(qkv, slabs):
    b, s, _ = qkv.shape
    rows = s // GRID_W
    width = C_HEADS_PER_STEP * C_HEAD_DIM
    nhb = C_HEADS // C_HEADS_PER_STEP
    head_spec = lambda blk0: pl.BlockSpec((1, s, width), lambda h, bi: (bi, 0, blk0 + h))
    return pl.pallas_call(
        functools.partial(_nb_attn_kernel, plan=_nb_slab_plan(rows), rows=rows),
        grid=(nhb, b),
        in_specs=[head_spec(0), head_spec(nhb), head_spec(2 * nhb),
                  pl.BlockSpec((C_HEADS_PER_STEP,) + slabs.shape[1:], lambda h, bi: (h, 0, 0, 0))],
        out_specs=head_spec(0),
        out_shape=jax.ShapeDtypeStruct((b, s, C_HEADS * C_HEAD_DIM), BF16),
        compiler_params=_cparams(("parallel", "arbitrary")),
        name="neighborhood_attention",
    )(qkv, qkv, qkv, slabs)


def _nb_bias_slabs(rel_bias):
    c = np.arange(GRID_W)[:, None]
    kc = np.arange(GRID_W)[None, :]
    cs = np.clip(c - NB_W // 2, 0, GRID_W - NB_W)
    col_ok = (kc >= cs) & (kc < cs + NB_W)
    dc = np.clip(kc - c + NB_W - 1, 0, 2 * NB_W - 2)
    bias = jnp.where(col_ok[None, None], rel_bias.astype(F32)[:, :, dc] * LOG2E, NEG_BIG)
    masked = jnp.full_like(bias, NEG_BIG)
    return jnp.concatenate([jnp.concatenate([bias[:, :-1], bias[:, 1:]], axis=-1),
                            jnp.concatenate([bias, masked], axis=-1),
                            jnp.concatenate([masked, bias], axis=-1),
                            jnp.concatenate([masked[:, :1], masked[:, :1]], axis=-1)], axis=1)


def _rope_angles(seq, dim):
    pos = jnp.arange(seq, dtype=F32)
    inv = ROPE_THETA ** (-jnp.arange(0, dim, 2, dtype=F32) / dim)
    ang = pos[:, None] * inv[None, :]
    return jnp.cos(ang), jnp.sin(ang)


def _rope_adjacent(seq, dim):
    half = dim // 2
    c, sn = _rope_angles(seq, dim)
    cos = jnp.ones((seq, LANES), F32).at[:, :dim].set(jnp.concatenate([c, c], axis=1))
    up = jnp.zeros((seq, LANES), F32).at[:, half:dim].set(sn)
    down = jnp.zeros((seq, LANES), F32).at[:, :half].set(-sn)
    return half, (cos, up, down)


def _rope_swapped(seq, dim):
    half = dim // 2
    c, sn = _rope_angles(seq, dim)
    cos = jnp.ones((seq, LANES), F32).at[:, :half].set(c).at[:, ROPE_SWAP:ROPE_SWAP + half].set(c)
    sin = jnp.zeros((seq, LANES), F32).at[:, :half].set(-sn).at[:, ROPE_SWAP:ROPE_SWAP + half].set(sn)
    return ROPE_SWAP, (cos, sin)


def _pair_split(w, dim):
    half = dim // 2
    gap = ROPE_SWAP - half
    return jnp.concatenate([w[..., :half], w[..., dim:dim + gap], w[..., half:dim],
                            w[..., dim + gap:]], axis=-1)


def kernel(x, pre_mix_norm, post_mix_norm, pre_ffn_norm, post_ffn_norm, a_w_qkv, a_sinks, a_w_o,
           b_w_down, b_q_norm, b_w_uq, b_kv_norm, b_w_ukv, b_w_o, c_w_qkv, c_rel_bias, c_w_o,
           ffn_w_in, ffn_w_out):
    bsz, seq, d = x.shape
    m = bsz * seq
    rope_a = _rope_adjacent(seq, A_ROT_DIM)
    rope_b = _rope_swapped(seq, B_ROPE_DIM)
    w_in, w_out = ffn_w_in.astype(BF16), ffn_w_out.astype(BF16)
    xf = x.reshape(m, d)
    for i in range(DEPTH):
        kind, slot = i % N_MIXERS, i // N_MIXERS
        if kind == 0:
            ops = (((True, A_HEAD_DIM ** -0.5 * LOG2E),) * A_HEADS + ((True, 1.0),) * A_KV_HEADS
                   + (PLAIN,) * A_KV_HEADS)
            qkv = norm_proj(xf, pre_mix_norm[i], a_w_qkv.astype(BF16), rope_a, seq=seq, tm=TM,
                            tn=len(ops) * LANES, kinds=((0, 1, ops),), layer=slot)
            o = window_attention(qkv.reshape(bsz, seq, -1), a_sinks[slot].astype(F32),
                                 nblk=A_BLOCKS_PER_STEP)
            w_o = a_w_o
        elif kind == 1:
            wd = b_w_down[slot]
            wkr = jnp.pad(wd[:, B_Q_RANK + B_KV_RANK:], ((0, 0), (0, LANES - B_ROPE_DIM)))
            cq, ckv, kr = mla_down(xf, pre_mix_norm[i], wd[:, :B_Q_RANK].astype(BF16),
                                   wd[:, B_Q_RANK:B_Q_RANK + B_KV_RANK].astype(BF16),
                                   _pair_split(wkr, B_ROPE_DIM).astype(BF16),
                                   b_q_norm[slot], b_kv_norm[slot], rope_b, seq=seq, tm=TM)
            wuq = b_w_uq[slot].reshape(B_Q_RANK, B_HEADS, B_NOPE_DIM + B_ROPE_DIM)
            wuq_r = jnp.pad(wuq[:, :, B_NOPE_DIM:], ((0, 0), (0, 0), (0, LANES - B_ROPE_DIM)))
            wuq = jnp.concatenate([wuq[:, :, :B_NOPE_DIM], _pair_split(wuq_r, B_ROPE_DIM)], axis=2)
            scale = (B_NOPE_DIM + B_ROPE_DIM) ** -0.5 * LOG2E
            q_ops = ((False, scale), (True, scale)) * (MLA_TN // B_QK_PAD)
            qcat = proj(cq, wuq.reshape(B_Q_RANK, -1).astype(BF16), rope_b, seq=seq,
                        tm=TM, tn=MLA_TN, kinds=((0, 1, q_ops),))
            kcat, v = mla_kv_up(ckv, kr, b_w_ukv[slot].astype(BF16), tm=TM,
                                heads_per_tile=MLA_KV_HEADS_PER_TILE)
            o = mla_attention(qcat.reshape(bsz, seq, -1), kcat.reshape(bsz, seq, -1),
                              v.reshape(bsz, seq, -1), tq=seq, chunk=MLA_CHUNK)
            w_o = b_w_o
        else:
            tn = C_HEADS * C_HEAD_DIM
            q_ops = ((False, C_HEAD_DIM ** -0.5 * LOG2E),) * C_HEADS
            qkv = norm_proj(xf, pre_mix_norm[i], c_w_qkv.astype(BF16), rope_a, seq=seq, tm=TM,
                            tn=tn, kinds=((0, 1, q_ops), (1, 3, (PLAIN,) * C_HEADS)), layer=slot)
            o = neighborhood_attention(qkv.reshape(bsz, seq, -1), _nb_bias_slabs(c_rel_bias[slot]))
            w_o = c_w_o
        xf = out_proj_residual(o.reshape(m, -1), w_o.astype(BF16), post_mix_norm[i], xf,
                               layer=slot, tm=TM)
        xf = ffn(xf, pre_ffn_norm[i], w_in, w_out, post_ffn_norm[i], layer=i, tm=TM, tf=FFN_TF)
    return xf.reshape(bsz, seq, d)
```

```python
import functools

import jax
import jax.numpy as jnp
import numpy as np
from jax import lax
from jax.experimental import pallas as pl
from jax.experimental.pallas import tpu as pltpu

D_MODEL = 2048
DEPTH = 4
N_MIXERS = 3
GRID_W = 64
ROPE_THETA = 500000.0
NORM_EPS = 1e-6
D_FF = 5632

A_HEADS = 16
A_KV_HEADS = 4
A_GROUP = A_HEADS // A_KV_HEADS
A_HEAD_DIM = 128
A_ROT_DIM = 32
WINDOW = 128
A_BLOCK = 128

B_HEADS = 16
B_NOPE_DIM = 128
B_ROPE_DIM = 64
B_V_DIM = 128
B_Q_RANK = 512
B_KV_RANK = 512
B_QK_PAD = 256

C_HEADS = 16
C_HEAD_DIM = 128
NB_H = 8
NB_W = 16

LANES = 128
MXU_WIDTH = 256
V7X_VMEM_BYTES = 64 * 1024 * 1024
V7X_VMEM_RESERVE_BYTES = 4 * 1024 * 1024
V7X_VMEM_LIMIT_BYTES = V7X_VMEM_BYTES - V7X_VMEM_RESERVE_BYTES

TM = 1024
PROJ_CHUNK = 2 * MXU_WIDTH
FFN_PIECE = MXU_WIDTH
FFN_TF = 2 * FFN_PIECE
NORM_ROW_PIECES = 2
OUT_ROW_PIECES = 2
MLA_TN = 2048
MLA_CHUNK = 256
SINK_ROWS = 16
A_BLOCKS_PER_STEP = 8
C_QROWS = 8
C_KROWS = 16
C_HEADS_PER_STEP = 2
ROPE_SWAP = LANES // 2

NEG_BIG = -1e30
LOG2E = float(np.log2(np.e))
BF16 = jnp.bfloat16
F32 = jnp.float32
PLAIN = (False, 1.0)


def _cparams(semantics):
    return pltpu.CompilerParams(dimension_semantics=semantics,
                                vmem_limit_bytes=V7X_VMEM_LIMIT_BYTES)


def _rms(xf, g):
    ms = jnp.mean(xf * xf, axis=-1, keepdims=True)
    return xf * lax.rsqrt(ms + NORM_EPS) * g


def _inv_rms(src_ref, stat_ref):
    xf = src_ref[...]
    stat_ref[...] = lax.rsqrt(jnp.mean(xf * xf, axis=-1, keepdims=True) + NORM_EPS)
    return stat_ref[...]


def _dot(a, b):
    return jnp.dot(a, b, preferred_element_type=F32)


def _dot_nt(a, b):
    return lax.dot_general(a, b, (((1,), (1,)), ((), ())), preferred_element_type=F32)


def _rope(y, tabs, shift, rows=slice(None)):
    r = y * tabs[0][rows, :] + pltpu.roll(y, shift, 1) * tabs[1][rows, :]
    if shift != ROPE_SWAP:
        r = r + pltpu.roll(y, LANES - shift, 1) * tabs[2][rows, :]
    return r


def _project_tile(lhs, w_ref, o_ref, tabs, shift, ops, rows=slice(None)):
    per_chunk = PROJ_CHUNK // LANES
    for c0 in range(0, len(ops), per_chunk):
        chunk = slice(c0 * LANES, (c0 + per_chunk) * LANES)
        y = _dot(lhs, w_ref[:, chunk])
        if all(op == PLAIN for op in ops[c0:c0 + per_chunk]):
            o_ref[rows, chunk] = y.astype(o_ref.dtype)
            continue
        for c in range(per_chunk):
            rope, scale = ops[c0 + c]
            yc = y[:, c * LANES:(c + 1) * LANES]
            if scale != 1.0:
                yc = yc * scale
            if rope:
                yc = _rope(yc, tabs, shift, rows)
            o_ref[rows, (c0 + c) * LANES:(c0 + c + 1) * LANES] = yc.astype(o_ref.dtype)


def _project(lhs, w_ref, o_ref, tabs, shift, kinds):
    if len(kinds) == 1:
        _project_tile(lhs, w_ref, o_ref, tabs, shift, kinds[0][2])
        return
    j = pl.program_id(1)
    for lo, hi, ops in kinds:
        @pl.when((j >= lo) & (j < hi))
        def _(ops=ops):
            _project_tile(lhs, w_ref, o_ref, tabs, shift, ops)


def _weight_spec(w, tn, layer):
    k, n = w.shape[-2:]
    lead = () if layer is None else (layer,)
    shape = (None,) * len(lead) + (k, tn)
    if n == tn:
        return pl.BlockSpec(shape, lambda i, j: lead + (0, 0), pipeline_mode=pl.Buffered(1))
    return pl.BlockSpec(shape, lambda i, j: lead + (0, j))


def _norm_proj_kernel(x_ref, g_ref, w_ref, *rest, kinds, shift, n_tiles):
    *tabs, o_ref, h_ref = rest
    j = pl.program_id(1)
    piece = x_ref.shape[0] // NORM_ROW_PIECES

    @pl.when(j == 0)
    def _():
        for r in range(NORM_ROW_PIECES):
            rows = slice(r * piece, (r + 1) * piece)
            h = _rms(x_ref[rows, :], g_ref[...]).astype(BF16)
            if n_tiles > 1:
                h_ref[rows, :] = h
            _project_tile(h, w_ref, o_ref, tabs, shift, kinds[0][2], rows)

    for lo, hi, ops in kinds:
        if max(lo, 1) < hi:
            @pl.when((j >= max(lo, 1)) & (j < hi))
            def _(ops=ops):
                _project_tile(h_ref[...], w_ref, o_ref, tabs, shift, ops)


def norm_proj(x, g, w, rope, *, seq, tm, tn, kinds, layer=None):
    m, d = x.shape
    n = w.shape[-1]
    shift, tabs = rope
    sblk = seq // tm
    tab_spec = pl.BlockSpec((tm, LANES), lambda i, j: (i % sblk, 0))
    return pl.pallas_call(
        functools.partial(_norm_proj_kernel, kinds=kinds, shift=shift, n_tiles=n // tn),
        grid=(m // tm, n // tn),
        in_specs=[pl.BlockSpec((tm, d), lambda i, j: (i, 0)),
                  pl.BlockSpec((1, d), lambda i, j: (0, 0)),
                  _weight_spec(w, tn, layer)] + [tab_spec] * len(tabs),
        out_specs=pl.BlockSpec((tm, tn), lambda i, j: (i, j)),
        out_shape=jax.ShapeDtypeStruct((m, n), BF16),
        scratch_shapes=[pltpu.VMEM((tm, d), BF16)],
        compiler_params=_cparams(("parallel", "arbitrary")),
        name="norm_proj",
    )(x, g.reshape(1, d), w, *tabs)


def _proj_kernel(a_ref, w_ref, *rest, kinds, shift):
    *tabs, o_ref = rest
    _project(a_ref[...], w_ref, o_ref, tabs, shift, kinds)


def proj(a, w, rope, *, seq, tm, tn, kinds, layer=None):
    m, k = a.shape
    n = w.shape[-1]
    shift, tabs = rope
    sblk = seq // tm
    tab_spec = pl.BlockSpec((tm, LANES), lambda i, j: (i % sblk, 0))
    return pl.pallas_call(
        functools.partial(_proj_kernel, kinds=kinds, shift=shift),
        grid=(m // tm, n // tn),
        in_specs=[pl.BlockSpec((tm, k), lambda i, j: (i, 0)),
                  _weight_spec(w, tn, layer)] + [tab_spec] * len(tabs),
        out_specs=pl.BlockSpec((tm, tn), lambda i, j: (i, j)),
        out_shape=jax.ShapeDtypeStruct((m, n), BF16),
        compiler_params=_cparams(("parallel", "arbitrary")),
        name="proj",
    )(a, w, *tabs)


def _out_proj_kernel(a_ref, w_ref, g_ref, x_ref, o_ref):
    piece = a_ref.shape[0] // OUT_ROW_PIECES
    for r in range(OUT_ROW_PIECES):
        rows = slice(r * piece, (r + 1) * piece)
        a = a_ref[rows, :]
        m = jnp.concatenate([_dot(a, w_ref[:, c:c + PROJ_CHUNK])
                             for c in range(0, w_ref.shape[1], PROJ_CHUNK)], axis=1)
        o_ref[rows, :] = x_ref[rows, :] + _rms(m, g_ref[...])


def out_proj_residual(a, w, g, x, *, layer, tm):
    m, k = a.shape
    d = w.shape[-1]
    return pl.pallas_call(
        _out_proj_kernel,
        grid=(m // tm,),
        in_specs=[pl.BlockSpec((tm, k), lambda i: (i, 0)),
                  pl.BlockSpec((None, k, d), lambda i: (layer, 0, 0), pipeline_mode=pl.Buffered(1)),
                  pl.BlockSpec((1, d), lambda i: (0, 0)),
                  pl.BlockSpec((tm, d), lambda i: (i, 0))],
        out_specs=pl.BlockSpec((tm, d), lambda i: (i, 0)),
        out_shape=jax.ShapeDtypeStruct((m, d), F32),
        compiler_params=_cparams(("parallel",)),
        name="out_proj_residual",
    )(a, w, g.reshape(1, d), x)


def _ffn_kernel(x_ref, gpre_ref, wg_ref, wu_ref, wo_ref, gpost_ref, o_ref, h_ref, stat_ref,
                act_ref):
    j = pl.program_id(1)

    @pl.when(j == 0)
    def _():
        h_ref[...] = _rms(x_ref[...], gpre_ref[...]).astype(BF16)
        o_ref[...] = jnp.zeros_like(o_ref)

    h = h_ref[...]
    for c in range(0, act_ref.shape[1], FFN_PIECE):
        cols = slice(c, c + FFN_PIECE)
        gate = _dot(h, wg_ref[:, cols])
        up = _dot(h, wu_ref[:, cols])
        act_ref[:, cols] = (gate / (1.0 + jnp.exp(-gate)) * up).astype(BF16)
    o_ref[...] += _dot(act_ref[...], wo_ref[...])

    @pl.when(j == pl.num_programs(1) - 1)
    def _():
        o_ref[...] = x_ref[...] + o_ref[...] * _inv_rms(o_ref, stat_ref) * gpost_ref[...]


def ffn(x, gpre, w_in, w_out, gpost, *, layer, tm, tf):
    m, d = x.shape
    dff = w_out.shape[1]
    nf = dff // tf
    return pl.pallas_call(
        _ffn_kernel,
        grid=(m // tm, nf),
        in_specs=[pl.BlockSpec((tm, d), lambda i, j: (i, 0)),
                  pl.BlockSpec((1, d), lambda i, j: (0, 0)),
                  pl.BlockSpec((None, d, tf), lambda i, j: (layer, 0, j)),
                  pl.BlockSpec((None, d, tf), lambda i, j: (layer, 0, nf + j)),
                  pl.BlockSpec((None, tf, d), lambda i, j: (layer, j, 0)),
                  pl.BlockSpec((1, d), lambda i, j: (0, 0))],
        out_specs=pl.BlockSpec((tm, d), lambda i, j: (i, 0)),
        out_shape=jax.ShapeDtypeStruct((m, d), F32),
        scratch_shapes=[pltpu.VMEM((tm, d), BF16), pltpu.VMEM((tm, 1), F32),
                        pltpu.VMEM((tm, tf), BF16)],
        compiler_params=_cparams(("parallel", "arbitrary")),
        name="ffn",
    )(x, gpre.reshape(1, d), w_in, w_in, w_out, gpost.reshape(1, d))


def _win_attn_kernel(sink_ref, q_ref, k_ref, v_ref, o_ref, *, seq, nblk):
    kwin = 3 * A_BLOCK
    cols = A_GROUP * A_BLOCK
    sink_values = jnp.concatenate([jnp.zeros((SINK_ROWS, A_HEAD_DIM), BF16),
                                   jnp.full((SINK_ROWS, A_HEAD_DIM), 1.0 / SINK_ROWS, BF16)], axis=1)
    for jb in range(nblk):
        blk = pl.program_id(1) * nblk + jb
        start = pl.multiple_of(jnp.clip((blk - 1) * A_BLOCK, 0, seq - kwin), A_BLOCK)
        kpos = start + lax.broadcasted_iota(jnp.int32, (kwin, cols), 0)
        qpos = blk * A_BLOCK + (lax.broadcasted_iota(jnp.int32, (kwin, cols), 1) & (A_BLOCK - 1))
        mask = jnp.where(jnp.abs(kpos - qpos) <= WINDOW, 0.0, NEG_BIG)
        qrows = slice(jb * A_BLOCK, (jb + 1) * A_BLOCK)
        for kv in range(A_KV_HEADS):
            heads = range(kv * A_GROUP, (kv + 1) * A_GROUP)
            kvcols = slice(kv * A_HEAD_DIM, (kv + 1) * A_HEAD_DIM)
            q4 = jnp.concatenate(
                [q_ref[0, qrows, h * A_HEAD_DIM:(h + 1) * A_HEAD_DIM] for h in heads], axis=0)
            kw = k_ref[0, pl.ds(start, kwin), kvcols]
            vw = v_ref[0, pl.ds(start, kwin), kvcols]
            st = _dot_nt(kw, q4) + mask
            sink = jnp.concatenate(
                [jnp.full((1, A_BLOCK), sink_ref[h] * LOG2E, F32) for h in heads], axis=1)
            mx = jnp.maximum(jnp.max(st, axis=0, keepdims=True), sink)
            sink_rows = jnp.broadcast_to(jnp.exp2(sink - mx), (SINK_ROWS, cols))
            pt = jnp.concatenate([jnp.exp2(st - mx), sink_rows], axis=0).astype(BF16)
            v_ext = jnp.concatenate(
                [jnp.concatenate([vw, jnp.ones_like(vw)], axis=1), sink_values], axis=0)
            o = lax.dot_general(pt, v_ext, (((0,), (0,)), ((), ())), preferred_element_type=F32)
            o = o[:, :A_HEAD_DIM] / o[:, A_HEAD_DIM:]
            for gi, h in enumerate(heads):
                o_ref[0, qrows, h * A_HEAD_DIM:(h + 1) * A_HEAD_DIM] = (
                    o[gi * A_BLOCK:(gi + 1) * A_BLOCK, :].astype(o_ref.dtype))


def window_attention(qkv, sinks, *, nblk):
    b, s, _ = qkv.shape
    dq = A_HEADS * A_HEAD_DIM
    dkv = A_KV_HEADS * A_HEAD_DIM
    tq = nblk * A_BLOCK
    return pl.pallas_call(
        functools.partial(_win_attn_kernel, seq=s, nblk=nblk),
        grid=(b, s // tq),
        in_specs=[pl.BlockSpec(memory_space=pltpu.SMEM),
                  pl.BlockSpec((1, tq, dq), lambda bi, i: (bi, i, 0)),
                  pl.BlockSpec((1, s, dkv), lambda bi, i: (bi, 0, dq // dkv)),
                  pl.BlockSpec((1, s, dkv), lambda bi, i: (bi, 0, dq // dkv + 1))],
        out_specs=pl.BlockSpec((1, tq, dq), lambda bi, i: (bi, i, 0)),
        out_shape=jax.ShapeDtypeStruct((b, s, dq), BF16),
        compiler_params=_cparams(("parallel", "arbitrary")),
        name="window_attention",
    )(sinks, qkv, qkv, qkv)


def _mla_down_kernel(x_ref, g_ref, wq_ref, wkv_ref, wkr_ref, qn_ref, kvn_ref, *rest, shift):
    *tabs, cq_ref, ckv_ref, kr_ref = rest
    h = _rms(x_ref[...], g_ref[...]).astype(BF16)
    cq_ref[...] = _rms(_dot(h, wq_ref[...]), qn_ref[...]).astype(BF16)
    ckv_ref[...] = _rms(_dot(h, wkv_ref[...]), kvn_ref[...]).astype(BF16)
    kr_ref[...] = _rope(_dot(h, wkr_ref[...]), tabs, shift).astype(BF16)


def mla_down(x, g, wq, wkv, wkr, qn, kvn, rope, *, seq, tm):
    m, d = x.shape
    shift, tabs = rope
    sblk = seq // tm
    tab_spec = pl.BlockSpec((tm, LANES), lambda i: (i % sblk, 0))
    full = lambda shape: pl.BlockSpec(shape, lambda i: (0, 0))
    widths = (B_Q_RANK, B_KV_RANK, LANES)
    return pl.pallas_call(
        functools.partial(_mla_down_kernel, shift=shift),
        grid=(m // tm,),
        in_specs=[pl.BlockSpec((tm, d), lambda i: (i, 0)), full((1, d)),
                  full((d, B_Q_RANK)), full((d, B_KV_RANK)), full((d, LANES)),
                  full((1, B_Q_RANK)), full((1, B_KV_RANK))] + [tab_spec] * len(tabs),
        out_specs=[pl.BlockSpec((tm, w), lambda i: (i, 0)) for w in widths],
        out_shape=[jax.ShapeDtypeStruct((m, w), BF16) for w in widths],
        compiler_params=_cparams(("parallel",)),
        name="mla_down",
    )(x, g.reshape(1, d), wq, wkv, wkr, qn.reshape(1, -1), kvn.reshape(1, -1), *tabs)


def _mla_attn_kernel(q_ref, kn_ref, kr_ref, v_ref, o_ref, *, chunk):
    k = jnp.concatenate([kn_ref[0], kr_ref[0]], axis=1)
    v = v_ref[0]
    v_ones = jnp.concatenate([v, jnp.ones_like(v)], axis=1)
    dv = v.shape[1]
    for c in range(q_ref.shape[1] // chunk):
        rows = slice(c * chunk, (c + 1) * chunk)
        s = _dot_nt(q_ref[0, rows, :], k)
        p = jnp.exp2(s - jnp.max(s, axis=-1, keepdims=True))
        o = _dot(p.astype(BF16), v_ones)
        o_ref[0, rows, :] = (o[:, :dv] / o[:, dv:]).astype(o_ref.dtype)


def mla_attention(qcat, kv, kr, *, tq, chunk):
    b, s, _ = qcat.shape
    kv_blocks = (B_NOPE_DIM + B_V_DIM) // B_V_DIM
    return pl.pallas_call(
        functools.partial(_mla_attn_kernel, chunk=chunk),
        grid=(b, B_HEADS, s // tq),
        in_specs=[pl.BlockSpec((1, tq, B_QK_PAD), lambda bi, h, qi: (bi, qi, h)),
                  pl.BlockSpec((1, s, B_NOPE_DIM), lambda bi, h, qi: (bi, 0, kv_blocks * h)),
                  pl.BlockSpec((1, s, LANES), lambda bi, h, qi: (bi, 0, 0)),
                  pl.BlockSpec((1, s, B_V_DIM), lambda bi, h, qi: (bi, 0, kv_blocks * h + 1))],
        out_specs=pl.BlockSpec((1, tq, B_V_DIM), lambda bi, h, qi: (bi, qi, h)),
        out_shape=jax.ShapeDtypeStruct((b, s, B_HEADS * B_V_DIM), BF16),
        compiler_params=_cparams(("parallel", "parallel", "arbitrary")),
        name="mla_attention",
    )(qcat, kv, kr, kv)


N_DR = 2 * NB_H - 1
SLAB_LOW = N_DR - 1
SLAB_HIGH = SLAB_LOW + N_DR
SLAB_NONE = SLAB_HIGH + N_DR


def _nb_slab_plan(rows):
    plan = []
    for rb in range(rows // C_QROWS):
        w0 = min(max(rb * C_QROWS - NB_H // 2, 0), rows - C_KROWS)
        block = []
        for i in range(C_QROWS):
            r = rb * C_QROWS + i
            rs = min(max(r - NB_H // 2, 0), rows - NB_H)
            ids = []
            for pair in range(C_KROWS // 2):
                kr = w0 + 2 * pair
                ok0, ok1 = rs <= kr < rs + NB_H, rs <= kr + 1 < rs + NB_H
                dr = kr - r + NB_H - 1
                ids.append(dr if ok0 and ok1 else SLAB_LOW + dr if ok0
                           else SLAB_HIGH + dr + 1 if ok1 else SLAB_NONE)
            block.append(tuple(ids))
        plan.append(tuple(block))
    return tuple(plan)


def _nb_attn_kernel(q_ref, k_ref, v_ref, t_ref, o_ref, *, plan, rows):
    tq = C_QROWS * GRID_W
    nk = C_KROWS * GRID_W
    for hh in range(C_HEADS_PER_STEP):
        cols = slice(hh * C_HEAD_DIM, (hh + 1) * C_HEAD_DIM)
        biases = {}
        for rb, block in enumerate(plan):
            if block not in biases:
                biases[block] = jnp.concatenate(
                    [jnp.concatenate([t_ref[hh, sid] for sid in ids], axis=1) for ids in block],
                    axis=0)
            start = min(max(rb * C_QROWS - NB_H // 2, 0), rows - C_KROWS) * GRID_W
            qrows = slice(rb * tq, (rb + 1) * tq)
            kw = k_ref[0, start:start + nk, cols]
            vw = v_ref[0, start:start + nk, cols]
            s = _dot_nt(q_ref[0, qrows, cols], kw) + biases[block]
            p = jnp.exp2(s - jnp.max(s, axis=-1, keepdims=True))
            o = _dot(p.astype(BF16), jnp.concatenate([vw, jnp.ones_like(vw)], axis=1))
            o_ref[0, qrows, cols] = (o[:, :C_HEAD_DIM] / o[:, C_HEAD_DIM:]).astype(o_ref.dtype)


def neighborhood_attention(qkv, slabs):
    b, s, _ = qkv.shape
    rows = s // GRID_W
    width = C_HEADS_PER_STEP * C_HEAD_DIM
    nhb = C_HEADS // C_HEADS_PER_STEP
    head_spec = lambda blk0: pl.BlockSpec((1, s, width), lambda h, bi: (bi, 0, blk0 + h))
    return pl.pallas_call(
        functools.partial(_nb_attn_kernel, plan=_nb_slab_plan(rows), rows=rows),
        grid=(nhb, b),
        in_specs=[head_spec(0), head_spec(nhb), head_spec(2 * nhb),
                  pl.BlockSpec((C_HEADS_PER_STEP,) + slabs.shape[1:], lambda h, bi: (h, 0, 0, 0))],
        out_specs=head_spec(0),
        out_shape=jax.ShapeDtypeStruct((b, s, C_HEADS * C_HEAD_DIM), BF16),
        compiler_params=_cparams(("parallel", "arbitrary")),
        name="neighborhood_attention",
    )(qkv, qkv, qkv, slabs)


def _nb_bias_slabs(rel_bias):
    c = np.arange(GRID_W)[:, None]
    kc = np.arange(GRID_W)[None, :]
    cs = np.clip(c - NB_W // 2, 0, GRID_W - NB_W)
    col_ok = (kc >= cs) & (kc < cs + NB_W)
    dc = np.clip(kc - c + NB_W - 1, 0, 2 * NB_W - 2)
    bias = jnp.where(col_ok[None, None], rel_bias.astype(F32)[:, :, dc] * LOG2E, NEG_BIG)
    masked = jnp.full_like(bias, NEG_BIG)
    return jnp.concatenate([jnp.concatenate([bias[:, :-1], bias[:, 1:]], axis=-1),
                            jnp.concatenate([bias, masked], axis=-1),
                            jnp.concatenate([masked, bias], axis=-1),
                            jnp.concatenate([masked[:, :1], masked[:, :1]], axis=-1)], axis=1)


def _rope_angles(seq, dim):
    pos = jnp.arange(seq, dtype=F32)
    inv = ROPE_THETA ** (-jnp.arange(0, dim, 2, dtype=F32) / dim)
    ang = pos[:, None] * inv[None, :]
    return jnp.cos(ang), jnp.sin(ang)


def _rope_adjacent(seq, dim):
    half = dim // 2
    c, sn = _rope_angles(seq, dim)
    cos = jnp.ones((seq, LANES), F32).at[:, :dim].set(jnp.concatenate([c, c], axis=1))
    up = jnp.zeros((seq, LANES), F32).at[:, half:dim].set(sn)
    down = jnp.zeros((seq, LANES), F32).at[:, :half].set(-sn)
    return half, (cos, up, down)


def _rope_swapped(seq, dim):
    half = dim // 2
    c, sn = _rope_angles(seq, dim)
    cos = jnp.ones((seq, LANES), F32).at[:, :half].set(c).at[:, ROPE_SWAP:ROPE_SWAP + half].set(c)
    sin = jnp.zeros((seq, LANES), F32).at[:, :half].set(-sn).at[:, ROPE_SWAP:ROPE_SWAP + half].set(sn)
    return ROPE_SWAP, (cos, sin)


def _pair_split(w, dim):
    half = dim // 2
    gap = ROPE_SWAP - half
    return jnp.concatenate([w[..., :half], w[..., dim:dim + gap], w[..., half:dim],
                            w[..., dim + gap:]], axis=-1)


def kernel(x, pre_mix_norm, post_mix_norm, pre_ffn_norm, post_ffn_norm, a_w_qkv, a_sinks, a_w_o,
           b_w_down, b_q_norm, b_w_uq, b_kv_norm, b_w_ukv, b_w_o, c_w_qkv, c_rel_bias, c_w_o,
           ffn_w_in, ffn_w_out):
    bsz, seq, d = x.shape
    m = bsz * seq
    rope_a = _rope_adjacent(seq, A_ROT_DIM)
    rope_b = _rope_swapped(seq, B_ROPE_DIM)
    w_in, w_out = ffn_w_in.astype(BF16), ffn_w_out.astype(BF16)
    xf = x.reshape(m, d)
    for i in range(DEPTH):
        kind, slot = i % N_MIXERS, i // N_MIXERS
        if kind == 0:
            ops = (((True, A_HEAD_DIM ** -0.5 * LOG2E),) * A_HEADS + ((True, 1.0),) * A_KV_HEADS
                   + (PLAIN,) * A_KV_HEADS)
            qkv = norm_proj(xf, pre_mix_norm[i], a_w_qkv.astype(BF16), rope_a, seq=seq, tm=TM,
                            tn=len(ops) * LANES, kinds=((0, 1, ops),), layer=slot)
            o = window_attention(qkv.reshape(bsz, seq, -1), a_sinks[slot].astype(F32),
                                 nblk=A_BLOCKS_PER_STEP)
            w_o = a_w_o
        elif kind == 1:
            wd = b_w_down[slot]
            wkr = jnp.pad(wd[:, B_Q_RANK + B_KV_RANK:], ((0, 0), (0, LANES - B_ROPE_DIM)))
            cq, ckv, kr = mla_down(xf, pre_mix_norm[i], wd[:, :B_Q_RANK].astype(BF16),
                                   wd[:, B_Q_RANK:B_Q_RANK + B_KV_RANK].astype(BF16),
                                   _pair_split(wkr, B_ROPE_DIM).astype(BF16),
                                   b_q_norm[slot], b_kv_norm[slot], rope_b, seq=seq, tm=TM)
            wuq = b_w_uq[slot].reshape(B_Q_RANK, B_HEADS, B_NOPE_DIM + B_ROPE_DIM)
            wuq_r = jnp.pad(wuq[:, :, B_NOPE_DIM:], ((0, 0), (0, 0), (0, LANES - B_ROPE_DIM)))
            wuq = jnp.concatenate([wuq[:, :, :B_NOPE_DIM], _pair_split(wuq_r, B_ROPE_DIM)], axis=2)
            scale = (B_NOPE_DIM + B_ROPE_DIM) ** -0.5 * LOG2E
            q_ops = ((False, scale), (True, scale)) * (MLA_TN // B_QK_PAD)
            qcat = proj(cq, wuq.reshape(B_Q_RANK, -1).astype(BF16), rope_b, seq=seq,
                        tm=TM, tn=MLA_TN, kinds=((0, 1, q_ops),))
            kv = proj(ckv, b_w_ukv[slot].astype(BF16), rope_b, seq=seq, tm=TM, tn=MLA_TN,
                      kinds=((0, 1, (PLAIN,) * (MLA_TN // LANES)),))
            o = mla_attention(qcat.reshape(bsz, seq, -1), kv.reshape(bsz, seq, -1),
                              kr.reshape(bsz, seq, -1), tq=seq, chunk=MLA_CHUNK)
            w_o = b_w_o
        else:
            tn = C_HEADS * C_HEAD_DIM
            q_ops = ((False, C_HEAD_DIM ** -0.5 * LOG2E),) * C_HEADS
            qkv = norm_proj(xf, pre_mix_norm[i], c_w_qkv.astype(BF16), rope_a, seq=seq, tm=TM,
                            tn=tn, kinds=((0, 1, q_ops), (1, 3, (PLAIN,) * C_HEADS)), layer=slot)
            o = neighborhood_attention(qkv.reshape(bsz, seq, -1), _nb_bias_slabs(c_rel_bias[slot]))
            w_o = c_w_o
        xf = out_proj_residual(o.reshape(m, -1), w_o.astype(BF16), post_mix_norm[i], xf,
                               layer=slot, tm=TM)
        xf = ffn(xf, pre_ffn_norm[i], w_in, w_out, post_ffn_norm[i], layer=i, tm=TM, tf=FFN_TF)
    return xf.reshape(bsz, seq, d)
```

```python
import functools

import jax
import jax.numpy as jnp
import numpy as np
from jax import lax
from jax.experimental import pallas as pl
from jax.experimental.pallas import tpu as pltpu

D_MODEL = 2048
DEPTH = 4
N_MIXERS = 3
GRID_W = 64
ROPE_THETA = 500000.0
NORM_EPS = 1e-6
D_FF = 5632

A_HEADS = 16
A_KV_HEADS = 4
A_GROUP = A_HEADS // A_KV_HEADS
A_HEAD_DIM = 128
A_ROT_DIM = 32
WINDOW = 128
A_BLOCK = 128

B_HEADS = 16
B_NOPE_DIM = 128
B_ROPE_DIM = 64
B_V_DIM = 128
B_Q_RANK = 512
B_KV_RANK = 512
B_QK_PAD = 256

C_HEADS = 16
C_HEAD_DIM = 128
NB_H = 8
NB_W = 16

LANES = 128
MXU_WIDTH = 256
V7X_VMEM_BYTES = 64 * 1024 * 1024
V7X_VMEM_RESERVE_BYTES = 4 * 1024 * 1024
V7X_VMEM_LIMIT_BYTES = V7X_VMEM_BYTES - V7X_VMEM_RESERVE_BYTES

TM = 1024
PROJ_CHUNK = 2 * MXU_WIDTH
FFN_PIECE = MXU_WIDTH
FFN_TF = 2 * FFN_PIECE
NORM_ROW_PIECES = 2
OUT_ROW_PIECES = 2
MLA_TN = 2048
MLA_CHUNK = 256
SINK_ROWS = 16
A_BLOCKS_PER_STEP = 8
C_QROWS = 8
C_KROWS = 16
C_HEADS_PER_STEP = 2
ROPE_SWAP = LANES // 2

NEG_BIG = -1e30
LOG2E = float(np.log2(np.e))
BF16 = jnp.bfloat16
F32 = jnp.float32
PLAIN = (False, 1.0)


def _cparams(semantics):
    return pltpu.CompilerParams(dimension_semantics=semantics,
                                vmem_limit_bytes=V7X_VMEM_LIMIT_BYTES)


def _rms(xf, g):
    ms = jnp.mean(xf * xf, axis=-1, keepdims=True)
    return xf * lax.rsqrt(ms + NORM_EPS) * g


def _inv_rms(src_ref, stat_ref):
    xf = src_ref[...]
    stat_ref[...] = lax.rsqrt(jnp.mean(xf * xf, axis=-1, keepdims=True) + NORM_EPS)
    return stat_ref[...]


def _dot(a, b):
    return jnp.dot(a, b, preferred_element_type=F32)


def _dot_nt(a, b):
    return lax.dot_general(a, b, (((1,), (1,)), ((), ())), preferred_element_type=F32)


def _rope(y, tabs, shift, rows=slice(None)):
    r = y * tabs[0][rows, :] + pltpu.roll(y, shift, 1) * tabs[1][rows, :]
    if shift != ROPE_SWAP:
        r = r + pltpu.roll(y, LANES - shift, 1) * tabs[2][rows, :]
    return r


def _project_tile(lhs, w_ref, o_ref, tabs, shift, ops, rows=slice(None)):
    per_chunk = PROJ_CHUNK // LANES
    for c0 in range(0, len(ops), per_chunk):
        chunk = slice(c0 * LANES, (c0 + per_chunk) * LANES)
        y = _dot(lhs, w_ref[:, chunk])
        if all(op == PLAIN for op in ops[c0:c0 + per_chunk]):
            o_ref[rows, chunk] = y.astype(o_ref.dtype)
            continue
        for c in range(per_chunk):
            rope, scale = ops[c0 + c]
            yc = y[:, c * LANES:(c + 1) * LANES]
            if scale != 1.0:
                yc = yc * scale
            if rope:
                yc = _rope(yc, tabs, shift, rows)
            o_ref[rows, (c0 + c) * LANES:(c0 + c + 1) * LANES] = yc.astype(o_ref.dtype)


def _project(lhs, w_ref, o_ref, tabs, shift, kinds):
    if len(kinds) == 1:
        _project_tile(lhs, w_ref, o_ref, tabs, shift, kinds[0][2])
        return
    j = pl.program_id(1)
    for lo, hi, ops in kinds:
        @pl.when((j >= lo) & (j < hi))
        def _(ops=ops):
            _project_tile(lhs, w_ref, o_ref, tabs, shift, ops)


def _weight_spec(w, tn, layer):
    k, n = w.shape[-2:]
    lead = () if layer is None else (layer,)
    shape = (None,) * len(lead) + (k, tn)
    if n == tn:
        return pl.BlockSpec(shape, lambda i, j: lead + (0, 0), pipeline_mode=pl.Buffered(1))
    return pl.BlockSpec(shape, lambda i, j: lead + (0, j))


def _norm_proj_kernel(x_ref, g_ref, w_ref, *rest, kinds, shift, n_tiles):
    *tabs, o_ref, h_ref = rest
    j = pl.program_id(1)
    piece = x_ref.shape[0] // NORM_ROW_PIECES

    @pl.when(j == 0)
    def _():
        for r in range(NORM_ROW_PIECES):
            rows = slice(r * piece, (r + 1) * piece)
            h = _rms(x_ref[rows, :], g_ref[...]).astype(BF16)
            if n_tiles > 1:
                h_ref[rows, :] = h
            _project_tile(h, w_ref, o_ref, tabs, shift, kinds[0][2], rows)

    for lo, hi, ops in kinds:
        if max(lo, 1) < hi:
            @pl.when((j >= max(lo, 1)) & (j < hi))
            def _(ops=ops):
                _project_tile(h_ref[...], w_ref, o_ref, tabs, shift, ops)


def norm_proj(x, g, w, rope, *, seq, tm, tn, kinds, layer=None):
    m, d = x.shape
    n = w.shape[-1]
    shift, tabs = rope
    sblk = seq // tm
    tab_spec = pl.BlockSpec((tm, LANES), lambda i, j: (i % sblk, 0))
    return pl.pallas_call(
        functools.partial(_norm_proj_kernel, kinds=kinds, shift=shift, n_tiles=n // tn),
        grid=(m // tm, n // tn),
        in_specs=[pl.BlockSpec((tm, d), lambda i, j: (i, 0)),
                  pl.BlockSpec((1, d), lambda i, j: (0, 0)),
                  _weight_spec(w, tn, layer)] + [tab_spec] * len(tabs),
        out_specs=pl.BlockSpec((tm, tn), lambda i, j: (i, j)),
        out_shape=jax.ShapeDtypeStruct((m, n), BF16),
        scratch_shapes=[pltpu.VMEM((tm, d), BF16)],
        compiler_params=_cparams(("parallel", "arbitrary")),
        name="norm_proj",
    )(x, g.reshape(1, d), w, *tabs)


def _proj_kernel(a_ref, w_ref, *rest, kinds, shift):
    *tabs, o_ref = rest
    _project(a_ref[...], w_ref, o_ref, tabs, shift, kinds)


def proj(a, w, rope, *, seq, tm, tn, kinds, layer=None):
    m, k = a.shape
    n = w.shape[-1]
    shift, tabs = rope
    sblk = seq // tm
    tab_spec = pl.BlockSpec((tm, LANES), lambda i, j: (i % sblk, 0))
    return pl.pallas_call(
        functools.partial(_proj_kernel, kinds=kinds, shift=shift),
        grid=(m // tm, n // tn),
        in_specs=[pl.BlockSpec((tm, k), lambda i, j: (i, 0)),
                  _weight_spec(w, tn, layer)] + [tab_spec] * len(tabs),
        out_specs=pl.BlockSpec((tm, tn), lambda i, j: (i, j)),
        out_shape=jax.ShapeDtypeStruct((m, n), BF16),
        compiler_params=_cparams(("parallel", "arbitrary")),
        name="proj",
    )(a, w, *tabs)


def _out_proj_kernel(a_ref, w_ref, g_ref, x_ref, o_ref):
    piece = a_ref.shape[0] // OUT_ROW_PIECES
    for r in range(OUT_ROW_PIECES):
        rows = slice(r * piece, (r + 1) * piece)
        a = a_ref[rows, :]
        m = jnp.concatenate([_dot(a, w_ref[:, c:c + PROJ_CHUNK])
                             for c in range(0, w_ref.shape[1], PROJ_CHUNK)], axis=1)
        o_ref[rows, :] = x_ref[rows, :] + _rms(m, g_ref[...])


def out_proj_residual(a, w, g, x, *, layer, tm):
    m, k = a.shape
    d = w.shape[-1]
    return pl.pallas_call(
        _out_proj_kernel,
        grid=(m // tm,),
        in_specs=[pl.BlockSpec((tm, k), lambda i: (i, 0)),
                  pl.BlockSpec((None, k, d), lambda i: (layer, 0, 0), pipeline_mode=pl.Buffered(1)),
                  pl.BlockSpec((1, d), lambda i: (0, 0)),
                  pl.BlockSpec((tm, d), lambda i: (i, 0))],
        out_specs=pl.BlockSpec((tm, d), lambda i: (i, 0)),
        out_shape=jax.ShapeDtypeStruct((m, d), F32),
        compiler_params=_cparams(("parallel",)),
        name="out_proj_residual",
    )(a, w, g.reshape(1, d), x)


def _ffn_kernel(x_ref, gpre_ref, wg_ref, wu_ref, wo_ref, gpost_ref, o_ref, h_ref, stat_ref,
                act_ref):
    j = pl.program_id(1)

    @pl.when(j == 0)
    def _():
        h_ref[...] = _rms(x_ref[...], gpre_ref[...]).astype(BF16)
        o_ref[...] = jnp.zeros_like(o_ref)

    h = h_ref[...]
    for c in range(0, act_ref.shape[1], FFN_PIECE):
        cols = slice(c, c + FFN_PIECE)
        gate = _dot(h, wg_ref[:, cols])
        up = _dot(h, wu_ref[:, cols])
        act_ref[:, cols] = (gate / (1.0 + jnp.exp(-gate)) * up).astype(BF16)
    o_ref[...] += _dot(act_ref[...], wo_ref[...].astype(BF16))

    @pl.when(j == pl.num_programs(1) - 1)
    def _():
        o_ref[...] = x_ref[...] + o_ref[...] * _inv_rms(o_ref, stat_ref) * gpost_ref[...]


def ffn(x, gpre, w_in, w_out, gpost, *, layer, tm, tf):
    m, d = x.shape
    dff = w_out.shape[1]
    nf = dff // tf
    return pl.pallas_call(
        _ffn_kernel,
        grid=(m // tm, nf),
        in_specs=[pl.BlockSpec((tm, d), lambda i, j: (i, 0)),
                  pl.BlockSpec((1, d), lambda i, j: (0, 0)),
                  pl.BlockSpec((None, d, tf), lambda i, j: (layer, 0, j)),
                  pl.BlockSpec((None, d, tf), lambda i, j: (layer, 0, nf + j)),
                  pl.BlockSpec((None, tf, d), lambda i, j: (layer, j, 0)),
                  pl.BlockSpec((1, d), lambda i, j: (0, 0))],
        out_specs=pl.BlockSpec((tm, d), lambda i, j: (i, 0)),
        out_shape=jax.ShapeDtypeStruct((m, d), F32),
        scratch_shapes=[pltpu.VMEM((tm, d), BF16), pltpu.VMEM((tm, 1), F32),
                        pltpu.VMEM((tm, tf), BF16)],
        compiler_params=_cparams(("parallel", "arbitrary")),
        name="ffn",
    )(x, gpre.reshape(1, d), w_in, w_in, w_out, gpost.reshape(1, d))


def _win_attn_kernel(sink_ref, q_ref, k_ref, v_ref, o_ref, *, seq, nblk):
    kwin = 3 * A_BLOCK
    cols = A_GROUP * A_BLOCK
    sink_values = jnp.concatenate([jnp.zeros((SINK_ROWS, A_HEAD_DIM), BF16),
                                   jnp.full((SINK_ROWS, A_HEAD_DIM), 1.0 / SINK_ROWS, BF16)], axis=1)
    for jb in range(nblk):
        blk = pl.program_id(1) * nblk + jb
        start = pl.multiple_of(jnp.clip((blk - 1) * A_BLOCK, 0, seq - kwin), A_BLOCK)
        kpos = start + lax.broadcasted_iota(jnp.int32, (kwin, cols), 0)
        qpos = blk * A_BLOCK + (lax.broadcasted_iota(jnp.int32, (kwin, cols), 1) & (A_BLOCK - 1))
        mask = jnp.where(jnp.abs(kpos - qpos) <= WINDOW, 0.0, NEG_BIG)
        qrows = slice(jb * A_BLOCK, (jb + 1) * A_BLOCK)
        for kv in range(A_KV_HEADS):
            heads = range(kv * A_GROUP, (kv + 1) * A_GROUP)
            kvcols = slice(kv * A_HEAD_DIM, (kv + 1) * A_HEAD_DIM)
            q4 = jnp.concatenate(
                [q_ref[0, qrows, h * A_HEAD_DIM:(h + 1) * A_HEAD_DIM] for h in heads], axis=0)
            kw = k_ref[0, pl.ds(start, kwin), kvcols]
            vw = v_ref[0, pl.ds(start, kwin), kvcols]
            st = _dot_nt(kw, q4) + mask
            sink = jnp.concatenate(
                [jnp.full((1, A_BLOCK), sink_ref[h] * LOG2E, F32) for h in heads], axis=1)
            mx = jnp.maximum(jnp.max(st, axis=0, keepdims=True), sink)
            sink_rows = jnp.broadcast_to(jnp.exp2(sink - mx), (SINK_ROWS, cols))
            pt = jnp.concatenate([jnp.exp2(st - mx), sink_rows], axis=0).astype(BF16)
            v_ext = jnp.concatenate(
                [jnp.concatenate([vw, jnp.ones_like(vw)], axis=1), sink_values], axis=0)
            o = lax.dot_general(pt, v_ext, (((0,), (0,)), ((), ())), preferred_element_type=F32)
            o = o[:, :A_HEAD_DIM] / o[:, A_HEAD_DIM:]
            for gi, h in enumerate(heads):
                o_ref[0, qrows, h * A_HEAD_DIM:(h + 1) * A_HEAD_DIM] = (
                    o[gi * A_BLOCK:(gi + 1) * A_BLOCK, :].astype(o_ref.dtype))


def window_attention(qkv, sinks, *, nblk):
    b, s, _ = qkv.shape
    dq = A_HEADS * A_HEAD_DIM
    dkv = A_KV_HEADS * A_HEAD_DIM
    tq = nblk * A_BLOCK
    return pl.pallas_call(
        functools.partial(_win_attn_kernel, seq=s, nblk=nblk),
        grid=(b, s // tq),
        in_specs=[pl.BlockSpec(memory_space=pltpu.SMEM),
                  pl.BlockSpec((1, tq, dq), lambda bi, i: (bi, i, 0)),
                  pl.BlockSpec((1, s, dkv), lambda bi, i: (bi, 0, dq // dkv)),
                  pl.BlockSpec((1, s, dkv), lambda bi, i: (bi, 0, dq // dkv + 1))],
        out_specs=pl.BlockSpec((1, tq, dq), lambda bi, i: (bi, i, 0)),
        out_shape=jax.ShapeDtypeStruct((b, s, dq), BF16),
        compiler_params=_cparams(("parallel", "arbitrary")),
        name="window_attention",
    )(sinks, qkv, qkv, qkv)


def _mla_down_kernel(x_ref, g_ref, wq_ref, wkv_ref, wkr_ref, qn_ref, kvn_ref, *rest, shift):
    *tabs, cq_ref, ckv_ref, kr_ref = rest
    h = _rms(x_ref[...], g_ref[...]).astype(BF16)
    cq_ref[...] = _rms(_dot(h, wq_ref[...]), qn_ref[...]).astype(BF16)
    ckv_ref[...] = _rms(_dot(h, wkv_ref[...]), kvn_ref[...]).astype(BF16)
    kr_ref[...] = _rope(_dot(h, wkr_ref[...]), tabs, shift).astype(BF16)


def mla_down(x, g, wq, wkv, wkr, qn, kvn, rope, *, seq, tm):
    m, d = x.shape
    shift, tabs = rope
    sblk = seq // tm
    tab_spec = pl.BlockSpec((tm, LANES), lambda i: (i % sblk, 0))
    full = lambda shape: pl.BlockSpec(shape, lambda i: (0, 0))
    widths = (B_Q_RANK, B_KV_RANK, LANES)
    return pl.pallas_call(
        functools.partial(_mla_down_kernel, shift=shift),
        grid=(m // tm,),
        in_specs=[pl.BlockSpec((tm, d), lambda i: (i, 0)), full((1, d)),
                  full((d, B_Q_RANK)), full((d, B_KV_RANK)), full((d, LANES)),
                  full((1, B_Q_RANK)), full((1, B_KV_RANK))] + [tab_spec] * len(tabs),
        out_specs=[pl.BlockSpec((tm, w), lambda i: (i, 0)) for w in widths],
        out_shape=[jax.ShapeDtypeStruct((m, w), BF16) for w in widths],
        compiler_params=_cparams(("parallel",)),
        name="mla_down",
    )(x, g.reshape(1, d), wq, wkv, wkr, qn.reshape(1, -1), kvn.reshape(1, -1), *tabs)


def _mla_attn_kernel(q_ref, kn_ref, kr_ref, v_ref, o_ref, *, chunk):
    k = jnp.concatenate([kn_ref[0], kr_ref[0]], axis=1)
    v = v_ref[0]
    v_ones = jnp.concatenate([v, jnp.ones_like(v)], axis=1)
    dv = v.shape[1]
    for c in range(q_ref.shape[1] // chunk):
        rows = slice(c * chunk, (c + 1) * chunk)
        s = _dot_nt(q_ref[0, rows, :], k)
        p = jnp.exp2(s - jnp.max(s, axis=-1, keepdims=True))
        o = _dot(p.astype(BF16), v_ones)
        o_ref[0, rows, :] = (o[:, :dv] / o[:, dv:]).astype(o_ref.dtype)


def mla_attention(qcat, kv, kr, *, tq, chunk):
    b, s, _ = qcat.shape
    kv_blocks = (B_NOPE_DIM + B_V_DIM) // B_V_DIM
    return pl.pallas_call(
        functools.partial(_mla_attn_kernel, chunk=chunk),
        grid=(b, B_HEADS, s // tq),
        in_specs=[pl.BlockSpec((1, tq, B_QK_PAD), lambda bi, h, qi: (bi, qi, h)),
                  pl.BlockSpec((1, s, B_NOPE_DIM), lambda bi, h, qi: (bi, 0, kv_blocks * h)),
                  pl.BlockSpec((1, s, LANES), lambda bi, h, qi: (bi, 0, 0)),
                  pl.BlockSpec((1, s, B_V_DIM), lambda bi, h, qi: (bi, 0, kv_blocks * h + 1))],
        out_specs=pl.BlockSpec((1, tq, B_V_DIM), lambda bi, h, qi: (bi, qi, h)),
        out_shape=jax.ShapeDtypeStruct((b, s, B_HEADS * B_V_DIM), BF16),
        compiler_params=_cparams(("parallel", "parallel", "arbitrary")),
        name="mla_attention",
    )(qcat, kv, kr, kv)


N_DR = 2 * NB_H - 1
SLAB_LOW = N_DR - 1
SLAB_HIGH = SLAB_LOW + N_DR
SLAB_NONE = SLAB_HIGH + N_DR


def _nb_slab_plan(rows):
    plan = []
    for rb in range(rows // C_QROWS):
        w0 = min(max(rb * C_QROWS - NB_H // 2, 0), rows - C_KROWS)
        block = []
        for i in range(C_QROWS):
            r = rb * C_QROWS + i
            rs = min(max(r - NB_H // 2, 0), rows - NB_H)
            ids = []
            for pair in range(C_KROWS // 2):
                kr = w0 + 2 * pair
                ok0, ok1 = rs <= kr < rs + NB_H, rs <= kr + 1 < rs + NB_H
                dr = kr - r + NB_H - 1
                ids.append(dr if ok0 and ok1 else SLAB_LOW + dr if ok0
                           else SLAB_HIGH + dr + 1 if ok1 else SLAB_NONE)
            block.append(tuple(ids))
        plan.append(tuple(block))
    return tuple(plan)


def _nb_attn_kernel(q_ref, k_ref, v_ref, t_ref, o_ref, *, plan, rows):
    tq = C_QROWS * GRID_W
    nk = C_KROWS * GRID_W
    for hh in range(C_HEADS_PER_STEP):
        cols = slice(hh * C_HEAD_DIM, (hh + 1) * C_HEAD_DIM)
        biases = {}
        for rb, block in enumerate(plan):
            if block not in biases:
                biases[block] = jnp.concatenate(
                    [jnp.concatenate([t_ref[hh, sid] for sid in ids], axis=1) for ids in block],
                    axis=0)
            start = min(max(rb * C_QROWS - NB_H // 2, 0), rows - C_KROWS) * GRID_W
            qrows = slice(rb * tq, (rb + 1) * tq)
            kw = k_ref[0, start:start + nk, cols]
            vw = v_ref[0, start:start + nk, cols]
            s = _dot_nt(q_ref[0, qrows, cols], kw) + biases[block]
            p = jnp.exp2(s - jnp.max(s, axis=-1, keepdims=True))
            o = _dot(p.astype(BF16), jnp.concatenate([vw, jnp.ones_like(vw)], axis=1))
            o_ref[0, qrows, cols] = (o[:, :C_HEAD_DIM] / o[:, C_HEAD_DIM:]).astype(o_ref.dtype)


def neighborhood_attention(qkv, slabs):
    b, s, _ = qkv.shape
    rows = s // GRID_W
    width = C_HEADS_PER_STEP * C_HEAD_DIM
    nhb = C_HEADS // C_HEADS_PER_STEP
    head_spec = lambda blk0: pl.BlockSpec((1, s, width), lambda h, bi: (bi, 0, blk0 + h))
    return pl.pallas_call(
        functools.partial(_nb_attn_kernel, plan=_nb_slab_plan(rows), rows=rows),
        grid=(nhb, b),
        in_specs=[head_spec(0), head_spec(nhb), head_spec(2 * nhb),
                  pl.BlockSpec((C_HEADS_PER_STEP,) + slabs.shape[1:], lambda h, bi: (h, 0, 0, 0))],
        out_specs=head_spec(0),
        out_shape=jax.ShapeDtypeStruct((b, s, C_HEADS * C_HEAD_DIM), BF16),
        compiler_params=_cparams(("parallel", "arbitrary")),
        name="neighborhood_attention",
    )(qkv, qkv, qkv, slabs)


def _nb_bias_slabs(rel_bias):
    c = np.arange(GRID_W)[:, None]
    kc = np.arange(GRID_W)[None, :]
    cs = np.clip(c - NB_W // 2, 0, GRID_W - NB_W)
    col_ok = (kc >= cs) & (kc < cs + NB_W)
    dc = np.clip(kc - c + NB_W - 1, 0, 2 * NB_W - 2)
    bias = jnp.where(col_ok[None, None], rel_bias.astype(F32)[:, :, dc] * LOG2E, NEG_BIG)
    masked = jnp.full_like(bias, NEG_BIG)
    return jnp.concatenate([jnp.concatenate([bias[:, :-1], bias[:, 1:]], axis=-1),
                            jnp.concatenate([bias, masked], axis=-1),
                            jnp.concatenate([masked, bias], axis=-1),
                            jnp.concatenate([masked[:, :1], masked[:, :1]], axis=-1)], axis=1)


def _rope_angles(seq, dim):
    pos = jnp.arange(seq, dtype=F32)
    inv = ROPE_THETA ** (-jnp.arange(0, dim, 2, dtype=F32) / dim)
    ang = pos[:, None] * inv[None, :]
    return jnp.cos(ang), jnp.sin(ang)


def _rope_adjacent(seq, dim):
    half = dim // 2
    c, sn = _rope_angles(seq, dim)
    cos = jnp.ones((seq, LANES), F32).at[:, :dim].set(jnp.concatenate([c, c], axis=1))
    up = jnp.zeros((seq, LANES), F32).at[:, half:dim].set(sn)
    down = jnp.zeros((seq, LANES), F32).at[:, :half].set(-sn)
    return half, (cos, up, down)


def _rope_swapped(seq, dim):
    half = dim // 2
    c, sn = _rope_angles(seq, dim)
    cos = jnp.ones((seq, LANES), F32).at[:, :half].set(c).at[:, ROPE_SWAP:ROPE_SWAP + half].set(c)
    sin = jnp.zeros((seq, LANES), F32).at[:, :half].set(-sn).at[:, ROPE_SWAP:ROPE_SWAP + half].set(sn)
    return ROPE_SWAP, (cos, sin)


def _pair_split(w, dim):
    half = dim // 2
    gap = ROPE_SWAP - half
    return jnp.concatenate([w[..., :half], w[..., dim:dim + gap], w[..., half:dim],
                            w[..., dim + gap:]], axis=-1)


def kernel(x, pre_mix_norm, post_mix_norm, pre_ffn_norm, post_ffn_norm, a_w_qkv, a_sinks, a_w_o,
           b_w_down, b_q_norm, b_w_uq, b_kv_norm, b_w_ukv, b_w_o, c_w_qkv, c_rel_bias, c_w_o,
           ffn_w_in, ffn_w_out):
    bsz, seq, d = x.shape
    m = bsz * seq
    rope_a = _rope_adjacent(seq, A_ROT_DIM)
    rope_b = _rope_swapped(seq, B_ROPE_DIM)
    w_in, w_out = ffn_w_in.astype(BF16), ffn_w_out
    xf = x.reshape(m, d)
    for i in range(DEPTH):
        kind, slot = i % N_MIXERS, i // N_MIXERS
        if kind == 0:
            ops = (((True, A_HEAD_DIM ** -0.5 * LOG2E),) * A_HEADS + ((True, 1.0),) * A_KV_HEADS
                   + (PLAIN,) * A_KV_HEADS)
            qkv = norm_proj(xf, pre_mix_norm[i], a_w_qkv.astype(BF16), rope_a, seq=seq, tm=TM,
                            tn=len(ops) * LANES, kinds=((0, 1, ops),), layer=slot)
            o = window_attention(qkv.reshape(bsz, seq, -1), a_sinks[slot].astype(F32),
                                 nblk=A_BLOCKS_PER_STEP)
            w_o = a_w_o
        elif kind == 1:
            wd = b_w_down[slot]
            wkr = jnp.pad(wd[:, B_Q_RANK + B_KV_RANK:], ((0, 0), (0, LANES - B_ROPE_DIM)))
            cq, ckv, kr = mla_down(xf, pre_mix_norm[i], wd[:, :B_Q_RANK].astype(BF16),
                                   wd[:, B_Q_RANK:B_Q_RANK + B_KV_RANK].astype(BF16),
                                   _pair_split(wkr, B_ROPE_DIM).astype(BF16),
                                   b_q_norm[slot], b_kv_norm[slot], rope_b, seq=seq, tm=TM)
            wuq = b_w_uq[slot].reshape(B_Q_RANK, B_HEADS, B_NOPE_DIM + B_ROPE_DIM)
            wuq_r = jnp.pad(wuq[:, :, B_NOPE_DIM:], ((0, 0), (0, 0), (0, LANES - B_ROPE_DIM)))
            wuq = jnp.concatenate([wuq[:, :, :B_NOPE_DIM], _pair_split(wuq_r, B_ROPE_DIM)], axis=2)
            scale = (B_NOPE_DIM + B_ROPE_DIM) ** -0.5 * LOG2E
            q_ops = ((False, scale), (True, scale)) * (MLA_TN // B_QK_PAD)
            qcat = proj(cq, wuq.reshape(B_Q_RANK, -1).astype(BF16), rope_b, seq=seq,
                        tm=TM, tn=MLA_TN, kinds=((0, 1, q_ops),))
            kv = proj(ckv, b_w_ukv[slot].astype(BF16), rope_b, seq=seq, tm=TM, tn=MLA_TN,
                      kinds=((0, 1, (PLAIN,) * (MLA_TN // LANES)),))
            o = mla_attention(qcat.reshape(bsz, seq, -1), kv.reshape(bsz, seq, -1),
                              kr.reshape(bsz, seq, -1), tq=seq, chunk=MLA_CHUNK)
            w_o = b_w_o
        else:
            tn = C_HEADS * C_HEAD_DIM
            q_ops = ((False, C_HEAD_DIM ** -0.5 * LOG2E),) * C_HEADS
            qkv = norm_proj(xf, pre_mix_norm[i], c_w_qkv.astype(BF16), rope_a, seq=seq, tm=TM,
                            tn=tn, kinds=((0, 1, q_ops), (1, 3, (PLAIN,) * C_HEADS)), layer=slot)
            o = neighborhood_attention(qkv.reshape(bsz, seq, -1), _nb_bias_slabs(c_rel_bias[slot]))
            w_o = c_w_o
        xf = out_proj_residual(o.reshape(m, -1), w_o.astype(BF16), post_mix_norm[i], xf,
                               layer=slot, tm=TM)
        xf = ffn(xf, pre_ffn_norm[i], w_in, w_out, post_ffn_norm[i], layer=i, tm=TM, tf=FFN_TF)
    return xf.reshape(bsz, seq, d)
```

```python
import functools

import jax
import jax.numpy as jnp
import numpy as np
from jax import lax
from jax.experimental import pallas as pl
from jax.experimental.pallas import tpu as pltpu

D_MODEL = 2048
DEPTH = 4
N_MIXERS = 3
GRID_W = 64
ROPE_THETA = 500000.0
NORM_EPS = 1e-6
D_FF = 5632

A_HEADS = 16
A_KV_HEADS = 4
A_GROUP = A_HEADS // A_KV_HEADS
A_HEAD_DIM = 128
A_ROT_DIM = 32
WINDOW = 128
A_BLOCK = 128

B_HEADS = 16
B_NOPE_DIM = 128
B_ROPE_DIM = 64
B_V_DIM = 128
B_Q_RANK = 512
B_KV_RANK = 512
B_QK_PAD = 256

C_HEADS = 16
C_HEAD_DIM = 128
NB_H = 8
NB_W = 16

LANES = 128
MXU_WIDTH = 256
V7X_VMEM_BYTES = 64 * 1024 * 1024
V7X_VMEM_RESERVE_BYTES = 4 * 1024 * 1024
V7X_VMEM_LIMIT_BYTES = V7X_VMEM_BYTES - V7X_VMEM_RESERVE_BYTES

TM = 1024
PROJ_CHUNK = 2 * MXU_WIDTH
FFN_PIECE = MXU_WIDTH
FFN_TF = 2 * FFN_PIECE
NORM_ROW_PIECES = 4
OUT_ROW_PIECES = 2
MLA_TN = 2048
MLA_CHUNK = 256
SINK_ROWS = 16
A_BLOCKS_PER_STEP = 8
C_QROWS = 8
C_KROWS = 16
C_HEADS_PER_STEP = 2
ROPE_SWAP = LANES // 2

NEG_BIG = -1e30
LOG2E = float(np.log2(np.e))
BF16 = jnp.bfloat16
F32 = jnp.float32
PLAIN = (False, 1.0)


def _cparams(semantics):
    return pltpu.CompilerParams(dimension_semantics=semantics,
                                vmem_limit_bytes=V7X_VMEM_LIMIT_BYTES)


def _rms(xf, g):
    ms = jnp.mean(xf * xf, axis=-1, keepdims=True)
    return xf * lax.rsqrt(ms + NORM_EPS) * g


def _inv_rms(src_ref, stat_ref):
    xf = src_ref[...]
    stat_ref[...] = lax.rsqrt(jnp.mean(xf * xf, axis=-1, keepdims=True) + NORM_EPS)
    return stat_ref[...]


def _dot(a, b):
    return jnp.dot(a, b, preferred_element_type=F32)


def _dot_nt(a, b):
    return lax.dot_general(a, b, (((1,), (1,)), ((), ())), preferred_element_type=F32)


def _rope(y, tabs, shift, rows=slice(None)):
    r = y * tabs[0][rows, :] + pltpu.roll(y, shift, 1) * tabs[1][rows, :]
    if shift != ROPE_SWAP:
        r = r + pltpu.roll(y, LANES - shift, 1) * tabs[2][rows, :]
    return r


def _project_tile(lhs, w_ref, o_ref, tabs, shift, ops, rows=slice(None)):
    per_chunk = PROJ_CHUNK // LANES
    for c0 in range(0, len(ops), per_chunk):
        chunk = slice(c0 * LANES, (c0 + per_chunk) * LANES)
        y = _dot(lhs, w_ref[:, chunk])
        if all(op == PLAIN for op in ops[c0:c0 + per_chunk]):
            o_ref[rows, chunk] = y.astype(o_ref.dtype)
            continue
        for c in range(per_chunk):
            rope, scale = ops[c0 + c]
            yc = y[:, c * LANES:(c + 1) * LANES]
            if scale != 1.0:
                yc = yc * scale
            if rope:
                yc = _rope(yc, tabs, shift, rows)
            o_ref[rows, (c0 + c) * LANES:(c0 + c + 1) * LANES] = yc.astype(o_ref.dtype)


def _project(lhs, w_ref, o_ref, tabs, shift, kinds):
    if len(kinds) == 1:
        _project_tile(lhs, w_ref, o_ref, tabs, shift, kinds[0][2])
        return
    j = pl.program_id(1)
    for lo, hi, ops in kinds:
        @pl.when((j >= lo) & (j < hi))
        def _(ops=ops):
            _project_tile(lhs, w_ref, o_ref, tabs, shift, ops)


def _weight_spec(w, tn, layer):
    k, n = w.shape[-2:]
    lead = () if layer is None else (layer,)
    shape = (None,) * len(lead) + (k, tn)
    if n == tn:
        return pl.BlockSpec(shape, lambda i, j: lead + (0, 0), pipeline_mode=pl.Buffered(1))
    return pl.BlockSpec(shape, lambda i, j: lead + (0, j))


def _norm_proj_kernel(x_ref, g_ref, w_ref, *rest, kinds, shift, n_tiles):
    *tabs, o_ref, h_ref = rest
    j = pl.program_id(1)
    piece = x_ref.shape[0] // NORM_ROW_PIECES

    @pl.when(j == 0)
    def _():
        for r in range(NORM_ROW_PIECES):
            rows = slice(r * piece, (r + 1) * piece)
            h = _rms(x_ref[rows, :], g_ref[...]).astype(BF16)
            if n_tiles > 1:
                h_ref[rows, :] = h
            _project_tile(h, w_ref, o_ref, tabs, shift, kinds[0][2], rows)

    for lo, hi, ops in kinds:
        if max(lo, 1) < hi:
            @pl.when((j >= max(lo, 1)) & (j < hi))
            def _(ops=ops):
                _project_tile(h_ref[...], w_ref, o_ref, tabs, shift, ops)


def norm_proj(x, g, w, rope, *, seq, tm, tn, kinds, layer=None):
    m, d = x.shape
    n = w.shape[-1]
    shift, tabs = rope
    sblk = seq // tm
    tab_spec = pl.BlockSpec((tm, LANES), lambda i, j: (i % sblk, 0))
    return pl.pallas_call(
        functools.partial(_norm_proj_kernel, kinds=kinds, shift=shift, n_tiles=n // tn),
        grid=(m // tm, n // tn),
        in_specs=[pl.BlockSpec((tm, d), lambda i, j: (i, 0)),
                  pl.BlockSpec((1, d), lambda i, j: (0, 0)),
                  _weight_spec(w, tn, layer)] + [tab_spec] * len(tabs),
        out_specs=pl.BlockSpec((tm, tn), lambda i, j: (i, j)),
        out_shape=jax.ShapeDtypeStruct((m, n), BF16),
        scratch_shapes=[pltpu.VMEM((tm, d), BF16)],
        compiler_params=_cparams(("parallel", "arbitrary")),
        name="norm_proj",
    )(x, g.reshape(1, d), w, *tabs)


def _proj_kernel(a_ref, w_ref, *rest, kinds, shift):
    *tabs, o_ref = rest
    _project(a_ref[...], w_ref, o_ref, tabs, shift, kinds)


def proj(a, w, rope, *, seq, tm, tn, kinds, layer=None):
    m, k = a.shape
    n = w.shape[-1]
    shift, tabs = rope
    sblk = seq // tm
    tab_spec = pl.BlockSpec((tm, LANES), lambda i, j: (i % sblk, 0))
    return pl.pallas_call(
        functools.partial(_proj_kernel, kinds=kinds, shift=shift),
        grid=(m // tm, n // tn),
        in_specs=[pl.BlockSpec((tm, k), lambda i, j: (i, 0)),
                  _weight_spec(w, tn, layer)] + [tab_spec] * len(tabs),
        out_specs=pl.BlockSpec((tm, tn), lambda i, j: (i, j)),
        out_shape=jax.ShapeDtypeStruct((m, n), BF16),
        compiler_params=_cparams(("parallel", "arbitrary")),
        name="proj",
    )(a, w, *tabs)


def _out_proj_kernel(a_ref, w_ref, g_ref, x_ref, o_ref):
    piece = a_ref.shape[0] // OUT_ROW_PIECES
    for r in range(OUT_ROW_PIECES):
        rows = slice(r * piece, (r + 1) * piece)
        a = a_ref[rows, :]
        m = jnp.concatenate([_dot(a, w_ref[:, c:c + PROJ_CHUNK])
                             for c in range(0, w_ref.shape[1], PROJ_CHUNK)], axis=1)
        o_ref[rows, :] = x_ref[rows, :] + _rms(m, g_ref[...])


def out_proj_residual(a, w, g, x, *, layer, tm):
    m, k = a.shape
    d = w.shape[-1]
    return pl.pallas_call(
        _out_proj_kernel,
        grid=(m // tm,),
        in_specs=[pl.BlockSpec((tm, k), lambda i: (i, 0)),
                  pl.BlockSpec((None, k, d), lambda i: (layer, 0, 0), pipeline_mode=pl.Buffered(1)),
                  pl.BlockSpec((1, d), lambda i: (0, 0)),
                  pl.BlockSpec((tm, d), lambda i: (i, 0))],
        out_specs=pl.BlockSpec((tm, d), lambda i: (i, 0)),
        out_shape=jax.ShapeDtypeStruct((m, d), F32),
        compiler_params=_cparams(("parallel",)),
        name="out_proj_residual",
    )(a, w, g.reshape(1, d), x)


def _ffn_kernel(x_ref, gpre_ref, wg_ref, wu_ref, wo_ref, gpost_ref, o_ref, h_ref, stat_ref,
                act_ref):
    j = pl.program_id(1)

    @pl.when(j == 0)
    def _():
        h_ref[...] = _rms(x_ref[...], gpre_ref[...]).astype(BF16)
        o_ref[...] = jnp.zeros_like(o_ref)

    h = h_ref[...]
    for c in range(0, act_ref.shape[1], FFN_PIECE):
        cols = slice(c, c + FFN_PIECE)
        gate = _dot(h, wg_ref[:, cols])
        up = _dot(h, wu_ref[:, cols])
        act_ref[:, cols] = (gate / (1.0 + jnp.exp(-gate)) * up).astype(BF16)
    o_ref[...] += _dot(act_ref[...], wo_ref[...].astype(BF16))

    @pl.when(j == pl.num_programs(1) - 1)
    def _():
        o_ref[...] = x_ref[...] + o_ref[...] * _inv_rms(o_ref, stat_ref) * gpost_ref[...]


def ffn(x, gpre, w_in, w_out, gpost, *, layer, tm, tf):
    m, d = x.shape
    dff = w_out.shape[1]
    nf = dff // tf
    return pl.pallas_call(
        _ffn_kernel,
        grid=(m // tm, nf),
        in_specs=[pl.BlockSpec((tm, d), lambda i, j: (i, 0)),
                  pl.BlockSpec((1, d), lambda i, j: (0, 0)),
                  pl.BlockSpec((None, d, tf), lambda i, j: (layer, 0, j)),
                  pl.BlockSpec((None, d, tf), lambda i, j: (layer, 0, nf + j)),
                  pl.BlockSpec((None, tf, d), lambda i, j: (layer, j, 0)),
                  pl.BlockSpec((1, d), lambda i, j: (0, 0))],
        out_specs=pl.BlockSpec((tm, d), lambda i, j: (i, 0)),
        out_shape=jax.ShapeDtypeStruct((m, d), F32),
        scratch_shapes=[pltpu.VMEM((tm, d), BF16), pltpu.VMEM((tm, 1), F32),
                        pltpu.VMEM((tm, tf), BF16)],
        compiler_params=_cparams(("parallel", "arbitrary")),
        name="ffn",
    )(x, gpre.reshape(1, d), w_in, w_in, w_out, gpost.reshape(1, d))


def _win_attn_kernel(sink_ref, q_ref, k_ref, v_ref, o_ref, *, seq, nblk):
    kwin = 3 * A_BLOCK
    cols = A_GROUP * A_BLOCK
    sink_values = jnp.concatenate([jnp.zeros((SINK_ROWS, A_HEAD_DIM), BF16),
                                   jnp.full((SINK_ROWS, A_HEAD_DIM), 1.0 / SINK_ROWS, BF16)], axis=1)
    for jb in range(nblk):
        blk = pl.program_id(1) * nblk + jb
        start = pl.multiple_of(jnp.clip((blk - 1) * A_BLOCK, 0, seq - kwin), A_BLOCK)
        kpos = start + lax.broadcasted_iota(jnp.int32, (kwin, cols), 0)
        qpos = blk * A_BLOCK + (lax.broadcasted_iota(jnp.int32, (kwin, cols), 1) & (A_BLOCK - 1))
        mask = jnp.where(jnp.abs(kpos - qpos) <= WINDOW, 0.0, NEG_BIG)
        qrows = slice(jb * A_BLOCK, (jb + 1) * A_BLOCK)
        for kv in range(A_KV_HEADS):
            heads = range(kv * A_GROUP, (kv + 1) * A_GROUP)
            kvcols = slice(kv * A_HEAD_DIM, (kv + 1) * A_HEAD_DIM)
            q4 = jnp.concatenate(
                [q_ref[0, qrows, h * A_HEAD_DIM:(h + 1) * A_HEAD_DIM] for h in heads], axis=0)
            kw = k_ref[0, pl.ds(start, kwin), kvcols]
            vw = v_ref[0, pl.ds(start, kwin), kvcols]
            st = _dot_nt(kw, q4) + mask
            sink = jnp.concatenate(
                [jnp.full((1, A_BLOCK), sink_ref[h] * LOG2E, F32) for h in heads], axis=1)
            mx = jnp.maximum(jnp.max(st, axis=0, keepdims=True), sink)
            sink_rows = jnp.broadcast_to(jnp.exp2(sink - mx), (SINK_ROWS, cols))
            pt = jnp.concatenate([jnp.exp2(st - mx), sink_rows], axis=0).astype(BF16)
            v_ext = jnp.concatenate(
                [jnp.concatenate([vw, jnp.ones_like(vw)], axis=1), sink_values], axis=0)
            o = lax.dot_general(pt, v_ext, (((0,), (0,)), ((), ())), preferred_element_type=F32)
            o = o[:, :A_HEAD_DIM] / o[:, A_HEAD_DIM:]
            for gi, h in enumerate(heads):
                o_ref[0, qrows, h * A_HEAD_DIM:(h + 1) * A_HEAD_DIM] = (
                    o[gi * A_BLOCK:(gi + 1) * A_BLOCK, :].astype(o_ref.dtype))


def window_attention(qkv, sinks, *, nblk):
    b, s, _ = qkv.shape
    dq = A_HEADS * A_HEAD_DIM
    dkv = A_KV_HEADS * A_HEAD_DIM
    tq = nblk * A_BLOCK
    return pl.pallas_call(
        functools.partial(_win_attn_kernel, seq=s, nblk=nblk),
        grid=(b, s // tq),
        in_specs=[pl.BlockSpec(memory_space=pltpu.SMEM),
                  pl.BlockSpec((1, tq, dq), lambda bi, i: (bi, i, 0)),
                  pl.BlockSpec((1, s, dkv), lambda bi, i: (bi, 0, dq // dkv)),
                  pl.BlockSpec((1, s, dkv), lambda bi, i: (bi, 0, dq // dkv + 1))],
        out_specs=pl.BlockSpec((1, tq, dq), lambda bi, i: (bi, i, 0)),
        out_shape=jax.ShapeDtypeStruct((b, s, dq), BF16),
        compiler_params=_cparams(("parallel", "arbitrary")),
        name="window_attention",
    )(sinks, qkv, qkv, qkv)


def _mla_down_kernel(x_ref, g_ref, wq_ref, wkv_ref, wkr_ref, qn_ref, kvn_ref, *rest, shift):
    *tabs, cq_ref, ckv_ref, kr_ref = rest
    h = _rms(x_ref[...], g_ref[...]).astype(BF16)
    cq_ref[...] = _rms(_dot(h, wq_ref[...]), qn_ref[...]).astype(BF16)
    ckv_ref[...] = _rms(_dot(h, wkv_ref[...]), kvn_ref[...]).astype(BF16)
    kr_ref[...] = _rope(_dot(h, wkr_ref[...]), tabs, shift).astype(BF16)


def mla_down(x, g, wq, wkv, wkr, qn, kvn, rope, *, seq, tm):
    m, d = x.shape
    shift, tabs = rope
    sblk = seq // tm
    tab_spec = pl.BlockSpec((tm, LANES), lambda i: (i % sblk, 0))
    full = lambda shape: pl.BlockSpec(shape, lambda i: (0, 0))
    widths = (B_Q_RANK, B_KV_RANK, LANES)
    return pl.pallas_call(
        functools.partial(_mla_down_kernel, shift=shift),
        grid=(m // tm,),
        in_specs=[pl.BlockSpec((tm, d), lambda i: (i, 0)), full((1, d)),
                  full((d, B_Q_RANK)), full((d, B_KV_RANK)), full((d, LANES)),
                  full((1, B_Q_RANK)), full((1, B_KV_RANK))] + [tab_spec] * len(tabs),
        out_specs=[pl.BlockSpec((tm, w), lambda i: (i, 0)) for w in widths],
        out_shape=[jax.ShapeDtypeStruct((m, w), BF16) for w in widths],
        compiler_params=_cparams(("parallel",)),
        name="mla_down",
    )(x, g.reshape(1, d), wq, wkv, wkr, qn.reshape(1, -1), kvn.reshape(1, -1), *tabs)


def _mla_attn_kernel(q_ref, kn_ref, kr_ref, v_ref, o_ref, *, chunk):
    k = jnp.concatenate([kn_ref[0], kr_ref[0]], axis=1)
    v = v_ref[0]
    v_ones = jnp.concatenate([v, jnp.ones_like(v)], axis=1)
    dv = v.shape[1]
    for c in range(q_ref.shape[1] // chunk):
        rows = slice(c * chunk, (c + 1) * chunk)
        s = _dot_nt(q_ref[0, rows, :], k)
        p = jnp.exp2(s - jnp.max(s, axis=-1, keepdims=True))
        o = _dot(p.astype(BF16), v_ones)
        o_ref[0, rows, :] = (o[:, :dv] / o[:, dv:]).astype(o_ref.dtype)


def mla_attention(qcat, kv, kr, *, tq, chunk):
    b, s, _ = qcat.shape
    kv_blocks = (B_NOPE_DIM + B_V_DIM) // B_V_DIM
    return pl.pallas_call(
        functools.partial(_mla_attn_kernel, chunk=chunk),
        grid=(b, B_HEADS, s // tq),
        in_specs=[pl.BlockSpec((1, tq, B_QK_PAD), lambda bi, h, qi: (bi, qi, h)),
                  pl.BlockSpec((1, s, B_NOPE_DIM), lambda bi, h, qi: (bi, 0, kv_blocks * h)),
                  pl.BlockSpec((1, s, LANES), lambda bi, h, qi: (bi, 0, 0)),
                  pl.BlockSpec((1, s, B_V_DIM), lambda bi, h, qi: (bi, 0, kv_blocks * h + 1))],
        out_specs=pl.BlockSpec((1, tq, B_V_DIM), lambda bi, h, qi: (bi, qi, h)),
        out_shape=jax.ShapeDtypeStruct((b, s, B_HEADS * B_V_DIM), BF16),
        compiler_params=_cparams(("parallel", "parallel", "arbitrary")),
        name="mla_attention",
    )(qcat, kv, kr, kv)


N_DR = 2 * NB_H - 1
SLAB_LOW = N_DR - 1
SLAB_HIGH = SLAB_LOW + N_DR
SLAB_NONE = SLAB_HIGH + N_DR


def _nb_slab_plan(rows):
    plan = []
    for rb in range(rows // C_QROWS):
        w0 = min(max(rb * C_QROWS - NB_H // 2, 0), rows - C_KROWS)
        block = []
        for i in range(C_QROWS):
            r = rb * C_QROWS + i
            rs = min(max(r - NB_H // 2, 0), rows - NB_H)
            ids = []
            for pair in range(C_KROWS // 2):
                kr = w0 + 2 * pair
                ok0, ok1 = rs <= kr < rs + NB_H, rs <= kr + 1 < rs + NB_H
                dr = kr - r + NB_H - 1
                ids.append(dr if ok0 and ok1 else SLAB_LOW + dr if ok0
                           else SLAB_HIGH + dr + 1 if ok1 else SLAB_NONE)
            block.append(tuple(ids))
        plan.append(tuple(block))
    return tuple(plan)


def _nb_attn_kernel(q_ref, k_ref, v_ref, t_ref, o_ref, *, plan, rows):
    tq = C_QROWS * GRID_W
    nk = C_KROWS * GRID_W
    for hh in range(C_HEADS_PER_STEP):
        cols = slice(hh * C_HEAD_DIM, (hh + 1) * C_HEAD_DIM)
        biases = {}
        for rb, block in enumerate(plan):
            if block not in biases:
                biases[block] = jnp.concatenate(
                    [jnp.concatenate([t_ref[hh, sid] for sid in ids], axis=1) for ids in block],
                    axis=0)
            start = min(max(rb * C_QROWS - NB_H // 2, 0), rows - C_KROWS) * GRID_W
            qrows = slice(rb * tq, (rb + 1) * tq)
            kw = k_ref[0, start:start + nk, cols]
            vw = v_ref[0, start:start + nk, cols]
            s = _dot_nt(q_ref[0, qrows, cols], kw) + biases[block]
            p = jnp.exp2(s - jnp.max(s, axis=-1, keepdims=True))
            o = _dot(p.astype(BF16), jnp.concatenate([vw, jnp.ones_like(vw)], axis=1))
            o_ref[0, qrows, cols] = (o[:, :C_HEAD_DIM] / o[:, C_HEAD_DIM:]).astype(o_ref.dtype)


def neighborhood_attention(qkv, slabs):
    b, s, _ = qkv.shape
    rows = s // GRID_W
    width = C_HEADS_PER_STEP * C_HEAD_DIM
    nhb = C_HEADS // C_HEADS_PER_STEP
    head_spec = lambda blk0: pl.BlockSpec((1, s, width), lambda h, bi: (bi, 0, blk0 + h))
    return pl.pallas_call(
        functools.partial(_nb_attn_kernel, plan=_nb_slab_plan(rows), rows=rows),
        grid=(nhb, b),
        in_specs=[head_spec(0), head_spec(nhb), head_spec(2 * nhb),
                  pl.BlockSpec((C_HEADS_PER_STEP,) + slabs.shape[1:], lambda h, bi: (h, 0, 0, 0))],
        out_specs=head_spec(0),
        out_shape=jax.ShapeDtypeStruct((b, s, C_HEADS * C_HEAD_DIM), BF16),
        compiler_params=_cparams(("parallel", "arbitrary")),
        name="neighborhood_attention",
    )(qkv, qkv, qkv, slabs)


def _nb_bias_slabs(rel_bias):
    c = np.arange(GRID_W)[:, None]
    kc = np.arange(GRID_W)[None, :]
    cs = np.clip(c - NB_W // 2, 0, GRID_W - NB_W)
    col_ok = (kc >= cs) & (kc < cs + NB_W)
    period = 2 * GRID_W
    lead = GRID_W - NB_W
    padded = jnp.pad(rel_bias.astype(F32) * LOG2E,
                     ((0, 0), (0, 0), (lead, period - lead - rel_bias.shape[-1])))
    shifts = jnp.tile(padded, (1, 1, GRID_W + 1))[..., :GRID_W * (period + 1)]
    shifts = shifts.reshape(shifts.shape[:2] + (GRID_W, period + 1))[..., :GRID_W]
    bias = jnp.where(col_ok[None, None], shifts[:, :, ::-1, :], NEG_BIG)
    masked = jnp.full_like(bias, NEG_BIG)
    return jnp.concatenate([jnp.concatenate([bias[:, :-1], bias[:, 1:]], axis=-1),
                            jnp.concatenate([bias, masked], axis=-1),
                            jnp.concatenate([masked, bias], axis=-1),
                            jnp.concatenate([masked[:, :1], masked[:, :1]], axis=-1)], axis=1)


def _rope_angles(seq, dim):
    pos = jnp.arange(seq, dtype=F32)
    inv = ROPE_THETA ** (-jnp.arange(0, dim, 2, dtype=F32) / dim)
    ang = pos[:, None] * inv[None, :]
    return jnp.cos(ang), jnp.sin(ang)


def _rope_adjacent(seq, dim):
    half = dim // 2
    c, sn = _rope_angles(seq, dim)
    cos = jnp.ones((seq, LANES), F32).at[:, :dim].set(jnp.concatenate([c, c], axis=1))
    up = jnp.zeros((seq, LANES), F32).at[:, half:dim].set(sn)
    down = jnp.zeros((seq, LANES), F32).at[:, :half].set(-sn)
    return half, (cos, up, down)


def _rope_swapped(seq, dim):
    half = dim // 2
    c, sn = _rope_angles(seq, dim)
    cos = jnp.ones((seq, LANES), F32).at[:, :half].set(c).at[:, ROPE_SWAP:ROPE_SWAP + half].set(c)
    sin = jnp.zeros((seq, LANES), F32).at[:, :half].set(-sn).at[:, ROPE_SWAP:ROPE_SWAP + half].set(sn)
    return ROPE_SWAP, (cos, sin)


def _pair_split(w, dim):
    half = dim // 2
    gap = ROPE_SWAP - half
    return jnp.concatenate([w[..., :half], w[..., dim:dim + gap], w[..., half:dim],
                            w[..., dim + gap:]], axis=-1)


def kernel(x, pre_mix_norm, post_mix_norm, pre_ffn_norm, post_ffn_norm, a_w_qkv, a_sinks, a_w_o,
           b_w_down, b_q_norm, b_w_uq, b_kv_norm, b_w_ukv, b_w_o, c_w_qkv, c_rel_bias, c_w_o,
           ffn_w_in, ffn_w_out):
    bsz, seq, d = x.shape
    m = bsz * seq
    rope_a = _rope_adjacent(seq, A_ROT_DIM)
    rope_b = _rope_swapped(seq, B_ROPE_DIM)
    w_in, w_out = ffn_w_in.astype(BF16), ffn_w_out
    xf = x.reshape(m, d)
    for i in range(DEPTH):
        kind, slot = i % N_MIXERS, i // N_MIXERS
        if kind == 0:
            ops = (((True, A_HEAD_DIM ** -0.5 * LOG2E),) * A_HEADS + ((True, 1.0),) * A_KV_HEADS
                   + (PLAIN,) * A_KV_HEADS)
            qkv = norm_proj(xf, pre_mix_norm[i], a_w_qkv.astype(BF16), rope_a, seq=seq, tm=TM,
                            tn=len(ops) * LANES, kinds=((0, 1, ops),), layer=slot)
            o = window_attention(qkv.reshape(bsz, seq, -1), a_sinks[slot].astype(F32),
                                 nblk=A_BLOCKS_PER_STEP)
            w_o = a_w_o
        elif kind == 1:
            wd = b_w_down[slot]
            wkr = jnp.pad(wd[:, B_Q_RANK + B_KV_RANK:], ((0, 0), (0, LANES - B_ROPE_DIM)))
            cq, ckv, kr = mla_down(xf, pre_mix_norm[i], wd[:, :B_Q_RANK].astype(BF16),
                                   wd[:, B_Q_RANK:B_Q_RANK + B_KV_RANK].astype(BF16),
                                   _pair_split(wkr, B_ROPE_DIM).astype(BF16),
                                   b_q_norm[slot], b_kv_norm[slot], rope_b, seq=seq, tm=TM)
            wuq = b_w_uq[slot].reshape(B_Q_RANK, B_HEADS, B_NOPE_DIM + B_ROPE_DIM)
            wuq_r = jnp.pad(wuq[:, :, B_NOPE_DIM:], ((0, 0), (0, 0), (0, LANES - B_ROPE_DIM)))
            wuq = jnp.concatenate([wuq[:, :, :B_NOPE_DIM], _pair_split(wuq_r, B_ROPE_DIM)], axis=2)
            scale = (B_NOPE_DIM + B_ROPE_DIM) ** -0.5 * LOG2E
            q_ops = ((False, scale), (True, scale)) * (MLA_TN // B_QK_PAD)
            qcat = proj(cq, wuq.reshape(B_Q_RANK, -1).astype(BF16), rope_b, seq=seq,
                        tm=TM, tn=MLA_TN, kinds=((0, 1, q_ops),))
            kv = proj(ckv, b_w_ukv[slot].astype(BF16), rope_b, seq=seq, tm=TM, tn=MLA_TN,
                      kinds=((0, 1, (PLAIN,) * (MLA_TN // LANES)),))
            o = mla_attention(qcat.reshape(bsz, seq, -1), kv.reshape(bsz, seq, -1),
                              kr.reshape(bsz, seq, -1), tq=seq, chunk=MLA_CHUNK)
            w_o = b_w_o
        else:
            tn = C_HEADS * C_HEAD_DIM
            q_ops = ((False, C_HEAD_DIM ** -0.5 * LOG2E),) * C_HEADS
            qkv = norm_proj(xf, pre_mix_norm[i], c_w_qkv.astype(BF16), rope_a, seq=seq, tm=TM,
                            tn=tn, kinds=((0, 1, q_ops), (1, 3, (PLAIN,) * C_HEADS)), layer=slot)
            o = neighborhood_attention(qkv.reshape(bsz, seq, -1), _nb_bias_slabs(c_rel_bias[slot]))
            w_o = c_w_o
        xf = out_proj_residual(o.reshape(m, -1), w_o.astype(BF16), post_mix_norm[i], xf,
                               layer=slot, tm=TM)
        xf = ffn(xf, pre_ffn_norm[i], w_in, w_out, post_ffn_norm[i], layer=i, tm=TM, tf=FFN_TF)
    return xf.reshape(bsz, seq, d)
```

```python
import functools

import jax
import jax.numpy as jnp
import numpy as np
from jax import lax
from jax.experimental import pallas as pl
from jax.experimental.pallas import tpu as pltpu

D_MODEL = 2048
DEPTH = 4
N_MIXERS = 3
GRID_W = 64
ROPE_THETA = 500000.0
NORM_EPS = 1e-6
D_FF = 5632

A_HEADS = 16
A_KV_HEADS = 4
A_GROUP = A_HEADS // A_KV_HEADS
A_HEAD_DIM = 128
A_ROT_DIM = 32
WINDOW = 128
A_BLOCK = 128

B_HEADS = 16
B_NOPE_DIM = 128
B_ROPE_DIM = 64
B_V_DIM = 128
B_Q_RANK = 512
B_KV_RANK = 512
B_QK_PAD = 256

C_HEADS = 16
C_HEAD_DIM = 128
NB_H = 8
NB_W = 16

LANES = 128
MXU_WIDTH = 256
V7X_VMEM_BYTES = 64 * 1024 * 1024
V7X_VMEM_RESERVE_BYTES = 4 * 1024 * 1024
V7X_VMEM_LIMIT_BYTES = V7X_VMEM_BYTES - V7X_VMEM_RESERVE_BYTES

TM = 1024
PROJ_CHUNK = 2 * MXU_WIDTH
FFN_PIECE = MXU_WIDTH
FFN_TF = 2 * FFN_PIECE
NORM_ROW_PIECES = 2
OUT_ROW_PIECES = 2
MLA_TN = 2048
MLA_CHUNK = 256
SINK_ROWS = 16
A_BLOCKS_PER_STEP = 8
C_QROWS = 8
C_KROWS = 16
C_HEADS_PER_STEP = 2
ROPE_SWAP = LANES // 2

NEG_BIG = -1e30
LOG2E = float(np.log2(np.e))
BF16 = jnp.bfloat16
F32 = jnp.float32
PLAIN = (False, 1.0)


def _cparams(semantics):
    return pltpu.CompilerParams(dimension_semantics=semantics,
                                vmem_limit_bytes=V7X_VMEM_LIMIT_BYTES)


def _rms(xf, g):
    ms = jnp.mean(xf * xf, axis=-1, keepdims=True)
    return xf * lax.rsqrt(ms + NORM_EPS) * g


def _inv_rms(src_ref, stat_ref):
    xf = src_ref[...]
    stat_ref[...] = lax.rsqrt(jnp.mean(xf * xf, axis=-1, keepdims=True) + NORM_EPS)
    return stat_ref[...]


def _dot(a, b):
    return jnp.dot(a, b, preferred_element_type=F32)


def _dot_nt(a, b):
    return lax.dot_general(a, b, (((1,), (1,)), ((), ())), preferred_element_type=F32)


def _rope(y, tabs, shift, rows=slice(None)):
    r = y * tabs[0][rows, :] + pltpu.roll(y, shift, 1) * tabs[1][rows, :]
    if shift != ROPE_SWAP:
        r = r + pltpu.roll(y, LANES - shift, 1) * tabs[2][rows, :]
    return r


def _project_tile(lhs, w_ref, o_ref, tabs, shift, ops, rows=slice(None)):
    per_chunk = PROJ_CHUNK // LANES
    for c0 in range(0, len(ops), per_chunk):
        chunk = slice(c0 * LANES, (c0 + per_chunk) * LANES)
        y = _dot(lhs, w_ref[:, chunk])
        if all(op == PLAIN for op in ops[c0:c0 + per_chunk]):
            o_ref[rows, chunk] = y.astype(o_ref.dtype)
            continue
        for c in range(per_chunk):
            rope, scale = ops[c0 + c]
            yc = y[:, c * LANES:(c + 1) * LANES]
            if scale != 1.0:
                yc = yc * scale
            if rope:
                yc = _rope(yc, tabs, shift, rows)
            o_ref[rows, (c0 + c) * LANES:(c0 + c + 1) * LANES] = yc.astype(o_ref.dtype)


def _project(lhs, w_ref, o_ref, tabs, shift, kinds):
    if len(kinds) == 1:
        _project_tile(lhs, w_ref, o_ref, tabs, shift, kinds[0][2])
        return
    j = pl.program_id(1)
    for lo, hi, ops in kinds:
        @pl.when((j >= lo) & (j < hi))
        def _(ops=ops):
            _project_tile(lhs, w_ref, o_ref, tabs, shift, ops)


def _weight_spec(w, tn, layer):
    k, n = w.shape[-2:]
    lead = () if layer is None else (layer,)
    shape = (None,) * len(lead) + (k, tn)
    if n == tn:
        return pl.BlockSpec(shape, lambda i, j: lead + (0, 0), pipeline_mode=pl.Buffered(1))
    return pl.BlockSpec(shape, lambda i, j: lead + (0, j))


def _norm_proj_kernel(x_ref, g_ref, w_ref, *rest, kinds, shift, n_tiles):
    *tabs, o_ref, h_ref = rest
    j = pl.program_id(1)
    piece = x_ref.shape[0] // NORM_ROW_PIECES

    @pl.when(j == 0)
    def _():
        for r in range(NORM_ROW_PIECES):
            rows = slice(r * piece, (r + 1) * piece)
            h = _rms(x_ref[rows, :], g_ref[...]).astype(BF16)
            if n_tiles > 1:
                h_ref[rows, :] = h
            _project_tile(h, w_ref, o_ref, tabs, shift, kinds[0][2], rows)

    for lo, hi, ops in kinds:
        if max(lo, 1) < hi:
            @pl.when((j >= max(lo, 1)) & (j < hi))
            def _(ops=ops):
                _project_tile(h_ref[...], w_ref, o_ref, tabs, shift, ops)


def norm_proj(x, g, w, rope, *, seq, tm, tn, kinds, layer=None):
    m, d = x.shape
    n = w.shape[-1]
    shift, tabs = rope
    sblk = seq // tm
    tab_spec = pl.BlockSpec((tm, LANES), lambda i, j: (i % sblk, 0))
    return pl.pallas_call(
        functools.partial(_norm_proj_kernel, kinds=kinds, shift=shift, n_tiles=n // tn),
        grid=(m // tm, n // tn),
        in_specs=[pl.BlockSpec((tm, d), lambda i, j: (i, 0)),
                  pl.BlockSpec((1, d), lambda i, j: (0, 0)),
                  _weight_spec(w, tn, layer)] + [tab_spec] * len(tabs),
        out_specs=pl.BlockSpec((tm, tn), lambda i, j: (i, j)),
        out_shape=jax.ShapeDtypeStruct((m, n), BF16),
        scratch_shapes=[pltpu.VMEM((tm, d), BF16)],
        compiler_params=_cparams(("parallel", "arbitrary")),
        name="norm_proj",
    )(x, g.reshape(1, d), w, *tabs)


def _proj_kernel(a_ref, w_ref, *rest, kinds, shift):
    *tabs, o_ref = rest
    _project(a_ref[...], w_ref, o_ref, tabs, shift, kinds)


def proj(a, w, rope, *, seq, tm, tn, kinds, layer=None):
    m, k = a.shape
    n = w.shape[-1]
    shift, tabs = rope
    sblk = seq // tm
    tab_spec = pl.BlockSpec((tm, LANES), lambda i, j: (i % sblk, 0))
    return pl.pallas_call(
        functools.partial(_proj_kernel, kinds=kinds, shift=shift),
        grid=(m // tm, n // tn),
        in_specs=[pl.BlockSpec((tm, k), lambda i, j: (i, 0)),
                  _weight_spec(w, tn, layer)] + [tab_spec] * len(tabs),
        out_specs=pl.BlockSpec((tm, tn), lambda i, j: (i, j)),
        out_shape=jax.ShapeDtypeStruct((m, n), BF16),
        compiler_params=_cparams(("parallel", "arbitrary")),
        name="proj",
    )(a, w, *tabs)


def _out_proj_kernel(a_ref, w_ref, g_ref, x_ref, o_ref):
    piece = a_ref.shape[0] // OUT_ROW_PIECES
    for r in range(OUT_ROW_PIECES):
        rows = slice(r * piece, (r + 1) * piece)
        a = a_ref[rows, :]
        m = jnp.concatenate([_dot(a, w_ref[:, c:c + PROJ_CHUNK])
                             for c in range(0, w_ref.shape[1], PROJ_CHUNK)], axis=1)
        o_ref[rows, :] = x_ref[rows, :] + _rms(m, g_ref[...])


def out_proj_residual(a, w, g, x, *, layer, tm):
    m, k = a.shape
    d = w.shape[-1]
    return pl.pallas_call(
        _out_proj_kernel,
        grid=(m // tm,),
        in_specs=[pl.BlockSpec((tm, k), lambda i: (i, 0)),
                  pl.BlockSpec((None, k, d), lambda i: (layer, 0, 0), pipeline_mode=pl.Buffered(1)),
                  pl.BlockSpec((1, d), lambda i: (0, 0)),
                  pl.BlockSpec((tm, d), lambda i: (i, 0))],
        out_specs=pl.BlockSpec((tm, d), lambda i: (i, 0)),
        out_shape=jax.ShapeDtypeStruct((m, d), F32),
        compiler_params=_cparams(("parallel",)),
        name="out_proj_residual",
    )(a, w, g.reshape(1, d), x)


def _ffn_kernel(x_ref, gpre_ref, wg_ref, wu_ref, wo_ref, gpost_ref, o_ref, h_ref, stat_ref,
                act_ref):
    j = pl.program_id(1)

    def chunk(first):
        h = h_ref[...]
        for c in range(0, act_ref.shape[1], FFN_PIECE):
            cols = slice(c, c + FFN_PIECE)
            gate = _dot(h, wg_ref[:, cols])
            up = _dot(h, wu_ref[:, cols])
            act_ref[:, cols] = (gate / (1.0 + jnp.exp(-gate)) * up).astype(BF16)
        down = _dot(act_ref[...], wo_ref[...].astype(BF16))
        o_ref[...] = down if first else o_ref[...] + down

    @pl.when(j == 0)
    def _():
        h_ref[...] = _rms(x_ref[...], gpre_ref[...]).astype(BF16)
        chunk(first=True)

    @pl.when(j > 0)
    def _():
        chunk(first=False)

    @pl.when(j == pl.num_programs(1) - 1)
    def _():
        o_ref[...] = x_ref[...] + o_ref[...] * _inv_rms(o_ref, stat_ref) * gpost_ref[...]


def ffn(x, gpre, w_in, w_out, gpost, *, layer, tm, tf):
    m, d = x.shape
    dff = w_out.shape[1]
    nf = dff // tf
    return pl.pallas_call(
        _ffn_kernel,
        grid=(m // tm, nf),
        in_specs=[pl.BlockSpec((tm, d), lambda i, j: (i, 0)),
                  pl.BlockSpec((1, d), lambda i, j: (0, 0)),
                  pl.BlockSpec((None, d, tf), lambda i, j: (layer, 0, j)),
                  pl.BlockSpec((None, d, tf), lambda i, j: (layer, 0, nf + j)),
                  pl.BlockSpec((None, tf, d), lambda i, j: (layer, j, 0)),
                  pl.BlockSpec((1, d), lambda i, j: (0, 0))],
        out_specs=pl.BlockSpec((tm, d), lambda i, j: (i, 0)),
        out_shape=jax.ShapeDtypeStruct((m, d), F32),
        scratch_shapes=[pltpu.VMEM((tm, d), BF16), pltpu.VMEM((tm, 1), F32),
                        pltpu.VMEM((tm, tf), BF16)],
        compiler_params=_cparams(("parallel", "arbitrary")),
        name="ffn",
    )(x, gpre.reshape(1, d), w_in, w_in, w_out, gpost.reshape(1, d))


def _win_attn_kernel(sink_ref, q_ref, k_ref, v_ref, o_ref, *, seq, nblk):
    kwin = 3 * A_BLOCK
    cols = A_GROUP * A_BLOCK
    sink_values = jnp.concatenate([jnp.zeros((SINK_ROWS, A_HEAD_DIM), BF16),
                                   jnp.full((SINK_ROWS, A_HEAD_DIM), 1.0 / SINK_ROWS, BF16)], axis=1)
    for jb in range(nblk):
        blk = pl.program_id(1) * nblk + jb
        start = pl.multiple_of(jnp.clip((blk - 1) * A_BLOCK, 0, seq - kwin), A_BLOCK)
        kpos = start + lax.broadcasted_iota(jnp.int32, (kwin, cols), 0)
        qpos = blk * A_BLOCK + (lax.broadcasted_iota(jnp.int32, (kwin, cols), 1) & (A_BLOCK - 1))
        mask = jnp.where(jnp.abs(kpos - qpos) <= WINDOW, 0.0, NEG_BIG)
        qrows = slice(jb * A_BLOCK, (jb + 1) * A_BLOCK)
        for kv in range(A_KV_HEADS):
            heads = range(kv * A_GROUP, (kv + 1) * A_GROUP)
            kvcols = slice(kv * A_HEAD_DIM, (kv + 1) * A_HEAD_DIM)
            q4 = jnp.concatenate(
                [q_ref[0, qrows, h * A_HEAD_DIM:(h + 1) * A_HEAD_DIM] for h in heads], axis=0)
            kw = k_ref[0, pl.ds(start, kwin), kvcols]
            vw = v_ref[0, pl.ds(start, kwin), kvcols]
            st = _dot_nt(kw, q4) + mask
            sink = jnp.concatenate(
                [jnp.full((1, A_BLOCK), sink_ref[h] * LOG2E, F32) for h in heads], axis=1)
            mx = jnp.maximum(jnp.max(st, axis=0, keepdims=True), sink)
            sink_rows = jnp.broadcast_to(jnp.exp2(sink - mx), (SINK_ROWS, cols))
            pt = jnp.concatenate([jnp.exp2(st - mx), sink_rows], axis=0).astype(BF16)
            v_ext = jnp.concatenate(
                [jnp.concatenate([vw, jnp.ones_like(vw)], axis=1), sink_values], axis=0)
            o = lax.dot_general(pt, v_ext, (((0,), (0,)), ((), ())), preferred_element_type=F32)
            o = o[:, :A_HEAD_DIM] / o[:, A_HEAD_DIM:]
            for gi, h in enumerate(heads):
                o_ref[0, qrows, h * A_HEAD_DIM:(h + 1) * A_HEAD_DIM] = (
                    o[gi * A_BLOCK:(gi + 1) * A_BLOCK, :].astype(o_ref.dtype))


def window_attention(qkv, sinks, *, nblk):
    b, s, _ = qkv.shape
    dq = A_HEADS * A_HEAD_DIM
    dkv = A_KV_HEADS * A_HEAD_DIM
    tq = nblk * A_BLOCK
    return pl.pallas_call(
        functools.partial(_win_attn_kernel, seq=s, nblk=nblk),
        grid=(b, s // tq),
        in_specs=[pl.BlockSpec(memory_space=pltpu.SMEM),
                  pl.BlockSpec((1, tq, dq), lambda bi, i: (bi, i, 0)),
                  pl.BlockSpec((1, s, dkv), lambda bi, i: (bi, 0, dq // dkv)),
                  pl.BlockSpec((1, s, dkv), lambda bi, i: (bi, 0, dq // dkv + 1))],
        out_specs=pl.BlockSpec((1, tq, dq), lambda bi, i: (bi, i, 0)),
        out_shape=jax.ShapeDtypeStruct((b, s, dq), BF16),
        compiler_params=_cparams(("parallel", "arbitrary")),
        name="window_attention",
    )(sinks, qkv, qkv, qkv)


def _mla_down_kernel(x_ref, g_ref, wq_ref, wkv_ref, wkr_ref, qn_ref, kvn_ref, *rest, shift):
    *tabs, cq_ref, ckv_ref, kr_ref = rest
    h = _rms(x_ref[...], g_ref[...]).astype(BF16)
    cq_ref[...] = _rms(_dot(h, wq_ref[...]), qn_ref[...]).astype(BF16)
    ckv_ref[...] = _rms(_dot(h, wkv_ref[...]), kvn_ref[...]).astype(BF16)
    kr_ref[...] = _rope(_dot(h, wkr_ref[...]), tabs, shift).astype(BF16)


def mla_down(x, g, wq, wkv, wkr, qn, kvn, rope, *, seq, tm):
    m, d = x.shape
    shift, tabs = rope
    sblk = seq // tm
    tab_spec = pl.BlockSpec((tm, LANES), lambda i: (i % sblk, 0))
    full = lambda shape: pl.BlockSpec(shape, lambda i: (0, 0))
    widths = (B_Q_RANK, B_KV_RANK, LANES)
    return pl.pallas_call(
        functools.partial(_mla_down_kernel, shift=shift),
        grid=(m // tm,),
        in_specs=[pl.BlockSpec((tm, d), lambda i: (i, 0)), full((1, d)),
                  full((d, B_Q_RANK)), full((d, B_KV_RANK)), full((d, LANES)),
                  full((1, B_Q_RANK)), full((1, B_KV_RANK))] + [tab_spec] * len(tabs),
        out_specs=[pl.BlockSpec((tm, w), lambda i: (i, 0)) for w in widths],
        out_shape=[jax.ShapeDtypeStruct((m, w), BF16) for w in widths],
        compiler_params=_cparams(("parallel",)),
        name="mla_down",
    )(x, g.reshape(1, d), wq, wkv, wkr, qn.reshape(1, -1), kvn.reshape(1, -1), *tabs)


def _mla_attn_kernel(q_ref, kn_ref, kr_ref, v_ref, o_ref, *, chunk):
    k = jnp.concatenate([kn_ref[0], kr_ref[0]], axis=1)
    v = v_ref[0]
    v_ones = jnp.concatenate([v, jnp.ones_like(v)], axis=1)
    dv = v.shape[1]
    for c in range(q_ref.shape[1] // chunk):
        rows = slice(c * chunk, (c + 1) * chunk)
        s = _dot_nt(q_ref[0, rows, :], k)
        p = jnp.exp2(s - jnp.max(s, axis=-1, keepdims=True))
        o = _dot(p.astype(BF16), v_ones)
        o_ref[0, rows, :] = (o[:, :dv] / o[:, dv:]).astype(o_ref.dtype)


def mla_attention(qcat, kv, kr, *, tq, chunk):
    b, s, _ = qcat.shape
    kv_blocks = (B_NOPE_DIM + B_V_DIM) // B_V_DIM
    return pl.pallas_call(
        functools.partial(_mla_attn_kernel, chunk=chunk),
        grid=(b, B_HEADS, s // tq),
        in_specs=[pl.BlockSpec((1, tq, B_QK_PAD), lambda bi, h, qi: (bi, qi, h)),
                  pl.BlockSpec((1, s, B_NOPE_DIM), lambda bi, h, qi: (bi, 0, kv_blocks * h)),
                  pl.BlockSpec((1, s, LANES), lambda bi, h, qi: (bi, 0, 0)),
                  pl.BlockSpec((1, s, B_V_DIM), lambda bi, h, qi: (bi, 0, kv_blocks * h + 1))],
        out_specs=pl.BlockSpec((1, tq, B_V_DIM), lambda bi, h, qi: (bi, qi, h)),
        out_shape=jax.ShapeDtypeStruct((b, s, B_HEADS * B_V_DIM), BF16),
        compiler_params=_cparams(("parallel", "parallel", "arbitrary")),
        name="mla_attention",
    )(qcat, kv, kr, kv)


N_DR = 2 * NB_H - 1
SLAB_LOW = N_DR - 1
SLAB_HIGH = SLAB_LOW + N_DR
SLAB_NONE = SLAB_HIGH + N_DR


def _nb_slab_plan(rows):
    plan = []
    for rb in range(rows // C_QROWS):
        w0 = min(max(rb * C_QROWS - NB_H // 2, 0), rows - C_KROWS)
        block = []
        for i in range(C_QROWS):
            r = rb * C_QROWS + i
            rs = min(max(r - NB_H // 2, 0), rows - NB_H)
            ids = []
            for pair in range(C_KROWS // 2):
                kr = w0 + 2 * pair
                ok0, ok1 = rs <= kr < rs + NB_H, rs <= kr + 1 < rs + NB_H
                dr = kr - r + NB_H - 1
                ids.append(dr if ok0 and ok1 else SLAB_LOW + dr if ok0
                           else SLAB_HIGH + dr + 1 if ok1 else SLAB_NONE)
            block.append(tuple(ids))
        plan.append(tuple(block))
    return tuple(plan)


def _nb_attn_kernel(q_ref, k_ref, v_ref, t_ref, o_ref, *, plan, rows):
    tq = C_QROWS * GRID_W
    nk = C_KROWS * GRID_W
    for hh in range(C_HEADS_PER_STEP):
        cols = slice(hh * C_HEAD_DIM, (hh + 1) * C_HEAD_DIM)
        biases = {}
        for rb, block in enumerate(plan):
            if block not in biases:
                biases[block] = jnp.concatenate(
                    [jnp.concatenate([t_ref[hh, sid] for sid in ids], axis=1) for ids in block],
                    axis=0)
            start = min(max(rb * C_QROWS - NB_H // 2, 0), rows - C_KROWS) * GRID_W
            qrows = slice(rb * tq, (rb + 1) * tq)
            kw = k_ref[0, start:start + nk, cols]
            vw = v_ref[0, start:start + nk, cols]
            s = _dot_nt(q_ref[0, qrows, cols], kw) + biases[block]
            p = jnp.exp2(s - jnp.max(s, axis=-1, keepdims=True))
            o = _dot(p.astype(BF16), jnp.concatenate([vw, jnp.ones_like(vw)], axis=1))
            o_ref[0, qrows, cols] = (o[:, :C_HEAD_DIM] / o[:, C_HEAD_DIM:]).astype(o_ref.dtype)


def neighborhood_attention(qkv, slabs):
    b, s, _ = qkv.shape
    rows = s // GRID_W
    width = C_HEADS_PER_STEP * C_HEAD_DIM
    nhb = C_HEADS // C_HEADS_PER_STEP
    head_spec = lambda blk0: pl.BlockSpec((1, s, width), lambda h, bi: (bi, 0, blk0 + h))
    return pl.pallas_call(
        functools.partial(_nb_attn_kernel, plan=_nb_slab_plan(rows), rows=rows),
        grid=(nhb, b),
        in_specs=[head_spec(0), head_spec(nhb), head_spec(2 * nhb),
                  pl.BlockSpec((C_HEADS_PER_STEP,) + slabs.shape[1:], lambda h, bi: (h, 0, 0, 0))],
        out_specs=head_spec(0),
        out_shape=jax.ShapeDtypeStruct((b, s, C_HEADS * C_HEAD_DIM), BF16),
        compiler_params=_cparams(("parallel", "arbitrary")),
        name="neighborhood_attention",
    )(qkv, qkv, qkv, slabs)


def _nb_bias_slabs(rel_bias):
    c = np.arange(GRID_W)[:, None]
    kc = np.arange(GRID_W)[None, :]
    cs = np.clip(c - NB_W // 2, 0, GRID_W - NB_W)
    col_ok = (kc >= cs) & (kc < cs + NB_W)
    dc = np.clip(kc - c + NB_W - 1, 0, 2 * NB_W - 2)
    bias = jnp.where(col_ok[None, None], rel_bias.astype(F32)[:, :, dc] * LOG2E, NEG_BIG)
    masked = jnp.full_like(bias, NEG_BIG)
    return jnp.concatenate([jnp.concatenate([bias[:, :-1], bias[:, 1:]], axis=-1),
                            jnp.concatenate([bias, masked], axis=-1),
                            jnp.concatenate([masked, bias], axis=-1),
                            jnp.concatenate([masked[:, :1], masked[:, :1]], axis=-1)], axis=1)


def _rope_angles(seq, dim):
    pos = jnp.arange(seq, dtype=F32)
    inv = ROPE_THETA ** (-jnp.arange(0, dim, 2, dtype=F32) / dim)
    ang = pos[:, None] * inv[None, :]
    return jnp.cos(ang), jnp.sin(ang)


def _rope_adjacent(seq, dim):
    half = dim // 2
    c, sn = _rope_angles(seq, dim)
    cos = jnp.ones((seq, LANES), F32).at[:, :dim].set(jnp.concatenate([c, c], axis=1))
    up = jnp.zeros((seq, LANES), F32).at[:, half:dim].set(sn)
    down = jnp.zeros((seq, LANES), F32).at[:, :half].set(-sn)
    return half, (cos, up, down)


def _rope_swapped(seq, dim):
    half = dim // 2
    c, sn = _rope_angles(seq, dim)
    cos = jnp.ones((seq, LANES), F32).at[:, :half].set(c).at[:, ROPE_SWAP:ROPE_SWAP + half].set(c)
    sin = jnp.zeros((seq, LANES), F32).at[:, :half].set(-sn).at[:, ROPE_SWAP:ROPE_SWAP + half].set(sn)
    return ROPE_SWAP, (cos, sin)


def _pair_split(w, dim):
    half = dim // 2
    gap = ROPE_SWAP - half
    return jnp.concatenate([w[..., :half], w[..., dim:dim + gap], w[..., half:dim],
                            w[..., dim + gap:]], axis=-1)


def kernel(x, pre_mix_norm, post_mix_norm, pre_ffn_norm, post_ffn_norm, a_w_qkv, a_sinks, a_w_o,
           b_w_down, b_q_norm, b_w_uq, b_kv_norm, b_w_ukv, b_w_o, c_w_qkv, c_rel_bias, c_w_o,
           ffn_w_in, ffn_w_out):
    bsz, seq, d = x.shape
    m = bsz * seq
    rope_a = _rope_adjacent(seq, A_ROT_DIM)
    rope_b = _rope_swapped(seq, B_ROPE_DIM)
    w_in, w_out = ffn_w_in.astype(BF16), ffn_w_out
    xf = x.reshape(m, d)
    for i in range(DEPTH):
        kind, slot = i % N_MIXERS, i // N_MIXERS
        if kind == 0:
            ops = (((True, A_HEAD_DIM ** -0.5 * LOG2E),) * A_HEADS + ((True, 1.0),) * A_KV_HEADS
                   + (PLAIN,) * A_KV_HEADS)
            qkv = norm_proj(xf, pre_mix_norm[i], a_w_qkv.astype(BF16), rope_a, seq=seq, tm=TM,
                            tn=len(ops) * LANES, kinds=((0, 1, ops),), layer=slot)
            o = window_attention(qkv.reshape(bsz, seq, -1), a_sinks[slot].astype(F32),
                                 nblk=A_BLOCKS_PER_STEP)
            w_o = a_w_o
        elif kind == 1:
            wd = b_w_down[slot]
            wkr = jnp.pad(wd[:, B_Q_RANK + B_KV_RANK:], ((0, 0), (0, LANES - B_ROPE_DIM)))
            cq, ckv, kr = mla_down(xf, pre_mix_norm[i], wd[:, :B_Q_RANK].astype(BF16),
                                   wd[:, B_Q_RANK:B_Q_RANK + B_KV_RANK].astype(BF16),
                                   _pair_split(wkr, B_ROPE_DIM).astype(BF16),
                                   b_q_norm[slot], b_kv_norm[slot], rope_b, seq=seq, tm=TM)
            wuq = b_w_uq[slot].reshape(B_Q_RANK, B_HEADS, B_NOPE_DIM + B_ROPE_DIM)
            wuq_r = jnp.pad(wuq[:, :, B_NOPE_DIM:], ((0, 0), (0, 0), (0, LANES - B_ROPE_DIM)))
            wuq = jnp.concatenate([wuq[:, :, :B_NOPE_DIM], _pair_split(wuq_r, B_ROPE_DIM)], axis=2)
            scale = (B_NOPE_DIM + B_ROPE_DIM) ** -0.5 * LOG2E
            q_ops = ((False, scale), (True, scale)) * (MLA_TN // B_QK_PAD)
            qcat = proj(cq, wuq.reshape(B_Q_RANK, -1).astype(BF16), rope_b, seq=seq,
                        tm=TM, tn=MLA_TN, kinds=((0, 1, q_ops),))
            kv = proj(ckv, b_w_ukv[slot].astype(BF16), rope_b, seq=seq, tm=TM, tn=MLA_TN,
                      kinds=((0, 1, (PLAIN,) * (MLA_TN // LANES)),))
            o = mla_attention(qcat.reshape(bsz, seq, -1), kv.reshape(bsz, seq, -1),
                              kr.reshape(bsz, seq, -1), tq=seq, chunk=MLA_CHUNK)
            w_o = b_w_o
        else:
            tn = C_HEADS * C_HEAD_DIM
            q_ops = ((False, C_HEAD_DIM ** -0.5 * LOG2E),) * C_HEADS
            qkv = norm_proj(xf, pre_mix_norm[i], c_w_qkv.astype(BF16), rope_a, seq=seq, tm=TM,
                            tn=tn, kinds=((0, 1, q_ops), (1, 3, (PLAIN,) * C_HEADS)), layer=slot)
            o = neighborhood_attention(qkv.reshape(bsz, seq, -1), _nb_bias_slabs(c_rel_bias[slot]))
            w_o = c_w_o
        xf = out_proj_residual(o.reshape(m, -1), w_o.astype(BF16), post_mix_norm[i], xf,
                               layer=slot, tm=TM)
        xf = ffn(xf, pre_ffn_norm[i], w_in, w_out, post_ffn_norm[i], layer=i, tm=TM, tf=FFN_TF)
    return xf.reshape(bsz, seq, d)
```

```python
import functools

import jax
import jax.numpy as jnp
import numpy as np
from jax import lax
from jax.experimental import pallas as pl
from jax.experimental.pallas import tpu as pltpu

D_MODEL = 2048
DEPTH = 4
N_MIXERS = 3
GRID_W = 64
ROPE_THETA = 500000.0
NORM_EPS = 1e-6
D_FF = 5632

A_HEADS = 16
A_KV_HEADS = 4
A_GROUP = A_HEADS // A_KV_HEADS
A_HEAD_DIM = 128
A_ROT_DIM = 32
WINDOW = 128
A_BLOCK = 128

B_HEADS = 16
B_NOPE_DIM = 128
B_ROPE_DIM = 64
B_V_DIM = 128
B_Q_RANK = 512
B_KV_RANK = 512
B_QK_PAD = 256

C_HEADS = 16
C_HEAD_DIM = 128
NB_H = 8
NB_W = 16

LANES = 128
MXU_WIDTH = 256
V7X_VMEM_BYTES = 64 * 1024 * 1024
V7X_VMEM_RESERVE_BYTES = 4 * 1024 * 1024
V7X_VMEM_LIMIT_BYTES = V7X_VMEM_BYTES - V7X_VMEM_RESERVE_BYTES

TM = 1024
PROJ_CHUNK = 2 * MXU_WIDTH
FFN_PIECE = MXU_WIDTH
FFN_TF = 2 * FFN_PIECE
NORM_ROW_PIECES = 2
OUT_ROW_PIECES = 2
MLA_TN = 2048
MLA_CHUNK = 256
SINK_ROWS = 16
A_BLOCKS_PER_STEP = 8
C_QROWS = 8
C_KROWS = 16
C_HEADS_PER_STEP = 2
ROPE_SWAP = LANES // 2

NEG_BIG = -1e30
LOG2E = float(np.log2(np.e))
BF16 = jnp.bfloat16
F32 = jnp.float32
PLAIN = (False, 1.0)


def _cparams(semantics):
    return pltpu.CompilerParams(dimension_semantics=semantics,
                                vmem_limit_bytes=V7X_VMEM_LIMIT_BYTES)


def _rms(xf, g):
    ms = jnp.mean(xf * xf, axis=-1, keepdims=True)
    return xf * lax.rsqrt(ms + NORM_EPS) * g


def _inv_rms(src_ref, stat_ref):
    xf = src_ref[...]
    stat_ref[...] = lax.rsqrt(jnp.mean(xf * xf, axis=-1, keepdims=True) + NORM_EPS)
    return stat_ref[...]


def _dot(a, b):
    return jnp.dot(a, b, preferred_element_type=F32)


def _dot_nt(a, b):
    return lax.dot_general(a, b, (((1,), (1,)), ((), ())), preferred_element_type=F32)


def _rope(y, tabs, shift, rows=slice(None)):
    r = y * tabs[0][rows, :] + pltpu.roll(y, shift, 1) * tabs[1][rows, :]
    if shift != ROPE_SWAP:
        r = r + pltpu.roll(y, LANES - shift, 1) * tabs[2][rows, :]
    return r


def _project_tile(lhs, w_ref, o_ref, tabs, shift, ops, rows=slice(None)):
    per_chunk = PROJ_CHUNK // LANES
    for c0 in range(0, len(ops), per_chunk):
        chunk = slice(c0 * LANES, (c0 + per_chunk) * LANES)
        y = _dot(lhs, w_ref[:, chunk])
        if all(op == PLAIN for op in ops[c0:c0 + per_chunk]):
            o_ref[rows, chunk] = y.astype(o_ref.dtype)
            continue
        for c in range(per_chunk):
            rope, scale = ops[c0 + c]
            yc = y[:, c * LANES:(c + 1) * LANES]
            if scale != 1.0:
                yc = yc * scale
            if rope:
                yc = _rope(yc, tabs, shift, rows)
            o_ref[rows, (c0 + c) * LANES:(c0 + c + 1) * LANES] = yc.astype(o_ref.dtype)


def _project(lhs, w_ref, o_ref, tabs, shift, kinds):
    if len(kinds) == 1:
        _project_tile(lhs, w_ref, o_ref, tabs, shift, kinds[0][2])
        return
    j = pl.program_id(1)
    for lo, hi, ops in kinds:
        @pl.when((j >= lo) & (j < hi))
        def _(ops=ops):
            _project_tile(lhs, w_ref, o_ref, tabs, shift, ops)


def _weight_spec(w, tn, layer):
    k, n = w.shape[-2:]
    lead = () if layer is None else (layer,)
    shape = (None,) * len(lead) + (k, tn)
    if n == tn:
        return pl.BlockSpec(shape, lambda i, j: lead + (0, 0), pipeline_mode=pl.Buffered(1))
    return pl.BlockSpec(shape, lambda i, j: lead + (0, j))


def _norm_proj_kernel(x_ref, g_ref, w_ref, *rest, kinds, shift, n_tiles):
    *tabs, o_ref, h_ref = rest
    j = pl.program_id(1)
    piece = x_ref.shape[0] // NORM_ROW_PIECES

    @pl.when(j == 0)
    def _():
        for r in range(NORM_ROW_PIECES):
            rows = slice(r * piece, (r + 1) * piece)
            h = _rms(x_ref[rows, :], g_ref[...]).astype(BF16)
            if n_tiles > 1:
                h_ref[rows, :] = h
            _project_tile(h, w_ref, o_ref, tabs, shift, kinds[0][2], rows)

    for lo, hi, ops in kinds:
        if max(lo, 1) < hi:
            @pl.when((j >= max(lo, 1)) & (j < hi))
            def _(ops=ops):
                _project_tile(h_ref[...], w_ref, o_ref, tabs, shift, ops)


def norm_proj(x, g, w, rope, *, seq, tm, tn, kinds, layer=None):
    m, d = x.shape
    n = w.shape[-1]
    shift, tabs = rope
    sblk = seq // tm
    tab_spec = pl.BlockSpec((tm, LANES), lambda i, j: (i % sblk, 0))
    return pl.pallas_call(
        functools.partial(_norm_proj_kernel, kinds=kinds, shift=shift, n_tiles=n // tn),
        grid=(m // tm, n // tn),
        in_specs=[pl.BlockSpec((tm, d), lambda i, j: (i, 0)),
                  pl.BlockSpec((1, d), lambda i, j: (0, 0)),
                  _weight_spec(w, tn, layer)] + [tab_spec] * len(tabs),
        out_specs=pl.BlockSpec((tm, tn), lambda i, j: (i, j)),
        out_shape=jax.ShapeDtypeStruct((m, n), BF16),
        scratch_shapes=[pltpu.VMEM((tm, d), BF16)],
        compiler_params=_cparams(("parallel", "arbitrary")),
        name="norm_proj",
    )(x, g.reshape(1, d), w, *tabs)


def _proj_kernel(a_ref, w_ref, *rest, kinds, shift):
    *tabs, o_ref = rest
    _project(a_ref[...], w_ref, o_ref, tabs, shift, kinds)


def proj(a, w, rope, *, seq, tm, tn, kinds, layer=None):
    m, k = a.shape
    n = w.shape[-1]
    shift, tabs = rope
    sblk = seq // tm
    tab_spec = pl.BlockSpec((tm, LANES), lambda i, j: (i % sblk, 0))
    return pl.pallas_call(
        functools.partial(_proj_kernel, kinds=kinds, shift=shift),
        grid=(m // tm, n // tn),
        in_specs=[pl.BlockSpec((tm, k), lambda i, j: (i, 0)),
                  _weight_spec(w, tn, layer)] + [tab_spec] * len(tabs),
        out_specs=pl.BlockSpec((tm, tn), lambda i, j: (i, j)),
        out_shape=jax.ShapeDtypeStruct((m, n), BF16),
        compiler_params=_cparams(("parallel", "arbitrary")),
        name="proj",
    )(a, w, *tabs)


def _out_proj_kernel(a_ref, w_ref, g_ref, x_ref, o_ref):
    piece = a_ref.shape[0] // OUT_ROW_PIECES
    for r in range(OUT_ROW_PIECES):
        rows = slice(r * piece, (r + 1) * piece)
        a = a_ref[rows, :]
        m = jnp.concatenate([_dot(a, w_ref[:, c:c + PROJ_CHUNK])
                             for c in range(0, w_ref.shape[1], PROJ_CHUNK)], axis=1)
        o_ref[rows, :] = x_ref[rows, :] + _rms(m, g_ref[...])


def out_proj_residual(a, w, g, x, *, layer, tm):
    m, k = a.shape
    d = w.shape[-1]
    return pl.pallas_call(
        _out_proj_kernel,
        grid=(m // tm,),
        in_specs=[pl.BlockSpec((tm, k), lambda i: (i, 0)),
                  pl.BlockSpec((None, k, d), lambda i: (layer, 0, 0), pipeline_mode=pl.Buffered(1)),
                  pl.BlockSpec((1, d), lambda i: (0, 0)),
                  pl.BlockSpec((tm, d), lambda i: (i, 0))],
        out_specs=pl.BlockSpec((tm, d), lambda i: (i, 0)),
        out_shape=jax.ShapeDtypeStruct((m, d), F32),
        compiler_params=_cparams(("parallel",)),
        name="out_proj_residual",
    )(a, w, g.reshape(1, d), x)


def _ffn_kernel(x_ref, gpre_ref, wg_ref, wu_ref, wo_ref, gpost_ref, o_ref, h_ref, stat_ref,
                act_ref):
    j = pl.program_id(1)

    def chunk(first):
        h = h_ref[...]
        for c in range(0, act_ref.shape[1], FFN_PIECE):
            cols = slice(c, c + FFN_PIECE)
            gate = _dot(h, wg_ref[:, cols])
            up = _dot(h, wu_ref[:, cols])
            act_ref[:, cols] = (gate / (1.0 + jnp.exp(-gate)) * up).astype(BF16)
        down = _dot(act_ref[...], wo_ref[...].astype(BF16))
        o_ref[...] = down if first else o_ref[...] + down

    @pl.when(j == 0)
    def _():
        h_ref[...] = _rms(x_ref[...], gpre_ref[...]).astype(BF16)
        chunk(first=True)

    @pl.when(j > 0)
    def _():
        chunk(first=False)

    @pl.when(j == pl.num_programs(1) - 1)
    def _():
        o_ref[...] = x_ref[...] + o_ref[...] * _inv_rms(o_ref, stat_ref) * gpost_ref[...]


def ffn(x, gpre, w_in, w_out, gpost, *, layer, tm, tf):
    m, d = x.shape
    dff = w_out.shape[1]
    nf = dff // tf
    return pl.pallas_call(
        _ffn_kernel,
        grid=(m // tm, nf),
        in_specs=[pl.BlockSpec((tm, d), lambda i, j: (i, 0)),
                  pl.BlockSpec((1, d), lambda i, j: (0, 0)),
                  pl.BlockSpec((None, d, tf), lambda i, j: (layer, 0, j)),
                  pl.BlockSpec((None, d, tf), lambda i, j: (layer, 0, nf + j)),
                  pl.BlockSpec((None, tf, d), lambda i, j: (layer, j, 0)),
                  pl.BlockSpec((1, d), lambda i, j: (0, 0))],
        out_specs=pl.BlockSpec((tm, d), lambda i, j: (i, 0)),
        out_shape=jax.ShapeDtypeStruct((m, d), F32),
        scratch_shapes=[pltpu.VMEM((tm, d), BF16), pltpu.VMEM((tm, 1), F32),
                        pltpu.VMEM((tm, tf), BF16)],
        compiler_params=_cparams(("parallel", "arbitrary")),
        name="ffn",
    )(x, gpre.reshape(1, d), w_in, w_in, w_out, gpost.reshape(1, d))


def _win_attn_kernel(sink_ref, q_ref, k_ref, v_ref, o_ref, *, seq, nblk):
    kwin = 3 * A_BLOCK
    cols = A_GROUP * A_BLOCK
    sink_values = jnp.concatenate([jnp.zeros((SINK_ROWS, A_HEAD_DIM), BF16),
                                   jnp.full((SINK_ROWS, A_HEAD_DIM), 1.0 / SINK_ROWS, BF16)], axis=1)
    for jb in range(nblk):
        blk = pl.program_id(1) * nblk + jb
        start = pl.multiple_of(jnp.clip((blk - 1) * A_BLOCK, 0, seq - kwin), A_BLOCK)
        kpos = start + lax.broadcasted_iota(jnp.int32, (kwin, cols), 0)
        qpos = blk * A_BLOCK + (lax.broadcasted_iota(jnp.int32, (kwin, cols), 1) & (A_BLOCK - 1))
        mask = jnp.where(jnp.abs(kpos - qpos) <= WINDOW, 0.0, NEG_BIG)
        qrows = slice(jb * A_BLOCK, (jb + 1) * A_BLOCK)
        for kv in range(A_KV_HEADS):
            heads = range(kv * A_GROUP, (kv + 1) * A_GROUP)
            kvcols = slice(kv * A_HEAD_DIM, (kv + 1) * A_HEAD_DIM)
            q4 = jnp.concatenate(
                [q_ref[0, qrows, h * A_HEAD_DIM:(h + 1) * A_HEAD_DIM] for h in heads], axis=0)
            kw = k_ref[0, pl.ds(start, kwin), kvcols]
            vw = v_ref[0, pl.ds(start, kwin), kvcols]
            st = _dot_nt(kw, q4) + mask
            sink = jnp.concatenate(
                [jnp.full((1, A_BLOCK), sink_ref[h] * LOG2E, F32) for h in heads], axis=1)
            mx = jnp.maximum(jnp.max(st, axis=0, keepdims=True), sink)
            sink_rows = jnp.broadcast_to(jnp.exp2(sink - mx), (SINK_ROWS, cols))
            pt = jnp.concatenate([jnp.exp2(st - mx), sink_rows], axis=0).astype(BF16)
            v_ext = jnp.concatenate(
                [jnp.concatenate([vw, jnp.ones_like(vw)], axis=1), sink_values], axis=0)
            o = lax.dot_general(pt, v_ext, (((0,), (0,)), ((), ())), preferred_element_type=F32)
            o = o[:, :A_HEAD_DIM] / o[:, A_HEAD_DIM:]
            for gi, h in enumerate(heads):
                o_ref[0, qrows, h * A_HEAD_DIM:(h + 1) * A_HEAD_DIM] = (
                    o[gi * A_BLOCK:(gi + 1) * A_BLOCK, :].astype(o_ref.dtype))


def window_attention(qkv, sinks, *, nblk):
    b, s, _ = qkv.shape
    dq = A_HEADS * A_HEAD_DIM
    dkv = A_KV_HEADS * A_HEAD_DIM
    tq = nblk * A_BLOCK
    return pl.pallas_call(
        functools.partial(_win_attn_kernel, seq=s, nblk=nblk),
        grid=(b, s // tq),
        in_specs=[pl.BlockSpec(memory_space=pltpu.SMEM),
                  pl.BlockSpec((1, tq, dq), lambda bi, i: (bi, i, 0)),
                  pl.BlockSpec((1, s, dkv), lambda bi, i: (bi, 0, dq // dkv)),
                  pl.BlockSpec((1, s, dkv), lambda bi, i: (bi, 0, dq // dkv + 1))],
        out_specs=pl.BlockSpec((1, tq, dq), lambda bi, i: (bi, i, 0)),
        out_shape=jax.ShapeDtypeStruct((b, s, dq), BF16),
        compiler_params=_cparams(("parallel", "arbitrary")),
        name="window_attention",
    )(sinks, qkv, qkv, qkv)


def _mla_down_kernel(x_ref, g_ref, wq_ref, wkv_ref, wkr_ref, qn_ref, kvn_ref, *rest, shift):
    *tabs, cq_ref, ckv_ref, kr_ref = rest
    piece = x_ref.shape[0] // NORM_ROW_PIECES
    for r in range(NORM_ROW_PIECES):
        rows = slice(r * piece, (r + 1) * piece)
        h = _rms(x_ref[rows, :], g_ref[...]).astype(BF16)
        cq_ref[rows, :] = _rms(_dot(h, wq_ref[...]), qn_ref[...]).astype(BF16)
        ckv_ref[rows, :] = _rms(_dot(h, wkv_ref[...]), kvn_ref[...]).astype(BF16)
        kr_ref[rows, :] = _rope(_dot(h, wkr_ref[...]), tabs, shift, rows).astype(BF16)


def mla_down(x, g, wq, wkv, wkr, qn, kvn, rope, *, seq, tm):
    m, d = x.shape
    shift, tabs = rope
    sblk = seq // tm
    tab_spec = pl.BlockSpec((tm, LANES), lambda i: (i % sblk, 0))
    full = lambda shape: pl.BlockSpec(shape, lambda i: (0, 0))
    widths = (B_Q_RANK, B_KV_RANK, LANES)
    return pl.pallas_call(
        functools.partial(_mla_down_kernel, shift=shift),
        grid=(m // tm,),
        in_specs=[pl.BlockSpec((tm, d), lambda i: (i, 0)), full((1, d)),
                  full((d, B_Q_RANK)), full((d, B_KV_RANK)), full((d, LANES)),
                  full((1, B_Q_RANK)), full((1, B_KV_RANK))] + [tab_spec] * len(tabs),
        out_specs=[pl.BlockSpec((tm, w), lambda i: (i, 0)) for w in widths],
        out_shape=[jax.ShapeDtypeStruct((m, w), BF16) for w in widths],
        compiler_params=_cparams(("parallel",)),
        name="mla_down",
    )(x, g.reshape(1, d), wq, wkv, wkr, qn.reshape(1, -1), kvn.reshape(1, -1), *tabs)


def _mla_attn_kernel(q_ref, kn_ref, kr_ref, v_ref, o_ref, *, chunk):
    k = jnp.concatenate([kn_ref[0], kr_ref[0]], axis=1)
    v = v_ref[0]
    v_ones = jnp.concatenate([v, jnp.ones_like(v)], axis=1)
    dv = v.shape[1]
    for c in range(q_ref.shape[1] // chunk):
        rows = slice(c * chunk, (c + 1) * chunk)
        s = _dot_nt(q_ref[0, rows, :], k)
        p = jnp.exp2(s - jnp.max(s, axis=-1, keepdims=True))
        o = _dot(p.astype(BF16), v_ones)
        o_ref[0, rows, :] = (o[:, :dv] / o[:, dv:]).astype(o_ref.dtype)


def mla_attention(qcat, kv, kr, *, tq, chunk):
    b, s, _ = qcat.shape
    kv_blocks = (B_NOPE_DIM + B_V_DIM) // B_V_DIM
    return pl.pallas_call(
        functools.partial(_mla_attn_kernel, chunk=chunk),
        grid=(b, B_HEADS, s // tq),
        in_specs=[pl.BlockSpec((1, tq, B_QK_PAD), lambda bi, h, qi: (bi, qi, h)),
                  pl.BlockSpec((1, s, B_NOPE_DIM), lambda bi, h, qi: (bi, 0, kv_blocks * h)),
                  pl.BlockSpec((1, s, LANES), lambda bi, h, qi: (bi, 0, 0)),
                  pl.BlockSpec((1, s, B_V_DIM), lambda bi, h, qi: (bi, 0, kv_blocks * h + 1))],
        out_specs=pl.BlockSpec((1, tq, B_V_DIM), lambda bi, h, qi: (bi, qi, h)),
        out_shape=jax.ShapeDtypeStruct((b, s, B_HEADS * B_V_DIM), BF16),
        compiler_params=_cparams(("parallel", "parallel", "arbitrary")),
        name="mla_attention",
    )(qcat, kv, kr, kv)


N_DR = 2 * NB_H - 1
SLAB_LOW = N_DR - 1
SLAB_HIGH = SLAB_LOW + N_DR
SLAB_NONE = SLAB_HIGH + N_DR


def _nb_slab_plan(rows):
    plan = []
    for rb in range(rows // C_QROWS):
        w0 = min(max(rb * C_QROWS - NB_H // 2, 0), rows - C_KROWS)
        block = []
        for i in range(C_QROWS):
            r = rb * C_QROWS + i
            rs = min(max(r - NB_H // 2, 0), rows - NB_H)
            ids = []
            for pair in range(C_KROWS // 2):
                kr = w0 + 2 * pair
                ok0, ok1 = rs <= kr < rs + NB_H, rs <= kr + 1 < rs + NB_H
                dr = kr - r + NB_H - 1
                ids.append(dr if ok0 and ok1 else SLAB_LOW + dr if ok0
                           else SLAB_HIGH + dr + 1 if ok1 else SLAB_NONE)
            block.append(tuple(ids))
        plan.append(tuple(block))
    return tuple(plan)


def _nb_attn_kernel(q_ref, k_ref, v_ref, t_ref, o_ref, *, plan, rows):
    tq = C_QROWS * GRID_W
    nk = C_KROWS * GRID_W
    for hh in range(C_HEADS_PER_STEP):
        cols = slice(hh * C_HEAD_DIM, (hh + 1) * C_HEAD_DIM)
        biases = {}
        for rb, block in enumerate(plan):
            if block not in biases:
                biases[block] = jnp.concatenate(
                    [jnp.concatenate([t_ref[hh, sid] for sid in ids], axis=1) for ids in block],
                    axis=0)
            start = min(max(rb * C_QROWS - NB_H // 2, 0), rows - C_KROWS) * GRID_W
            qrows = slice(rb * tq, (rb + 1) * tq)
            kw = k_ref[0, start:start + nk, cols]
            vw = v_ref[0, start:start + nk, cols]
            s = _dot_nt(q_ref[0, qrows, cols], kw) + biases[block]
            p = jnp.exp2(s - jnp.max(s, axis=-1, keepdims=True))
            o = _dot(p.astype(BF16), jnp.concatenate([vw, jnp.ones_like(vw)], axis=1))
            o_ref[0, qrows, cols] = (o[:, :C_HEAD_DIM] / o[:, C_HEAD_DIM:]).astype(o_ref.dtype)


def neighborhood_attention(qkv, slabs):
    b, s, _ = qkv.shape
    rows = s // GRID_W
    width = C_HEADS_PER_STEP * C_HEAD_DIM
    nhb = C_HEADS // C_HEADS_PER_STEP
    head_spec = lambda blk0: pl.BlockSpec((1, s, width), lambda h, bi: (bi, 0, blk0 + h))
    return pl.pallas_call(
        functools.partial(_nb_attn_kernel, plan=_nb_slab_plan(rows), rows=rows),
        grid=(nhb, b),
        in_specs=[head_spec(0), head_spec(nhb), head_spec(2 * nhb),
                  pl.BlockSpec((C_HEADS_PER_STEP,) + slabs.shape[1:], lambda h, bi: (h, 0, 0, 0))],
        out_specs=head_spec(0),
        out_shape=jax.ShapeDtypeStruct((b, s, C_HEADS * C_HEAD_DIM), BF16),
        compiler_params=_cparams(("parallel", "arbitrary")),
        name="neighborhood_attention",
    )(qkv, qkv, qkv, slabs)


def _nb_bias_slabs(rel_bias):
    c = np.arange(GRID_W)[:, None]
    kc = np.arange(GRID_W)[None, :]
    cs = np.clip(c - NB_W // 2, 0, GRID_W - NB_W)
    col_ok = (kc >= cs) & (kc < cs + NB_W)
    dc = np.clip(kc - c + NB_W - 1, 0, 2 * NB_W - 2)
    bias = jnp.where(col_ok[None, None], rel_bias.astype(F32)[:, :, dc] * LOG2E, NEG_BIG)
    masked = jnp.full_like(bias, NEG_BIG)
    return jnp.concatenate([jnp.concatenate([bias[:, :-1], bias[:, 1:]], axis=-1),
                            jnp.concatenate([bias, masked], axis=-1),
                            jnp.concatenate([masked, bias], axis=-1),
                            jnp.concatenate([masked[:, :1], masked[:, :1]], axis=-1)], axis=1)


def _rope_angles(seq, dim):
    pos = jnp.arange(seq, dtype=F32)
    inv = ROPE_THETA ** (-jnp.arange(0, dim, 2, dtype=F32) / dim)
    ang = pos[:, None] * inv[None, :]
    return jnp.cos(ang), jnp.sin(ang)


def _rope_adjacent(seq, dim):
    half = dim // 2
    c, sn = _rope_angles(seq, dim)
    cos = jnp.ones((seq, LANES), F32).at[:, :dim].set(jnp.concatenate([c, c], axis=1))
    up = jnp.zeros((seq, LANES), F32).at[:, half:dim].set(sn)
    down = jnp.zeros((seq, LANES), F32).at[:, :half].set(-sn)
    return half, (cos, up, down)


def _rope_swapped(seq, dim):
    half = dim // 2
    c, sn = _rope_angles(seq, dim)
    cos = jnp.ones((seq, LANES), F32).at[:, :half].set(c).at[:, ROPE_SWAP:ROPE_SWAP + half].set(c)
    sin = jnp.zeros((seq, LANES), F32).at[:, :half].set(-sn).at[:, ROPE_SWAP:ROPE_SWAP + half].set(sn)
    return ROPE_SWAP, (cos, sin)


def _pair_split(w, dim):
    half = dim // 2
    gap = ROPE_SWAP - half
    return jnp.concatenate([w[..., :half], w[..., dim:dim + gap], w[..., half:dim],
                            w[..., dim + gap:]], axis=-1)


def kernel(x, pre_mix_norm, post_mix_norm, pre_ffn_norm, post_ffn_norm, a_w_qkv, a_sinks, a_w_o,
           b_w_down, b_q_norm, b_w_uq, b_kv_norm, b_w_ukv, b_w_o, c_w_qkv, c_rel_bias, c_w_o,
           ffn_w_in, ffn_w_out):
    bsz, seq, d = x.shape
    m = bsz * seq
    rope_a = _rope_adjacent(seq, A_ROT_DIM)
    rope_b = _rope_swapped(seq, B_ROPE_DIM)
    w_in, w_out = ffn_w_in.astype(BF16), ffn_w_out
    xf = x.reshape(m, d)
    for i in range(DEPTH):
        kind, slot = i % N_MIXERS, i // N_MIXERS
        if kind == 0:
            ops = (((True, A_HEAD_DIM ** -0.5 * LOG2E),) * A_HEADS + ((True, 1.0),) * A_KV_HEADS
                   + (PLAIN,) * A_KV_HEADS)
            qkv = norm_proj(xf, pre_mix_norm[i], a_w_qkv.astype(BF16), rope_a, seq=seq, tm=TM,
                            tn=len(ops) * LANES, kinds=((0, 1, ops),), layer=slot)
            o = window_attention(qkv.reshape(bsz, seq, -1), a_sinks[slot].astype(F32),
                                 nblk=A_BLOCKS_PER_STEP)
            w_o = a_w_o
        elif kind == 1:
            wd = b_w_down[slot]
            wkr = jnp.pad(wd[:, B_Q_RANK + B_KV_RANK:], ((0, 0), (0, LANES - B_ROPE_DIM)))
            cq, ckv, kr = mla_down(xf, pre_mix_norm[i], wd[:, :B_Q_RANK].astype(BF16),
                                   wd[:, B_Q_RANK:B_Q_RANK + B_KV_RANK].astype(BF16),
                                   _pair_split(wkr, B_ROPE_DIM).astype(BF16),
                                   b_q_norm[slot], b_kv_norm[slot], rope_b, seq=seq, tm=TM)
            wuq = b_w_uq[slot].reshape(B_Q_RANK, B_HEADS, B_NOPE_DIM + B_ROPE_DIM)
            wuq_r = jnp.pad(wuq[:, :, B_NOPE_DIM:], ((0, 0), (0, 0), (0, LANES - B_ROPE_DIM)))
            wuq = jnp.concatenate([wuq[:, :, :B_NOPE_DIM], _pair_split(wuq_r, B_ROPE_DIM)], axis=2)
            scale = (B_NOPE_DIM + B_ROPE_DIM) ** -0.5 * LOG2E
            q_ops = ((False, scale), (True, scale)) * (MLA_TN // B_QK_PAD)
            qcat = proj(cq, wuq.reshape(B_Q_RANK, -1).astype(BF16), rope_b, seq=seq,
                        tm=TM, tn=MLA_TN, kinds=((0, 1, q_ops),))
            kv = proj(ckv, b_w_ukv[slot].astype(BF16), rope_b, seq=seq, tm=TM, tn=MLA_TN,
                      kinds=((0, 1, (PLAIN,) * (MLA_TN // LANES)),))
            o = mla_attention(qcat.reshape(bsz, seq, -1), kv.reshape(bsz, seq, -1),
                              kr.reshape(bsz, seq, -1), tq=seq, chunk=MLA_CHUNK)
            w_o = b_w_o
        else:
            tn = C_HEADS * C_HEAD_DIM
            q_ops = ((False, C_HEAD_DIM ** -0.5 * LOG2E),) * C_HEADS
            qkv = norm_proj(xf, pre_mix_norm[i], c_w_qkv.astype(BF16), rope_a, seq=seq, tm=TM,
                            tn=tn, kinds=((0, 1, q_ops), (1, 3, (PLAIN,) * C_HEADS)), layer=slot)
            o = neighborhood_attention(qkv.reshape(bsz, seq, -1), _nb_bias_slabs(c_rel_bias[slot]))
            w_o = c_w_o
        xf = out_proj_residual(o.reshape(m, -1), w_o.astype(BF16), post_mix_norm[i], xf,
                               layer=slot, tm=TM)
        xf = ffn(xf, pre_ffn_norm[i], w_in, w_out, post_ffn_norm[i], layer=i, tm=TM, tf=FFN_TF)
    return xf.reshape(bsz, seq, d)
```

```python
import functools

import jax
import jax.numpy as jnp
import numpy as np
from jax import lax
from jax.experimental import pallas as pl
from jax.experimental.pallas import tpu as pltpu

D_MODEL = 2048
DEPTH = 4
N_MIXERS = 3
GRID_W = 64
ROPE_THETA = 500000.0
NORM_EPS = 1e-6
D_FF = 5632

A_HEADS = 16
A_KV_HEADS = 4
A_GROUP = A_HEADS // A_KV_HEADS
A_HEAD_DIM = 128
A_ROT_DIM = 32
WINDOW = 128
A_BLOCK = 128

B_HEADS = 16
B_NOPE_DIM = 128
B_ROPE_DIM = 64
B_V_DIM = 128
B_Q_RANK = 512
B_KV_RANK = 512
B_QK_PAD = 256

C_HEADS = 16
C_HEAD_DIM = 128
NB_H = 8
NB_W = 16

LANES = 128
MXU_WIDTH = 256
V7X_VMEM_BYTES = 64 * 1024 * 1024
V7X_VMEM_RESERVE_BYTES = 4 * 1024 * 1024
V7X_VMEM_LIMIT_BYTES = V7X_VMEM_BYTES - V7X_VMEM_RESERVE_BYTES

TM = 1024
PROJ_CHUNK = 2 * MXU_WIDTH
FFN_PIECE = MXU_WIDTH
FFN_TF = 2 * FFN_PIECE
NORM_ROW_PIECES = 2
OUT_ROW_PIECES = 2
MLA_TN = 4096
MLA_CHUNK = 256
SINK_ROWS = 16
A_BLOCKS_PER_STEP = 8
C_QROWS = 8
C_KROWS = 16
C_HEADS_PER_STEP = 2
ROPE_SWAP = LANES // 2

NEG_BIG = -1e30
LOG2E = float(np.log2(np.e))
BF16 = jnp.bfloat16
F32 = jnp.float32
PLAIN = (False, 1.0)


def _cparams(semantics):
    return pltpu.CompilerParams(dimension_semantics=semantics,
                                vmem_limit_bytes=V7X_VMEM_LIMIT_BYTES)


def _rms(xf, g):
    ms = jnp.mean(xf * xf, axis=-1, keepdims=True)
    return xf * lax.rsqrt(ms + NORM_EPS) * g


def _inv_rms(src_ref, stat_ref):
    xf = src_ref[...]
    stat_ref[...] = lax.rsqrt(jnp.mean(xf * xf, axis=-1, keepdims=True) + NORM_EPS)
    return stat_ref[...]


def _dot(a, b):
    return jnp.dot(a, b, preferred_element_type=F32)


def _dot_nt(a, b):
    return lax.dot_general(a, b, (((1,), (1,)), ((), ())), preferred_element_type=F32)


def _rope(y, tabs, shift, rows=slice(None)):
    r = y * tabs[0][rows, :] + pltpu.roll(y, shift, 1) * tabs[1][rows, :]
    if shift != ROPE_SWAP:
        r = r + pltpu.roll(y, LANES - shift, 1) * tabs[2][rows, :]
    return r


def _project_tile(lhs, w_ref, o_ref, tabs, shift, ops, rows=slice(None)):
    per_chunk = PROJ_CHUNK // LANES
    for c0 in range(0, len(ops), per_chunk):
        chunk = slice(c0 * LANES, (c0 + per_chunk) * LANES)
        y = _dot(lhs, w_ref[:, chunk])
        if all(op == PLAIN for op in ops[c0:c0 + per_chunk]):
            o_ref[rows, chunk] = y.astype(o_ref.dtype)
            continue
        for c in range(per_chunk):
            rope, scale = ops[c0 + c]
            yc = y[:, c * LANES:(c + 1) * LANES]
            if scale != 1.0:
                yc = yc * scale
            if rope:
                yc = _rope(yc, tabs, shift, rows)
            o_ref[rows, (c0 + c) * LANES:(c0 + c + 1) * LANES] = yc.astype(o_ref.dtype)


def _project(lhs, w_ref, o_ref, tabs, shift, kinds):
    if len(kinds) == 1:
        _project_tile(lhs, w_ref, o_ref, tabs, shift, kinds[0][2])
        return
    j = pl.program_id(1)
    for lo, hi, ops in kinds:
        @pl.when((j >= lo) & (j < hi))
        def _(ops=ops):
            _project_tile(lhs, w_ref, o_ref, tabs, shift, ops)


def _weight_spec(w, tn, layer):
    k, n = w.shape[-2:]
    lead = () if layer is None else (layer,)
    shape = (None,) * len(lead) + (k, tn)
    if n == tn:
        return pl.BlockSpec(shape, lambda i, j: lead + (0, 0), pipeline_mode=pl.Buffered(1))
    return pl.BlockSpec(shape, lambda i, j: lead + (0, j))


def _norm_proj_kernel(x_ref, g_ref, w_ref, *rest, kinds, shift, n_tiles):
    *tabs, o_ref, h_ref = rest
    j = pl.program_id(1)
    piece = x_ref.shape[0] // NORM_ROW_PIECES

    @pl.when(j == 0)
    def _():
        for r in range(NORM_ROW_PIECES):
            rows = slice(r * piece, (r + 1) * piece)
            h = _rms(x_ref[rows, :], g_ref[...]).astype(BF16)
            if n_tiles > 1:
                h_ref[rows, :] = h
            _project_tile(h, w_ref, o_ref, tabs, shift, kinds[0][2], rows)

    for lo, hi, ops in kinds:
        if max(lo, 1) < hi:
            @pl.when((j >= max(lo, 1)) & (j < hi))
            def _(ops=ops):
                _project_tile(h_ref[...], w_ref, o_ref, tabs, shift, ops)


def norm_proj(x, g, w, rope, *, seq, tm, tn, kinds, layer=None):
    m, d = x.shape
    n = w.shape[-1]
    shift, tabs = rope
    sblk = seq // tm
    tab_spec = pl.BlockSpec((tm, LANES), lambda i, j: (i % sblk, 0))
    return pl.pallas_call(
        functools.partial(_norm_proj_kernel, kinds=kinds, shift=shift, n_tiles=n // tn),
        grid=(m // tm, n // tn),
        in_specs=[pl.BlockSpec((tm, d), lambda i, j: (i, 0)),
                  pl.BlockSpec((1, d), lambda i, j: (0, 0)),
                  _weight_spec(w, tn, layer)] + [tab_spec] * len(tabs),
        out_specs=pl.BlockSpec((tm, tn), lambda i, j: (i, j)),
        out_shape=jax.ShapeDtypeStruct((m, n), BF16),
        scratch_shapes=[pltpu.VMEM((tm, d), BF16)],
        compiler_params=_cparams(("parallel", "arbitrary")),
        name="norm_proj",
    )(x, g.reshape(1, d), w, *tabs)


def _proj_kernel(a_ref, w_ref, *rest, kinds, shift):
    *tabs, o_ref = rest
    _project(a_ref[...], w_ref, o_ref, tabs, shift, kinds)


def proj(a, w, rope, *, seq, tm, tn, kinds, layer=None):
    m, k = a.shape
    n = w.shape[-1]
    shift, tabs = rope
    sblk = seq // tm
    tab_spec = pl.BlockSpec((tm, LANES), lambda i, j: (i % sblk, 0))
    return pl.pallas_call(
        functools.partial(_proj_kernel, kinds=kinds, shift=shift),
        grid=(m // tm, n // tn),
        in_specs=[pl.BlockSpec((tm, k), lambda i, j: (i, 0)),
                  _weight_spec(w, tn, layer)] + [tab_spec] * len(tabs),
        out_specs=pl.BlockSpec((tm, tn), lambda i, j: (i, j)),
        out_shape=jax.ShapeDtypeStruct((m, n), BF16),
        compiler_params=_cparams(("parallel", "arbitrary")),
        name="proj",
    )(a, w, *tabs)


def _out_proj_kernel(a_ref, w_ref, g_ref, x_ref, o_ref):
    piece = a_ref.shape[0] // OUT_ROW_PIECES
    for r in range(OUT_ROW_PIECES):
        rows = slice(r * piece, (r + 1) * piece)
        a = a_ref[rows, :]
        m = jnp.concatenate([_dot(a, w_ref[:, c:c + PROJ_CHUNK])
                             for c in range(0, w_ref.shape[1], PROJ_CHUNK)], axis=1)
        o_ref[rows, :] = x_ref[rows, :] + _rms(m, g_ref[...])


def out_proj_residual(a, w, g, x, *, layer, tm):
    m, k = a.shape
    d = w.shape[-1]
    return pl.pallas_call(
        _out_proj_kernel,
        grid=(m // tm,),
        in_specs=[pl.BlockSpec((tm, k), lambda i: (i, 0)),
                  pl.BlockSpec((None, k, d), lambda i: (layer, 0, 0), pipeline_mode=pl.Buffered(1)),
                  pl.BlockSpec((1, d), lambda i: (0, 0)),
                  pl.BlockSpec((tm, d), lambda i: (i, 0))],
        out_specs=pl.BlockSpec((tm, d), lambda i: (i, 0)),
        out_shape=jax.ShapeDtypeStruct((m, d), F32),
        compiler_params=_cparams(("parallel",)),
        name="out_proj_residual",
    )(a, w, g.reshape(1, d), x)


def _ffn_kernel(x_ref, gpre_ref, wg_ref, wu_ref, wo_ref, gpost_ref, o_ref, h_ref, stat_ref,
                act_ref):
    j = pl.program_id(1)

    def chunk(first):
        h = h_ref[...]
        for c in range(0, act_ref.shape[1], FFN_PIECE):
            cols = slice(c, c + FFN_PIECE)
            gate = _dot(h, wg_ref[:, cols])
            up = _dot(h, wu_ref[:, cols])
            act_ref[:, cols] = (gate / (1.0 + jnp.exp(-gate)) * up).astype(BF16)
        down = _dot(act_ref[...], wo_ref[...].astype(BF16))
        o_ref[...] = down if first else o_ref[...] + down

    @pl.when(j == 0)
    def _():
        h_ref[...] = _rms(x_ref[...], gpre_ref[...]).astype(BF16)
        chunk(first=True)

    @pl.when(j > 0)
    def _():
        chunk(first=False)

    @pl.when(j == pl.num_programs(1) - 1)
    def _():
        o_ref[...] = x_ref[...] + o_ref[...] * _inv_rms(o_ref, stat_ref) * gpost_ref[...]


def ffn(x, gpre, w_in, w_out, gpost, *, layer, tm, tf):
    m, d = x.shape
    dff = w_out.shape[1]
    nf = dff // tf
    return pl.pallas_call(
        _ffn_kernel,
        grid=(m // tm, nf),
        in_specs=[pl.BlockSpec((tm, d), lambda i, j: (i, 0)),
                  pl.BlockSpec((1, d), lambda i, j: (0, 0)),
                  pl.BlockSpec((None, d, tf), lambda i, j: (layer, 0, j)),
                  pl.BlockSpec((None, d, tf), lambda i, j: (layer, 0, nf + j)),
                  pl.BlockSpec((None, tf, d), lambda i, j: (layer, j, 0)),
                  pl.BlockSpec((1, d), lambda i, j: (0, 0))],
        out_specs=pl.BlockSpec((tm, d), lambda i, j: (i, 0)),
        out_shape=jax.ShapeDtypeStruct((m, d), F32),
        scratch_shapes=[pltpu.VMEM((tm, d), BF16), pltpu.VMEM((tm, 1), F32),
                        pltpu.VMEM((tm, tf), BF16)],
        compiler_params=_cparams(("parallel", "arbitrary")),
        name="ffn",
    )(x, gpre.reshape(1, d), w_in, w_in, w_out, gpost.reshape(1, d))


def _win_attn_kernel(sink_ref, q_ref, k_ref, v_ref, o_ref, *, seq, nblk):
    kwin = 3 * A_BLOCK
    cols = A_GROUP * A_BLOCK
    sink_values = jnp.concatenate([jnp.zeros((SINK_ROWS, A_HEAD_DIM), BF16),
                                   jnp.full((SINK_ROWS, A_HEAD_DIM), 1.0 / SINK_ROWS, BF16)], axis=1)
    for jb in range(nblk):
        blk = pl.program_id(1) * nblk + jb
        start = pl.multiple_of(jnp.clip((blk - 1) * A_BLOCK, 0, seq - kwin), A_BLOCK)
        kpos = start + lax.broadcasted_iota(jnp.int32, (kwin, cols), 0)
        qpos = blk * A_BLOCK + (lax.broadcasted_iota(jnp.int32, (kwin, cols), 1) & (A_BLOCK - 1))
        mask = jnp.where(jnp.abs(kpos - qpos) <= WINDOW, 0.0, NEG_BIG)
        qrows = slice(jb * A_BLOCK, (jb + 1) * A_BLOCK)
        for kv in range(A_KV_HEADS):
            heads = range(kv * A_GROUP, (kv + 1) * A_GROUP)
            kvcols = slice(kv * A_HEAD_DIM, (kv + 1) * A_HEAD_DIM)
            q4 = jnp.concatenate(
                [q_ref[0, qrows, h * A_HEAD_DIM:(h + 1) * A_HEAD_DIM] for h in heads], axis=0)
            kw = k_ref[0, pl.ds(start, kwin), kvcols]
            vw = v_ref[0, pl.ds(start, kwin), kvcols]
            st = _dot_nt(kw, q4) + mask
            sink = jnp.concatenate(
                [jnp.full((1, A_BLOCK), sink_ref[h] * LOG2E, F32) for h in heads], axis=1)
            mx = jnp.maximum(jnp.max(st, axis=0, keepdims=True), sink)
            sink_rows = jnp.broadcast_to(jnp.exp2(sink - mx), (SINK_ROWS, cols))
            pt = jnp.concatenate([jnp.exp2(st - mx), sink_rows], axis=0).astype(BF16)
            v_ext = jnp.concatenate(
                [jnp.concatenate([vw, jnp.ones_like(vw)], axis=1), sink_values], axis=0)
            o = lax.dot_general(pt, v_ext, (((0,), (0,)), ((), ())), preferred_element_type=F32)
            o = o[:, :A_HEAD_DIM] / o[:, A_HEAD_DIM:]
            for gi, h in enumerate(heads):
                o_ref[0, qrows, h * A_HEAD_DIM:(h + 1) * A_HEAD_DIM] = (
                    o[gi * A_BLOCK:(gi + 1) * A_BLOCK, :].astype(o_ref.dtype))


def window_attention(qkv, sinks, *, nblk):
    b, s, _ = qkv.shape
    dq = A_HEADS * A_HEAD_DIM
    dkv = A_KV_HEADS * A_HEAD_DIM
    tq = nblk * A_BLOCK
    return pl.pallas_call(
        functools.partial(_win_attn_kernel, seq=s, nblk=nblk),
        grid=(b, s // tq),
        in_specs=[pl.BlockSpec(memory_space=pltpu.SMEM),
                  pl.BlockSpec((1, tq, dq), lambda bi, i: (bi, i, 0)),
                  pl.BlockSpec((1, s, dkv), lambda bi, i: (bi, 0, dq // dkv)),
                  pl.BlockSpec((1, s, dkv), lambda bi, i: (bi, 0, dq // dkv + 1))],
        out_specs=pl.BlockSpec((1, tq, dq), lambda bi, i: (bi, i, 0)),
        out_shape=jax.ShapeDtypeStruct((b, s, dq), BF16),
        compiler_params=_cparams(("parallel", "arbitrary")),
        name="window_attention",
    )(sinks, qkv, qkv, qkv)


def _mla_down_kernel(x_ref, g_ref, wq_ref, wkv_ref, wkr_ref, qn_ref, kvn_ref, *rest, shift):
    *tabs, cq_ref, ckv_ref, kr_ref = rest
    piece = x_ref.shape[0] // NORM_ROW_PIECES
    for r in range(NORM_ROW_PIECES):
        rows = slice(r * piece, (r + 1) * piece)
        h = _rms(x_ref[rows, :], g_ref[...]).astype(BF16)
        cq_ref[rows, :] = _rms(_dot(h, wq_ref[...]), qn_ref[...]).astype(BF16)
        ckv_ref[rows, :] = _rms(_dot(h, wkv_ref[...]), kvn_ref[...]).astype(BF16)
        kr_ref[rows, :] = _rope(_dot(h, wkr_ref[...]), tabs, shift, rows).astype(BF16)


def mla_down(x, g, wq, wkv, wkr, qn, kvn, rope, *, seq, tm):
    m, d = x.shape
    shift, tabs = rope
    sblk = seq // tm
    tab_spec = pl.BlockSpec((tm, LANES), lambda i: (i % sblk, 0))
    full = lambda shape: pl.BlockSpec(shape, lambda i: (0, 0))
    widths = (B_Q_RANK, B_KV_RANK, LANES)
    return pl.pallas_call(
        functools.partial(_mla_down_kernel, shift=shift),
        grid=(m // tm,),
        in_specs=[pl.BlockSpec((tm, d), lambda i: (i, 0)), full((1, d)),
                  full((d, B_Q_RANK)), full((d, B_KV_RANK)), full((d, LANES)),
                  full((1, B_Q_RANK)), full((1, B_KV_RANK))] + [tab_spec] * len(tabs),
        out_specs=[pl.BlockSpec((tm, w), lambda i: (i, 0)) for w in widths],
        out_shape=[jax.ShapeDtypeStruct((m, w), BF16) for w in widths],
        compiler_params=_cparams(("parallel",)),
        name="mla_down",
    )(x, g.reshape(1, d), wq, wkv, wkr, qn.reshape(1, -1), kvn.reshape(1, -1), *tabs)


def _mla_attn_kernel(q_ref, kn_ref, kr_ref, v_ref, o_ref, *, chunk):
    k = jnp.concatenate([kn_ref[0], kr_ref[0]], axis=1)
    v = v_ref[0]
    v_ones = jnp.concatenate([v, jnp.ones_like(v)], axis=1)
    dv = v.shape[1]
    for c in range(q_ref.shape[1] // chunk):
        rows = slice(c * chunk, (c + 1) * chunk)
        s = _dot_nt(q_ref[0, rows, :], k)
        p = jnp.exp2(s - jnp.max(s, axis=-1, keepdims=True))
        o = _dot(p.astype(BF16), v_ones)
        o_ref[0, rows, :] = (o[:, :dv] / o[:, dv:]).astype(o_ref.dtype)


def mla_attention(qcat, kv, kr, *, tq, chunk):
    b, s, _ = qcat.shape
    kv_blocks = (B_NOPE_DIM + B_V_DIM) // B_V_DIM
    return pl.pallas_call(
        functools.partial(_mla_attn_kernel, chunk=chunk),
        grid=(b, B_HEADS, s // tq),
        in_specs=[pl.BlockSpec((1, tq, B_QK_PAD), lambda bi, h, qi: (bi, qi, h)),
                  pl.BlockSpec((1, s, B_NOPE_DIM), lambda bi, h, qi: (bi, 0, kv_blocks * h)),
                  pl.BlockSpec((1, s, LANES), lambda bi, h, qi: (bi, 0, 0)),
                  pl.BlockSpec((1, s, B_V_DIM), lambda bi, h, qi: (bi, 0, kv_blocks * h + 1))],
        out_specs=pl.BlockSpec((1, tq, B_V_DIM), lambda bi, h, qi: (bi, qi, h)),
        out_shape=jax.ShapeDtypeStruct((b, s, B_HEADS * B_V_DIM), BF16),
        compiler_params=_cparams(("parallel", "parallel", "arbitrary")),
        name="mla_attention",
    )(qcat, kv, kr, kv)


N_DR = 2 * NB_H - 1
SLAB_LOW = N_DR - 1
SLAB_HIGH = SLAB_LOW + N_DR
SLAB_NONE = SLAB_HIGH + N_DR


def _nb_slab_plan(rows):
    plan = []
    for rb in range(rows // C_QROWS):
        w0 = min(max(rb * C_QROWS - NB_H // 2, 0), rows - C_KROWS)
        block = []
        for i in range(C_QROWS):
            r = rb * C_QROWS + i
            rs = min(max(r - NB_H // 2, 0), rows - NB_H)
            ids = []
            for pair in range(C_KROWS // 2):
                kr = w0 + 2 * pair
                ok0, ok1 = rs <= kr < rs + NB_H, rs <= kr + 1 < rs + NB_H
                dr = kr - r + NB_H - 1
                ids.append(dr if ok0 and ok1 else SLAB_LOW + dr if ok0
                           else SLAB_HIGH + dr + 1 if ok1 else SLAB_NONE)
            block.append(tuple(ids))
        plan.append(tuple(block))
    return tuple(plan)


def _nb_attn_kernel(q_ref, k_ref, v_ref, t_ref, o_ref, *, plan, rows):
    tq = C_QROWS * GRID_W
    nk = C_KROWS * GRID_W
    for hh in range(C_HEADS_PER_STEP):
        cols = slice(hh * C_HEAD_DIM, (hh + 1) * C_HEAD_DIM)
        biases = {}
        for rb, block in enumerate(plan):
            if block not in biases:
                biases[block] = jnp.concatenate(
                    [jnp.concatenate([t_ref[hh, sid] for sid in ids], axis=1) for ids in block],
                    axis=0)
            start = min(max(rb * C_QROWS - NB_H // 2, 0), rows - C_KROWS) * GRID_W
            qrows = slice(rb * tq, (rb + 1) * tq)
            kw = k_ref[0, start:start + nk, cols]
            vw = v_ref[0, start:start + nk, cols]
            s = _dot_nt(q_ref[0, qrows, cols], kw) + biases[block]
            p = jnp.exp2(s - jnp.max(s, axis=-1, keepdims=True))
            o = _dot(p.astype(BF16), jnp.concatenate([vw, jnp.ones_like(vw)], axis=1))
            o_ref[0, qrows, cols] = (o[:, :C_HEAD_DIM] / o[:, C_HEAD_DIM:]).astype(o_ref.dtype)


def neighborhood_attention(qkv, slabs):
    b, s, _ = qkv.shape
    rows = s // GRID_W
    width = C_HEADS_PER_STEP * C_HEAD_DIM
    nhb = C_HEADS // C_HEADS_PER_STEP
    head_spec = lambda blk0: pl.BlockSpec((1, s, width), lambda h, bi: (bi, 0, blk0 + h))
    return pl.pallas_call(
        functools.partial(_nb_attn_kernel, plan=_nb_slab_plan(rows), rows=rows),
        grid=(nhb, b),
        in_specs=[head_spec(0), head_spec(nhb), head_spec(2 * nhb),
                  pl.BlockSpec((C_HEADS_PER_STEP,) + slabs.shape[1:], lambda h, bi: (h, 0, 0, 0))],
        out_specs=head_spec(0),
        out_shape=jax.ShapeDtypeStruct((b, s, C_HEADS * C_HEAD_DIM), BF16),
        compiler_params=_cparams(("parallel", "arbitrary")),
        name="neighborhood_attention",
    )(qkv, qkv, qkv, slabs)


def _nb_bias_slabs(rel_bias):
    c = np.arange(GRID_W)[:, None]
    kc = np.arange(GRID_W)[None, :]
    cs = np.clip(c - NB_W // 2, 0, GRID_W - NB_W)
    col_ok = (kc >= cs) & (kc < cs + NB_W)
    dc = np.clip(kc - c + NB_W - 1, 0, 2 * NB_W - 2)
    bias = jnp.where(col_ok[None, None], rel_bias.astype(F32)[:, :, dc] * LOG2E, NEG_BIG)
    masked = jnp.full_like(bias, NEG_BIG)
    return jnp.concatenate([jnp.concatenate([bias[:, :-1], bias[:, 1:]], axis=-1),
                            jnp.concatenate([bias, masked], axis=-1),
                            jnp.concatenate([masked, bias], axis=-1),
                            jnp.concatenate([masked[:, :1], masked[:, :1]], axis=-1)], axis=1)


def _rope_angles(seq, dim):
    pos = jnp.arange(seq, dtype=F32)
    inv = ROPE_THETA ** (-jnp.arange(0, dim, 2, dtype=F32) / dim)
    ang = pos[:, None] * inv[None, :]
    return jnp.cos(ang), jnp.sin(ang)


def _rope_adjacent(seq, dim):
    half = dim // 2
    c, sn = _rope_angles(seq, dim)
    cos = jnp.ones((seq, LANES), F32).at[:, :dim].set(jnp.concatenate([c, c], axis=1))
    up = jnp.zeros((seq, LANES), F32).at[:, half:dim].set(sn)
    down = jnp.zeros((seq, LANES), F32).at[:, :half].set(-sn)
    return half, (cos, up, down)


def _rope_swapped(seq, dim):
    half = dim // 2
    c, sn = _rope_angles(seq, dim)
    cos = jnp.ones((seq, LANES), F32).at[:, :half].set(c).at[:, ROPE_SWAP:ROPE_SWAP + half].set(c)
    sin = jnp.zeros((seq, LANES), F32).at[:, :half].set(-sn).at[:, ROPE_SWAP:ROPE_SWAP + half].set(sn)
    return ROPE_SWAP, (cos, sin)


def _pair_split(w, dim):
    half = dim // 2
    gap = ROPE_SWAP - half
    return jnp.concatenate([w[..., :half], w[..., dim:dim + gap], w[..., half:dim],
                            w[..., dim + gap:]], axis=-1)


def kernel(x, pre_mix_norm, post_mix_norm, pre_ffn_norm, post_ffn_norm, a_w_qkv, a_sinks, a_w_o,
           b_w_down, b_q_norm, b_w_uq, b_kv_norm, b_w_ukv, b_w_o, c_w_qkv, c_rel_bias, c_w_o,
           ffn_w_in, ffn_w_out):
    bsz, seq, d = x.shape
    m = bsz * seq
    rope_a = _rope_adjacent(seq, A_ROT_DIM)
    rope_b = _rope_swapped(seq, B_ROPE_DIM)
    w_in, w_out = ffn_w_in.astype(BF16), ffn_w_out
    xf = x.reshape(m, d)
    for i in range(DEPTH):
        kind, slot = i % N_MIXERS, i // N_MIXERS
        if kind == 0:
            ops = (((True, A_HEAD_DIM ** -0.5 * LOG2E),) * A_HEADS + ((True, 1.0),) * A_KV_HEADS
                   + (PLAIN,) * A_KV_HEADS)
            qkv = norm_proj(xf, pre_mix_norm[i], a_w_qkv.astype(BF16), rope_a, seq=seq, tm=TM,
                            tn=len(ops) * LANES, kinds=((0, 1, ops),), layer=slot)
            o = window_attention(qkv.reshape(bsz, seq, -1), a_sinks[slot].astype(F32),
                                 nblk=A_BLOCKS_PER_STEP)
            w_o = a_w_o
        elif kind == 1:
            wd = b_w_down[slot]
            wkr = jnp.pad(wd[:, B_Q_RANK + B_KV_RANK:], ((0, 0), (0, LANES - B_ROPE_DIM)))
            cq, ckv, kr = mla_down(xf, pre_mix_norm[i], wd[:, :B_Q_RANK].astype(BF16),
                                   wd[:, B_Q_RANK:B_Q_RANK + B_KV_RANK].astype(BF16),
                                   _pair_split(wkr, B_ROPE_DIM).astype(BF16),
                                   b_q_norm[slot], b_kv_norm[slot], rope_b, seq=seq, tm=TM)
            wuq = b_w_uq[slot].reshape(B_Q_RANK, B_HEADS, B_NOPE_DIM + B_ROPE_DIM)
            wuq_r = jnp.pad(wuq[:, :, B_NOPE_DIM:], ((0, 0), (0, 0), (0, LANES - B_ROPE_DIM)))
            wuq = jnp.concatenate([wuq[:, :, :B_NOPE_DIM], _pair_split(wuq_r, B_ROPE_DIM)], axis=2)
            scale = (B_NOPE_DIM + B_ROPE_DIM) ** -0.5 * LOG2E
            q_ops = ((False, scale), (True, scale)) * (MLA_TN // B_QK_PAD)
            qcat = proj(cq, wuq.reshape(B_Q_RANK, -1).astype(BF16), rope_b, seq=seq,
                        tm=TM, tn=MLA_TN, kinds=((0, 1, q_ops),))
            kv = proj(ckv, b_w_ukv[slot].astype(BF16), rope_b, seq=seq, tm=TM, tn=MLA_TN,
                      kinds=((0, 1, (PLAIN,) * (MLA_TN // LANES)),))
            o = mla_attention(qcat.reshape(bsz, seq, -1), kv.reshape(bsz, seq, -1),
                              kr.reshape(bsz, seq, -1), tq=seq, chunk=MLA_CHUNK)
            w_o = b_w_o
        else:
            tn = C_HEADS * C_HEAD_DIM
            q_ops = ((False, C_HEAD_DIM ** -0.5 * LOG2E),) * C_HEADS
            qkv = norm_proj(xf, pre_mix_norm[i], c_w_qkv.astype(BF16), rope_a, seq=seq, tm=TM,
                            tn=tn, kinds=((0, 1, q_ops), (1, 3, (PLAIN,) * C_HEADS)), layer=slot)
            o = neighborhood_attention(qkv.reshape(bsz, seq, -1), _nb_bias_slabs(c_rel_bias[slot]))
            w_o = c_w_o
        xf = out_proj_residual(o.reshape(m, -1), w_o.astype(BF16), post_mix_norm[i], xf,
                               layer=slot, tm=TM)
        xf = ffn(xf, pre_ffn_norm[i], w_in, w_out, post_ffn_norm[i], layer=i, tm=TM, tf=FFN_TF)
    return xf.reshape(bsz, seq, d)
```
